```python
import jax, jax.numpy as jnp
from jax import lax
import numpy as np

D_MODEL = 2048
BATCH = 4
SEQ = 4096
DEPTH = 4

GRID_W = 64
CTX_LEN = 256
GLA_HEADS = 4
GLA_DK = 128
GLA_DV = 256
GLA_RANK = 16
GLA_GATE_NORMALIZER = 16.0
GLA_CHUNK = 64
MLSTM_HEADS = 4
MLSTM_DQK = 128
MLSTM_DV = 256
MLSTM_CHUNK = 64
FORGET_BIAS = 3.0
GATE_SOFT_CAP = 15.0
CONV_K = 3
QA = GLA_HEADS * GLA_DK
QB = MLSTM_HEADS * MLSTM_DQK
QK_COLS = 2 * QA + 2 * QB
V_A = GLA_HEADS * GLA_DV
V_B = MLSTM_HEADS * MLSTM_DV
N_GATE_B = 4 * MLSTM_HEADS
W_IN_COLS = QK_COLS + 2 * V_A + 2 * V_B + 2 * GLA_RANK + N_GATE_B + 2 * D_MODEL
D_FF = 5632
N_EXPERTS = 8
TOP_K = 2
N_DENSE = (DEPTH + 1) // 2
N_MOE = DEPTH // 2
EPS = 1e-6

kernel_name = 'hybrid_gla_mlstm_prefix_dit'


def _rmsnorm(x, g):
    xf = x.astype(jnp.float32)
    y = xf * lax.rsqrt(jnp.mean(xf * xf, axis=-1, keepdims=True) + EPS) * g.astype(jnp.float32)
    return y.astype(x.dtype)


def _modulate(u, shift, scale):
    return u * (1 + scale) + shift


def _heads(a, n_heads):
    b, t, _ = a.shape
    return a.reshape(b, t, n_heads, -1).transpose(0, 2, 1, 3)


def _head_rmsnorm(o, g):
    o = o * lax.rsqrt(jnp.mean(o * o, axis=-1, keepdims=True) + EPS)
    b, h, t, d = o.shape
    return o.transpose(0, 2, 1, 3).reshape(b, t, h * d) * g.astype(jnp.float32)


def _soft_cap(z):
    return GATE_SOFT_CAP * jnp.tanh(z / GATE_SOFT_CAP)


def _depthwise_conv(u, w, rows, cols):
    b, t, ch = u.shape
    img = u.reshape(b, rows, cols, ch)
    out = lax.conv_general_dilated(img, w[:, :, None, :].astype(u.dtype), (1, 1), 'SAME',
                                   dimension_numbers=('NHWC', 'HWIO', 'NHWC'),
                                   feature_group_count=ch)
    return out.reshape(b, t, ch)


def _to_chunks(a, size):
    b, h, t = a.shape[:3]
    a = a.reshape((b, h, t // size, size) + a.shape[3:])
    return jnp.moveaxis(a, 2, 0)


def _from_chunks(a):
    a = jnp.moveaxis(a, 0, 2)
    return a.reshape((a.shape[0], a.shape[1], a.shape[2] * a.shape[3]) + a.shape[4:])


def _gla_scan(q, k, v, log_a, state):
    (s0,) = state
    mask = jnp.tril(jnp.ones((GLA_CHUNK, GLA_CHUNK), dtype=bool))

    def step(s, blk):
        qc, kc, vc, ac = blk
        cum = jnp.cumsum(ac, axis=2)
        inter = jnp.einsum('bhtk,bhkv->bhtv', qc * jnp.exp(cum), s)
        decay = jnp.exp(jnp.where(mask[:, :, None],
                                  cum[:, :, :, None, :] - cum[:, :, None, :, :], -jnp.inf))
        scores = jnp.einsum('bhtk,bhsk,bhtsk->bhts', qc, kc, decay)
        intra = jnp.einsum('bhts,bhsv->bhtv', scores, vc)
        c_end = cum[:, :, -1, :]
        s_new = jnp.exp(c_end)[..., None] * s + jnp.einsum(
            'bhsk,bhsv->bhkv', kc * jnp.exp(c_end[:, :, None, :] - cum), vc)
        return s_new, inter + intra

    s_fin, out = lax.scan(step, s0, tuple(_to_chunks(a, GLA_CHUNK) for a in (q, k, v, log_a)))
    return _from_chunks(out), (s_fin,)


def _mlstm_scan(q, k, v, log_i, log_f, state):
    c0, n0, m0 = state
    mask = jnp.tril(jnp.ones((MLSTM_CHUNK, MLSTM_CHUNK), dtype=bool))

    def step(carry, blk):
        cmat, nvec, m = carry
        qc, kc, vc, ic, fc = blk
        cum = jnp.cumsum(fc, axis=-1)
        dlog = jnp.where(mask, cum[..., :, None] - cum[..., None, :] + ic[..., None, :], -jnp.inf)
        inter_log = cum + m[..., None]
        m_t = jnp.maximum(inter_log, jnp.max(dlog, axis=-1))
        w_inter = jnp.exp(inter_log - m_t)
        s = jnp.einsum('bhtk,bhsk->bhts', qc, kc) * jnp.exp(dlog - m_t[..., None])
        num = w_inter[..., None] * jnp.einsum('bhtk,bhkv->bhtv', qc, cmat) + jnp.einsum('bhts,bhsv->bhtv', s, vc)
        den = w_inter * jnp.einsum('bhtk,bhk->bht', qc, nvec) + jnp.sum(s, axis=-1)
        h = num / jnp.maximum(jnp.abs(den), jnp.exp(-m_t))[..., None]
        c_end = cum[..., -1]
        g = c_end[..., None] - cum + ic
        m_new = jnp.maximum(c_end + m, jnp.max(g, axis=-1))
        w_c = jnp.exp(c_end + m - m_new)
        w_k = jnp.exp(g - m_new[..., None])
        c_new = w_c[..., None, None] * cmat + jnp.einsum('bhsk,bhsv->bhkv', kc * w_k[..., None], vc)
        n_new = w_c[..., None] * nvec + jnp.einsum('bhs,bhsk->bhk', w_k, kc)
        return (c_new, n_new, m_new), h

    fin, out = lax.scan(step, (c0, n0, m0),
                        tuple(_to_chunks(a, MLSTM_CHUNK) for a in (q, k, v, log_i, log_f)))
    return _from_chunks(out), fin


def _bidirectional(scan_fn, init_state, ctx_fwd, lat_fwd, ctx_bwd, lat_bwd):
    flip = lambda ops: tuple(jnp.flip(a, axis=2) for a in ops)
    out_ctx_f, state_f = scan_fn(*ctx_fwd, init_state)
    out_lat_f, _ = scan_fn(*lat_fwd, state_f)
    out_ctx_b, state_b = scan_fn(*flip(ctx_bwd), init_state)
    out_lat_b, _ = scan_fn(*flip(lat_bwd), state_b)
    return out_lat_f + jnp.flip(out_lat_b, axis=2), out_ctx_f + jnp.flip(out_ctx_b, axis=2)


def _mixer_inputs(u, w_in_l, conv_w_l, rows, cols):
    p = jnp.einsum('btd,de->bte', u, w_in_l)
    qk = jax.nn.silu(_depthwise_conv(p[..., :QK_COLS], conv_w_l, rows, cols))
    q_a, k_a, q_b, k_b = jnp.split(qk, [QA, 2 * QA, 2 * QA + QB], axis=-1)
    sizes = (V_A, V_B, V_A, V_B, GLA_RANK, GLA_RANK, N_GATE_B, D_MODEL)
    cuts, acc = [QK_COLS], QK_COLS
    for s in sizes:
        acc += s
        cuts.append(acc)
    v_a, v_b, r_a, o_b, lr_f, lr_b, gates_b, g_a, g_b = jnp.split(p, cuts, axis=-1)[1:]
    return {'q_a': q_a, 'k_a': k_a, 'q_b': q_b, 'k_b': k_b, 'v_a': v_a, 'v_b': v_b,
            'r_a': r_a, 'o_b': o_b, 'lr_f': lr_f, 'lr_b': lr_b, 'gates_b': gates_b,
            'g_a': g_a, 'g_b': g_b}


def _scan_operands(p, gla_gate_w_l, gla_gate_b_l, mlstm_gate_b_l):
    f32 = lambda a: a.astype(jnp.float32)
    q_a = _heads(f32(p['q_a']), GLA_HEADS) * GLA_DK ** -0.5
    k_a = _heads(f32(p['k_a']), GLA_HEADS)
    v_a = _heads(f32(p['v_a']), GLA_HEADS)

    def decay(lr, d):
        z = jnp.einsum('btr,rk->btk', f32(lr), f32(gla_gate_w_l[d])) + f32(gla_gate_b_l[d])
        return _heads(jax.nn.log_sigmoid(z) / GLA_GATE_NORMALIZER, GLA_HEADS)

    la_f = decay(p['lr_f'], 0)
    la_b = decay(p['lr_b'], 1)
    q_b = _heads(f32(p['q_b']), MLSTM_HEADS)
    k_b = _heads(f32(p['k_b']), MLSTM_HEADS) * MLSTM_DQK ** -0.5
    v_b = _heads(f32(p['v_b']), MLSTM_HEADS)
    g = _soft_cap(f32(p['gates_b']) + f32(mlstm_gate_b_l))
    b, t, _ = g.shape
    g = g.reshape(b, t, 4, MLSTM_HEADS).transpose(2, 0, 3, 1)
    return {'gla_fwd': (q_a, k_a, v_a, la_f), 'gla_bwd': (q_a, k_a, v_a, la_b),
            'mlstm_fwd': (q_b, k_b, v_b, g[0], jax.nn.log_sigmoid(g[1])),
            'mlstm_bwd': (q_b, k_b, v_b, g[2], jax.nn.log_sigmoid(g[3]))}


def _merge_branches(o_a, o_b, p, gla_out_norm_l, mlstm_out_norm_l, w_up_a_l, w_up_b_l, w_o_l):
    dt = p['v_a'].dtype
    h_a = _head_rmsnorm(o_a, gla_out_norm_l) * jax.nn.silu(p['r_a'].astype(jnp.float32))
    h_b = _head_rmsnorm(o_b, mlstm_out_norm_l) * jax.nn.sigmoid(p['o_b'].astype(jnp.float32))
    up_a = jnp.einsum('btv,vd->btd', h_a.astype(dt), w_up_a_l)
    up_b = jnp.einsum('btv,vd->btd', h_b.astype(dt), w_up_b_l)
    merged = jax.nn.sigmoid(p['g_a']) * up_a + jax.nn.sigmoid(p['g_b']) * up_b
    return jnp.einsum('btd,de->bte', merged, w_o_l)


def _token_mixer(u_lat, u_ctx, w_in_l, conv_w_l, gla_gate_w_l, gla_gate_b_l, mlstm_gate_b_l,
                 gla_out_norm_l, mlstm_out_norm_l, w_up_a_l, w_up_b_l, w_o_l, rows, with_ctx_out):
    p_lat = _mixer_inputs(u_lat, w_in_l, conv_w_l, rows, GRID_W)
    p_ctx = _mixer_inputs(u_ctx, w_in_l, conv_w_l[1:2], 1, u_ctx.shape[1])
    op_lat = _scan_operands(p_lat, gla_gate_w_l, gla_gate_b_l, mlstm_gate_b_l)
    op_ctx = _scan_operands(p_ctx, gla_gate_w_l, gla_gate_b_l, mlstm_gate_b_l)
    bsz = u_lat.shape[0]
    gla_init = (jnp.zeros((bsz, GLA_HEADS, GLA_DK, GLA_DV), jnp.float32),)
    mlstm_init = (jnp.zeros((bsz, MLSTM_HEADS, MLSTM_DQK, MLSTM_DV), jnp.float32),
                  jnp.zeros((bsz, MLSTM_HEADS, MLSTM_DQK), jnp.float32),
                  jnp.zeros((bsz, MLSTM_HEADS), jnp.float32))
    oa_lat, oa_ctx = _bidirectional(_gla_scan, gla_init, op_ctx['gla_fwd'], op_lat['gla_fwd'],
                                    op_ctx['gla_bwd'], op_lat['gla_bwd'])
    ob_lat, ob_ctx = _bidirectional(_mlstm_scan, mlstm_init, op_ctx['mlstm_fwd'], op_lat['mlstm_fwd'],
                                    op_ctx['mlstm_bwd'], op_lat['mlstm_bwd'])
    out_lat = _merge_branches(oa_lat, ob_lat, p_lat, gla_out_norm_l, mlstm_out_norm_l,
                              w_up_a_l, w_up_b_l, w_o_l)
    out_ctx = None
    if with_ctx_out:
        out_ctx = _merge_branches(oa_ctx, ob_ctx, p_ctx, gla_out_norm_l, mlstm_out_norm_l,
                                  w_up_a_l, w_up_b_l, w_o_l)
    return out_lat, out_ctx


def _swiglu(u, w1, w3, w2):
    a = jnp.einsum('btd,df->btf', u, w1)
    g = jnp.einsum('btd,df->btf', u, w3)
    return jnp.einsum('btf,fd->btd', jax.nn.silu(a) * g, w2)


def _moe(u, router_w_l, w1_l, w3_l, w2_l):
    logits = jnp.einsum('btd,de->bte', u, router_w_l).astype(jnp.float32)
    top_v, top_i = lax.top_k(logits, TOP_K)
    w = jax.nn.softmax(top_v, axis=-1)
    comb = jnp.sum(jax.nn.one_hot(top_i, N_EXPERTS, dtype=jnp.float32) * w[..., None], axis=-2)
    comb = comb.astype(u.dtype)
    y = jnp.zeros_like(u)
    for e in range(N_EXPERTS):
        y = y + comb[..., e:e + 1] * _swiglu(u, w1_l[e], w3_l[e], w2_l[e])
    return y


def setup_inputs(seed: int = 0) -> dict:
    key = jax.random.key(seed)
    ks = jax.random.split(key, 26)

    def nrm(k, shape, scale):
        return jax.random.normal(k, shape, jnp.float32) * scale

    gate_base = jnp.repeat(jnp.array([0.0, FORGET_BIAS, 0.0, FORGET_BIAS], jnp.float32), MLSTM_HEADS)
    return {
        'x': nrm(ks[0], (BATCH, SEQ, D_MODEL), 1.0),
        'c': nrm(ks[1], (BATCH, D_MODEL), 1.0),
        'ctx': nrm(ks[2], (BATCH, CTX_LEN, D_MODEL), 1.0),
        'c_ctx': nrm(ks[3], (D_MODEL,), 1.0),
        'ada_w': nrm(ks[4], (DEPTH, D_MODEL, 6 * D_MODEL), 0.5 * D_MODEL ** -0.5),
        'ada_b': nrm(ks[5], (DEPTH, 6 * D_MODEL), 0.02),
        'norm_mix': 1.0 + nrm(ks[6], (DEPTH, D_MODEL), 0.02),
        'norm_ffn': 1.0 + nrm(ks[7], (DEPTH, D_MODEL), 0.02),
        'w_in': nrm(ks[8], (DEPTH, D_MODEL, W_IN_COLS), D_MODEL ** -0.5),
        'conv_w': nrm(ks[9], (DEPTH, CONV_K, CONV_K, QK_COLS), 1.0 / CONV_K),
        'gla_gate_w': nrm(ks[10], (DEPTH, 2, GLA_RANK, QA), GLA_RANK ** -0.5),
        'gla_gate_b': nrm(ks[11], (DEPTH, 2, QA), 0.1),
        'mlstm_gate_b': gate_base + nrm(ks[12], (DEPTH, N_GATE_B), 0.1),
        'gla_out_norm': 1.0 + nrm(ks[13], (DEPTH, V_A), 0.02),
        'mlstm_out_norm': 1.0 + nrm(ks[14], (DEPTH, V_B), 0.02),
        'w_up_a': nrm(ks[15], (DEPTH, V_A, D_MODEL), V_A ** -0.5),
        'w_up_b': nrm(ks[16], (DEPTH, V_B, D_MODEL), V_B ** -0.5),
        'w_o': nrm(ks[17], (DEPTH, D_MODEL, D_MODEL), D_MODEL ** -0.5),
        'ffn_w1': nrm(ks[18], (N_DENSE, D_MODEL, D_FF), D_MODEL ** -0.5),
        'ffn_w3': nrm(ks[19], (N_DENSE, D_MODEL, D_FF), D_MODEL ** -0.5),
        'ffn_w2': nrm(ks[20], (N_DENSE, D_FF, D_MODEL), D_FF ** -0.5),
        'router_w': nrm(ks[21], (N_MOE, D_MODEL, N_EXPERTS), D_MODEL ** -0.5),
        'moe_w1': nrm(ks[22], (N_MOE, N_EXPERTS, D_MODEL, D_FF), D_MODEL ** -0.5),
        'moe_w3': nrm(ks[23], (N_MOE, N_EXPERTS, D_MODEL, D_FF), D_MODEL ** -0.5),
        'moe_w2': nrm(ks[24], (N_MOE, N_EXPERTS, D_FF, D_MODEL), D_FF ** -0.5),
        'norm_final': 1.0 + nrm(ks[25], (D_MODEL,), 0.02),
    }


def reference(x, c, ctx, c_ctx, ada_w, ada_b, norm_mix, norm_ffn, w_in, conv_w, gla_gate_w,
              gla_gate_b, mlstm_gate_b, gla_out_norm, mlstm_out_norm, w_up_a, w_up_b, w_o,
              ffn_w1, ffn_w3, ffn_w2, router_w, moe_w1, moe_w3, moe_w2, norm_final):
    seq = x.shape[1]
    rows = seq // GRID_W
    h, hc = x, ctx
    for l in range(DEPTH):
        last = l == DEPTH - 1
        m_lat = jnp.einsum('bd,de->be', jax.nn.silu(c), ada_w[l]) + ada_b[l]
        m_ctx = jnp.einsum('d,de->e', jax.nn.silu(c_ctx), ada_w[l]) + ada_b[l]
        sh1, sc1, g1, sh2, sc2, g2 = jnp.split(m_lat[:, None, :], 6, axis=-1)
        csh1, csc1, cg1, csh2, csc2, cg2 = jnp.split(m_ctx[None, None, :], 6, axis=-1)

        u_lat = _modulate(_rmsnorm(h, norm_mix[l]), sh1, sc1)
        u_ctx = _modulate(_rmsnorm(hc, norm_mix[l]), csh1, csc1)
        mix_lat, mix_ctx = _token_mixer(u_lat, u_ctx, w_in[l], conv_w[l], gla_gate_w[l], gla_gate_b[l],
                                        mlstm_gate_b[l], gla_out_norm[l], mlstm_out_norm[l],
                                        w_up_a[l], w_up_b[l], w_o[l], rows, not last)
        h = h + g1 * mix_lat
        if not last:
            hc = hc + cg1 * mix_ctx

        v = _modulate(_rmsnorm(h, norm_ffn[l]), sh2, sc2)
        if not last:
            v = jnp.concatenate([v, _modulate(_rmsnorm(hc, norm_ffn[l]), csh2, csc2)], axis=1)
        if l % 2 == 0:
            y = _swiglu(v, ffn_w1[l // 2], ffn_w3[l // 2], ffn_w2[l // 2])
        else:
            y = _moe(v, router_w[l // 2], moe_w1[l // 2], moe_w3[l // 2], moe_w2[l // 2])
        h = h + g2 * y[:, :seq]
        if not last:
            hc = hc + cg2 * y[:, seq:]
    return _rmsnorm(h, norm_final)
```

```python
import functools
import math

import jax
import jax.numpy as jnp
from jax import lax
from jax.experimental import pallas as pl
from jax.experimental.pallas import tpu as pltpu

F32 = jnp.float32
BF16 = jnp.bfloat16
I32 = jnp.int32

GRID_W = 64
GLA_HEADS = 4
GLA_DK = 128
GLA_DV = 256
GLA_RANK = 16
GLA_GATE_NORMALIZER = 16.0
MLSTM_HEADS = 4
MLSTM_DQK = 128
MLSTM_DV = 256
GATE_SOFT_CAP = 15.0
N_GATE_B = 4 * MLSTM_HEADS
TOP_K = 2
EPS = 1e-6
QA = GLA_HEADS * GLA_DK
QB = MLSTM_HEADS * MLSTM_DQK
QK_COLS = 2 * QA + 2 * QB
V_A = GLA_HEADS * GLA_DV
V_B = MLSTM_HEADS * MLSTM_DV

LANES = 128
SUBLANES = 8
SCAN_CHUNK = 256
VMEM_LIMIT = 52 * 1024 * 1024
NEG_BIG = -1e30
HIGHEST = lax.Precision.HIGHEST


def _cparams(*sem):
    return pltpu.CompilerParams(dimension_semantics=sem, vmem_limit_bytes=VMEM_LIMIT)


def _pick(dim, prefs):
    for p in prefs:
        if dim % p == 0:
            return p
    return dim


def _dot(a, b, precision=None):
    return jnp.dot(a, b, preferred_element_type=F32, precision=precision)


def _dot_nt(a, b, precision=None):
    return lax.dot_general(a, b, (((1,), (1,)), ((), ())), preferred_element_type=F32, precision=precision)


def _dot_tn(a, b):
    return lax.dot_general(a, b, (((0,), (0,)), ((), ())), preferred_element_type=F32)


def _silu(x):
    return x * jax.nn.sigmoid(x)


def _ada_kernel(c_ref, w_ref, b_ref, o_ref):
    o_ref[...] = _dot(_silu(c_ref[...]), w_ref[...], HIGHEST) + b_ref[...]


def _ada_mods(cvec, ada_w, ada_b):
    depth, d, n = ada_w.shape
    tn = _pick(n, (1024, 512, 256, 128))
    return pl.pallas_call(
        _ada_kernel,
        grid=(depth, n // tn),
        in_specs=[pl.BlockSpec(cvec.shape, lambda l, j: (0, 0)),
                  pl.BlockSpec((None, d, tn), lambda l, j: (l, 0, j)),
                  pl.BlockSpec((None, 1, tn), lambda l, j: (l, 0, j))],
        out_specs=pl.BlockSpec((None, cvec.shape[0], tn), lambda l, j: (l, 0, j)),
        out_shape=jax.ShapeDtypeStruct((depth, cvec.shape[0], n), F32),
        compiler_params=_cparams("parallel", "parallel"),
        name="ada_mods",
    )(cvec, ada_w, ada_b.reshape(depth, 1, n))


class _Rows:
    def __init__(self, batch, seq, ctx_len):
        self.batch, self.seq, self.ctx_len = batch, seq, ctx_len
        self.n_lat = batch * seq
        self.n_ctx = batch * ctx_len
        self.m = self.n_lat + self.n_ctx

    def row_tile(self, prefs):
        return _pick(math.gcd(self.seq, self.n_ctx), prefs)

    def gid(self, i, tm):
        r0 = i * tm
        return jnp.where(r0 < self.n_lat, r0 // self.seq, self.batch)


def _mod_spec(rows, which, tm, tn, d):
    if tn == d:
        return pl.BlockSpec((None, None, 1, d), lambda i, *_: (which, rows.gid(i, tm), 0, 0))
    return pl.BlockSpec((None, None, 1, tn), lambda i, j, *_: (which, rows.gid(i, tm), 0, j))


def _norm_mod(x, gn, sh, sc):
    y = x * lax.rsqrt(jnp.mean(x * x, axis=-1, keepdims=True) + EPS) * gn
    return y * (1.0 + sc) + sh


def _norm_mod_kernel(x_ref, gn_ref, sh_ref, sc_ref, o_ref):
    o_ref[...] = _norm_mod(x_ref[...], gn_ref[...], sh_ref[...], sc_ref[...]).astype(o_ref.dtype)


def _norm_mod_router_kernel(x_ref, gn_ref, sh_ref, sc_ref, rw_ref, o_ref, ei_ref, ew_ref, *, n_experts):
    u = _norm_mod(x_ref[...], gn_ref[...], sh_ref[...], sc_ref[...])
    o_ref[...] = u
    logits = _dot(u, rw_ref[...], HIGHEST)
    lane_i = lax.broadcasted_iota(I32, logits.shape, 1)
    lane = lane_i.astype(F32)
    lg = jnp.where(lane_i < n_experts, logits, -jnp.inf)
    m1 = jnp.max(lg, axis=-1, keepdims=True)
    i1 = jnp.min(jnp.where(lg == m1, lane, float(LANES)), axis=-1, keepdims=True)
    lg2 = jnp.where(lane == i1, -jnp.inf, lg)
    m2 = jnp.max(lg2, axis=-1, keepdims=True)
    i2 = jnp.min(jnp.where(lg2 == m2, lane, float(LANES)), axis=-1, keepdims=True)
    e = jnp.exp(m2 - m1)
    w1 = 1.0 / (1.0 + e)
    w2 = e / (1.0 + e)
    ei_ref[...] = jnp.where(lane_i == 0, i1, jnp.where(lane_i == 1, i2, 0.0)).astype(I32)
    ew_ref[...] = jnp.where(lane_i == 0, w1, jnp.where(lane_i == 1, w2, 0.0))


def _norm_mod_call(rows, x, gn, mods, which_sh, which_sc, out_dtype, router_w=None):
    m, d = x.shape
    tm = rows.row_tile((256, 128, 64, 32, 16, 8))
    in_specs = [pl.BlockSpec((tm, d), lambda i: (i, 0)),
                pl.BlockSpec((1, d), lambda i: (0, 0)),
                _mod_spec(rows, which_sh, tm, d, d),
                _mod_spec(rows, which_sc, tm, d, d)]
    row_spec = pl.BlockSpec((tm, d), lambda i: (i, 0))
    if router_w is None:
        return pl.pallas_call(
            _norm_mod_kernel, grid=(m // tm,), in_specs=in_specs, out_specs=row_spec,
            out_shape=jax.ShapeDtypeStruct((m, d), out_dtype),
            compiler_params=_cparams("parallel"), name="norm_mod",
        )(x, gn.reshape(1, d), mods, mods)
    n_experts = router_w.shape[1]
    rw = jnp.zeros((d, LANES), F32).at[:, :n_experts].set(router_w)
    lane_spec = pl.BlockSpec((tm, LANES), lambda i: (i, 0))
    return pl.pallas_call(
        functools.partial(_norm_mod_router_kernel, n_experts=n_experts),
        grid=(m // tm,),
        in_specs=in_specs + [pl.BlockSpec((d, LANES), lambda i: (0, 0))],
        out_specs=[row_spec, lane_spec, lane_spec],
        out_shape=[jax.ShapeDtypeStruct((m, d), F32), jax.ShapeDtypeStruct((m, LANES), I32),
                   jax.ShapeDtypeStruct((m, LANES), F32)],
        compiler_params=_cparams("parallel"), name="norm_mod_router",
    )(x, gn.reshape(1, d), mods, mods, rw)


def _final_norm_kernel(x_ref, gn_ref, o_ref):
    x = x_ref[...]
    o_ref[...] = x * lax.rsqrt(jnp.mean(x * x, axis=-1, keepdims=True) + EPS) * gn_ref[...]


def _final_norm(rows, x, gn):
    d = x.shape[1]
    tm = rows.row_tile((256, 128, 64, 32, 16, 8))
    return pl.pallas_call(
        _final_norm_kernel, grid=(rows.n_lat // tm,),
        in_specs=[pl.BlockSpec((tm, d), lambda i: (i, 0)), pl.BlockSpec((1, d), lambda i: (0, 0))],
        out_specs=pl.BlockSpec((tm, d), lambda i: (i, 0)),
        out_shape=jax.ShapeDtypeStruct((rows.n_lat, d), F32),
        compiler_params=_cparams("parallel"), name="final_norm",
    )(x, gn.reshape(1, d))


def _mm_kernel(*refs, nk, act, residual):
    if residual:
        a_ref, w_ref, x_ref, g_ref, o_ref = refs[:5]
        rest = refs[5:]
    else:
        a_ref, w_ref, o_ref = refs[:3]
        rest = refs[3:]

    def finish(r):
        if act == "sigmoid":
            r = jax.nn.sigmoid(r)
        if residual:
            r = x_ref[...] + g_ref[...] * r
        o_ref[...] = r.astype(o_ref.dtype)

    part = _dot(a_ref[...], w_ref[...])
    if nk == 1:
        finish(part)
        return
    acc_ref, = rest
    k = pl.program_id(2)

    @pl.when(k == 0)
    def _():
        acc_ref[...] = part

    @pl.when(k > 0)
    def _():
        acc_ref[...] += part

    @pl.when(k == nk - 1)
    def _():
        finish(acc_ref[...])


def _matmul(a, w, out_dtype, *, act=None, res=None, rows=None, mods=None, which=None):
    m, kdim = a.shape
    n = w.shape[1]
    tm = _pick(m, (1024, 512, 256, 128, 64, 32, 16, 8)) if rows is None else rows.row_tile((1024, 512, 256, 128, 64, 32, 16, 8))
    tn = _pick(n, (1024, 512, 256, 128))
    tk = kdim if kdim <= 2048 else _pick(kdim, (2816, 2048, 1024, 512, 256, 128))
    nk = kdim // tk
    residual = res is not None
    in_specs = [pl.BlockSpec((tm, tk), lambda i, j, k: (i, k)),
                pl.BlockSpec((tk, tn), lambda i, j, k: (k, j))]
    args = [a, w]
    if residual:
        in_specs += [pl.BlockSpec((tm, tn), lambda i, j, k: (i, j)), _mod_spec(rows, which, tm, tn, None)]
        args += [res, mods]
    return pl.pallas_call(
        functools.partial(_mm_kernel, nk=nk, act=act, residual=residual),
        grid=(m // tm, n // tn, nk),
        in_specs=in_specs,
        out_specs=pl.BlockSpec((tm, tn), lambda i, j, k: (i, j)),
        out_shape=jax.ShapeDtypeStruct((m, n), out_dtype),
        scratch_shapes=[pltpu.VMEM((tm, tn), F32)] if nk > 1 else [],
        compiler_params=_cparams("parallel", "parallel", "arbitrary"),
        name="matmul",
    )(*args)


def _swiglu_kernel(a_ref, w1_ref, w3_ref, o_ref):
    a = a_ref[...]
    o_ref[...] = (_silu(_dot(a, w1_ref[...])) * _dot(a, w3_ref[...])).astype(o_ref.dtype)


def _swiglu_up(a, w1, w3):
    m, d = a.shape
    f = w1.shape[1]
    tm = _pick(m, (1024, 512, 256, 128, 64, 32, 16, 8))
    tn = _pick(f, (512, 256, 128))
    return pl.pallas_call(
        _swiglu_kernel, grid=(m // tm, f // tn),
        in_specs=[pl.BlockSpec((tm, d), lambda i, j: (i, 0)),
                  pl.BlockSpec((d, tn), lambda i, j: (0, j)),
                  pl.BlockSpec((d, tn), lambda i, j: (0, j))],
        out_specs=pl.BlockSpec((tm, tn), lambda i, j: (i, j)),
        out_shape=jax.ShapeDtypeStruct((m, f), BF16),
        compiler_params=_cparams("parallel", "parallel"), name="swiglu_up",
    )(a, w1, w3)


def _up_merge_kernel(ha_ref, hb_ref, wa_ref, wb_ref, sga_ref, sgb_ref, o_ref):
    up_a = _dot(ha_ref[...], wa_ref[...])
    up_b = _dot(hb_ref[...], wb_ref[...])
    o_ref[...] = (sga_ref[...] * up_a + sgb_ref[...] * up_b).astype(o_ref.dtype)


def _up_merge(ha, hb, wa, wb, sg):
    m, va = ha.shape
    d = wa.shape[1]
    tm = _pick(m, (1024, 512, 256, 128, 64, 32, 16, 8))
    tn = _pick(d, (1024, 512, 256, 128))
    nj = d // tn
    return pl.pallas_call(
        _up_merge_kernel, grid=(m // tm, nj),
        in_specs=[pl.BlockSpec((tm, va), lambda i, j: (i, 0)),
                  pl.BlockSpec((tm, hb.shape[1]), lambda i, j: (i, 0)),
                  pl.BlockSpec((va, tn), lambda i, j: (0, j)),
                  pl.BlockSpec((hb.shape[1], tn), lambda i, j: (0, j)),
                  pl.BlockSpec((tm, tn), lambda i, j: (i, j)),
                  pl.BlockSpec((tm, tn), lambda i, j: (i, j + nj))],
        out_specs=pl.BlockSpec((tm, tn), lambda i, j: (i, j)),
        out_shape=jax.ShapeDtypeStruct((m, d), BF16),
        compiler_params=_cparams("parallel", "parallel"), name="up_merge",
    )(ha, hb, wa, wb, sg, sg)


def _conv_taps(xs, w, dy, not_first, not_last):
    n = xs.shape[0]
    left = jnp.where(not_first, pltpu.roll(xs, 1, 0), 0.0)
    right = jnp.where(not_last, pltpu.roll(xs, n - 1, 0), 0.0)
    return w[3 * dy:3 * dy + 1] * left + w[3 * dy + 1:3 * dy + 2] * xs + w[3 * dy + 2:3 * dy + 3] * right


def _conv_kernel(up_ref, x_ref, dn_ref, w_ref, s_ref, o_ref, pad_ref, *, rows, rb, strip):
    r0 = pl.program_id(0) * rb
    w = w_ref[...]

    @pl.when(r0 < rows.n_lat)
    def _():
        tc = x_ref.shape[-1]
        at_start = lax.rem(r0, rows.seq) == 0
        at_end = lax.rem(r0 + rb, rows.seq) == 0
        pad_ref[pl.ds(0, GRID_W), :] = jnp.where(at_start, 0.0, up_ref[...])
        pad_ref[pl.ds(GRID_W + rb, GRID_W), :] = jnp.where(at_end, 0.0, dn_ref[...])
        pad_ref[pl.ds(GRID_W, rb), :] = x_ref[...]
        col = lax.broadcasted_iota(I32, (strip, tc), 0) % GRID_W
        not_first = col != 0
        not_last = col != GRID_W - 1
        for s in range(rb // strip):
            acc = jnp.zeros((strip, tc), F32)
            for dy in range(3):
                xs = pad_ref[pl.ds(s * strip + dy * GRID_W, strip), :]
                acc = acc + _conv_taps(xs, w, dy, not_first, not_last)
            o_ref[pl.ds(s * strip, strip), :] = _silu(acc) * s_ref[...]

    @pl.when(r0 >= rows.n_lat)
    def _():
        x = x_ref[...]
        pos = lax.broadcasted_iota(I32, x.shape, 0) % rows.ctx_len
        acc = _conv_taps(x, w, 1, pos != 0, pos != rows.ctx_len - 1)
        o_ref[...] = _silu(acc) * s_ref[...]


def _conv_silu(rows, p_qk, conv_w, scale):
    m, c = p_qk.shape
    tc = _pick(c, (256, 128))
    rb = rows.n_ctx
    strip = _pick(rb, (512, 256, 128, 64))
    per = rb // GRID_W
    last = m // GRID_W - 1
    return pl.pallas_call(
        functools.partial(_conv_kernel, rows=rows, rb=rb, strip=strip),
        grid=(m // rb, c // tc),
        in_specs=[pl.BlockSpec((GRID_W, tc), lambda i, j: (jnp.maximum(i * per - 1, 0), j)),
                  pl.BlockSpec((rb, tc), lambda i, j: (i, j)),
                  pl.BlockSpec((GRID_W, tc), lambda i, j: (jnp.minimum((i + 1) * per, last), j)),
                  pl.BlockSpec((9, tc), lambda i, j: (0, j)),
                  pl.BlockSpec((1, tc), lambda i, j: (0, j))],
        out_specs=pl.BlockSpec((rb, tc), lambda i, j: (i, j)),
        out_shape=jax.ShapeDtypeStruct((m, c), F32),
        scratch_shapes=[pltpu.VMEM((rb + 2 * GRID_W, tc), F32)],
        compiler_params=_cparams("parallel", "parallel"), name="conv_silu",
    )(p_qk, p_qk, p_qk, conv_w.reshape(9, c), scale)


def _causal_masks(n):
    t = lax.broadcasted_iota(I32, (n, n), 0)
    s = lax.broadcasted_iota(I32, (n, n), 1)
    return t, s, (s <= t, s >= t)


def _block_ref_rows(x, blk, row):
    n, c = x.shape
    if blk >= SUBLANES:
        x3 = x.reshape(n // blk, blk, c)
        return jnp.broadcast_to(x3[:, row:row + 1, :], x3.shape).reshape(n, c)
    x3 = x.reshape(n // SUBLANES, SUBLANES, c)
    sub = lax.broadcasted_iota(I32, x3.shape, 1)
    out = jnp.zeros_like(x3)
    for g in range(SUBLANES // blk):
        r = g * blk + row
        out = jnp.where(sub // blk == g, jnp.broadcast_to(x3[:, r:r + 1, :], x3.shape), out)
    return out.reshape(n, c)


def _gla_direction(d, q, k, v, lr, wg, bg, s_ref, t_idx, s_idx, causal):
    n = q.shape[0]
    z = _dot(lr, wg, HIGHEST) + bg
    la = jax.nn.log_sigmoid(z) * (1.0 / GLA_GATE_NORMALIZER)
    cum = _dot(causal.astype(F32), la, HIGHEST)
    total = cum[n - 1:n] if d == 0 else cum[0:1]
    state = s_ref[...]
    out = _dot((q * jnp.exp(cum)).astype(BF16), state.astype(BF16))
    lev = jnp.where(causal, 31 - lax.clz(t_idx ^ s_idx), -2)
    scores = jnp.where(lev == -1, _dot_nt(q.astype(BF16), k.astype(BF16)), 0.0)
    for l in range(n.bit_length() - 1):
        half = 1 << l
        ref = _block_ref_rows(cum, 2 * half, half - 1 if d == 0 else half)
        e = jnp.exp(-jnp.abs(cum - ref))
        p = _dot_nt((q * e).astype(BF16), (k * e).astype(BF16))
        scores = jnp.where(lev == l, p, scores)
    out = out + _dot(scores.astype(BF16), v)
    k_out = (k * jnp.exp(total - cum)).astype(BF16)
    dk = state.shape[0]
    et = jnp.broadcast_to(jnp.exp(total), (dk, dk)).T
    scale = jnp.concatenate([et] * (state.shape[1] // dk), axis=1)
    s_ref[...] = scale * state + _dot_tn(k_out, v)
    return out


def _gla_kernel(qf, kf, vf, smf, qb, kb, vb, smb, wg_ref, bg_ref, of_ref, ob_ref, sf_ref, sb_ref):
    @pl.when(pl.program_id(2) == 0)
    def _():
        sf_ref[...] = jnp.zeros_like(sf_ref)
        sb_ref[...] = jnp.zeros_like(sb_ref)

    n = qf.shape[0]
    t_idx, s_idx, causal = _causal_masks(n)
    dirs = ((qf, kf, vf, smf, of_ref, sf_ref), (qb, kb, vb, smb, ob_ref, sb_ref))
    for d, (q, k, v, sm, o_ref, s_ref) in enumerate(dirs):
        lr = sm[:, d * GLA_RANK:(d + 1) * GLA_RANK]
        o_ref[...] = _gla_direction(d, q[...], k[...], v[...], lr, wg_ref[d], bg_ref[d], s_ref,
                                    t_idx, s_idx, causal[d])


def _soft_cap(z):
    return GATE_SOFT_CAP * jnp.tanh(z * (1.0 / GATE_SOFT_CAP))


def _mlstm_direction(d, head, q, k, v, g_col, g_row, c_ref, m_ref, causal):
    n = q.shape[0]
    tri = causal.astype(F32)
    cum_c_all = _dot(tri, jax.nn.log_sigmoid(g_col), HIGHEST)
    cum_r_all = _dot_nt(jax.nn.log_sigmoid(g_row), tri, HIGHEST)
    lane = lax.broadcasted_iota(I32, (1, N_GATE_B), 1)
    sub = lax.broadcasted_iota(I32, (N_GATE_B, 1), 0)
    i_idx = 2 * MLSTM_HEADS * d + head
    f_idx = i_idx + MLSTM_HEADS
    pick_c = lambda a, idx: jnp.sum(jnp.where(lane == idx, a, 0.0), axis=1, keepdims=True)
    pick_r = lambda a, idx: jnp.sum(jnp.where(sub == idx, a, 0.0), axis=0, keepdims=True)
    i_c, cum_c = pick_c(g_col, i_idx), pick_c(cum_c_all, f_idx)
    i_r, cum_r = pick_r(g_row, i_idx), pick_r(cum_r_all, f_idx)
    total = cum_c[n - 1:n] if d == 0 else cum_c[0:1]
    m_prev = m_ref[...]
    dlog = jnp.where(causal, cum_c - cum_r + i_r, NEG_BIG)
    inter_log = cum_c + m_prev
    m_t = jnp.maximum(inter_log, jnp.max(dlog, axis=1, keepdims=True))
    w_inter = jnp.exp(inter_log - m_t)
    qb16 = q.astype(BF16)
    s = _dot_nt(qb16, k.astype(BF16)) * jnp.exp(dlog - m_t)
    one_col = (lax.broadcasted_iota(I32, (n, LANES), 1) == 0).astype(BF16)
    v_ext = jnp.concatenate([v, one_col], axis=1)
    state = c_ref[...]
    acc = w_inter * _dot(qb16, state.astype(BF16)) + _dot(s.astype(BF16), v_ext)
    dv = v.shape[1]
    den = acc[:, dv:dv + 1]
    out = acc[:, :dv] / jnp.maximum(jnp.abs(den), jnp.exp(-m_t))
    g = total - cum_c + i_c
    m_new = jnp.maximum(total + m_prev, jnp.max(g, axis=0, keepdims=True))
    w_c = jnp.exp(total + m_prev - m_new)
    w_k = jnp.exp(g - m_new)
    c_ref[...] = w_c * state + _dot_tn((k * w_k).astype(BF16), v_ext)
    m_ref[...] = m_new
    return out


def _mlstm_kernel(qf, kf, vf, gcf, grf, qb, kb, vb, gcb, grb, brow_ref, bcol_ref,
                  of_ref, ob_ref, cf_ref, cb_ref, mf_ref, mb_ref):
    @pl.when(pl.program_id(2) == 0)
    def _():
        cf_ref[...] = jnp.zeros_like(cf_ref)
        cb_ref[...] = jnp.zeros_like(cb_ref)
        mf_ref[...] = jnp.zeros_like(mf_ref)
        mb_ref[...] = jnp.zeros_like(mb_ref)

    head = pl.program_id(1)
    n = qf.shape[0]
    _, _, causal = _causal_masks(n)
    g0 = 2 * GLA_RANK
    dirs = ((qf, kf, vf, gcf, grf, of_ref, cf_ref, mf_ref), (qb, kb, vb, gcb, grb, ob_ref, cb_ref, mb_ref))
    for d, (q, k, v, gc, gr, o_ref, c_ref, m_ref) in enumerate(dirs):
        g_col = _soft_cap(gc[:, g0:g0 + N_GATE_B] + brow_ref[...])
        g_row = _soft_cap(gr[...] + bcol_ref[...])
        o_ref[...] = _mlstm_direction(d, head, q[...], k[...], v[...], g_col, g_row, c_ref, m_ref, causal[d])


def _scan_row_maps(rows, chunk):
    lat_chunks = rows.seq // chunk
    ctx_chunks = rows.ctx_len // chunk
    ctx0 = rows.n_lat // chunk

    def fwd(b, s):
        return jnp.where(s < ctx_chunks, ctx0 + b * ctx_chunks + s, b * lat_chunks + (s - ctx_chunks))

    def bwd(b, s):
        return jnp.where(s < ctx_chunks, ctx0 + b * ctx_chunks + (ctx_chunks - 1 - s),
                         b * lat_chunks + (lat_chunks - 1 - (s - ctx_chunks)))

    return fwd, bwd, ctx_chunks + lat_chunks


def _gla_scan(rows, qk, v, small, gate_w, gate_b):
    m = qk.shape[0]
    n = SCAN_CHUNK
    fwd, bwd, steps = _scan_row_maps(rows, n)
    hq = QA // GLA_DK

    def specs(rmap):
        return [pl.BlockSpec((n, GLA_DK), lambda b, h, s: (rmap(b, s), h)),
                pl.BlockSpec((n, GLA_DK), lambda b, h, s: (rmap(b, s), hq + h)),
                pl.BlockSpec((n, GLA_DV), lambda b, h, s: (rmap(b, s), h)),
                pl.BlockSpec((n, LANES), lambda b, h, s: (rmap(b, s), 0))]

    out_spec = lambda rmap: pl.BlockSpec((n, GLA_DV), lambda b, h, s: (rmap(b, s), h))
    out_sds = jax.ShapeDtypeStruct((m, V_A), F32)
    return pl.pallas_call(
        _gla_kernel,
        grid=(rows.batch, GLA_HEADS, steps),
        in_specs=specs(fwd) + specs(bwd) + [
            pl.BlockSpec((2, GLA_RANK, GLA_DK), lambda b, h, s: (0, 0, h)),
            pl.BlockSpec((2, 1, GLA_DK), lambda b, h, s: (0, 0, h))],
        out_specs=[out_spec(fwd), out_spec(bwd)],
        out_shape=[out_sds, out_sds],
        scratch_shapes=[pltpu.VMEM((GLA_DK, GLA_DV), F32), pltpu.VMEM((GLA_DK, GLA_DV), F32)],
        compiler_params=_cparams("parallel", "parallel", "arbitrary"), name="gla_scan",
    )(qk, qk, v, small, qk, qk, v, small, gate_w, gate_b.reshape(2, 1, QA))


def _mlstm_scan(rows, qk, v, small, gates_t, gate_b):
    m = qk.shape[0]
    n = SCAN_CHUNK
    fwd, bwd, steps = _scan_row_maps(rows, n)
    q0 = 2 * QA // MLSTM_DQK
    k0 = q0 + QB // MLSTM_DQK
    v0 = V_A // MLSTM_DV

    def specs(rmap):
        return [pl.BlockSpec((n, MLSTM_DQK), lambda b, h, s: (rmap(b, s), q0 + h)),
                pl.BlockSpec((n, MLSTM_DQK), lambda b, h, s: (rmap(b, s), k0 + h)),
                pl.BlockSpec((n, MLSTM_DV), lambda b, h, s: (rmap(b, s), v0 + h)),
                pl.BlockSpec((n, LANES), lambda b, h, s: (rmap(b, s), 0)),
                pl.BlockSpec((N_GATE_B, n), lambda b, h, s: (0, rmap(b, s)))]

    out_spec = lambda rmap: pl.BlockSpec((n, MLSTM_DV), lambda b, h, s: (rmap(b, s), h))
    out_sds = jax.ShapeDtypeStruct((m, V_B), F32)
    ext = MLSTM_DV + LANES
    return pl.pallas_call(
        _mlstm_kernel,
        grid=(rows.batch, MLSTM_HEADS, steps),
        in_specs=specs(fwd) + specs(bwd) + [
            pl.BlockSpec((1, N_GATE_B), lambda b, h, s: (0, 0)),
            pl.BlockSpec((N_GATE_B, 1), lambda b, h, s: (0, 0))],
        out_specs=[out_spec(fwd), out_spec(bwd)],
        out_shape=[out_sds, out_sds],
        scratch_shapes=[pltpu.VMEM((MLSTM_DQK, ext), F32), pltpu.VMEM((MLSTM_DQK, ext), F32),
                        pltpu.VMEM((1, 1), F32), pltpu.VMEM((1, 1), F32)],
        compiler_params=_cparams("parallel", "parallel", "arbitrary"), name="mlstm_scan",
    )(qk, qk, v, small, gates_t, qk, qk, v, small, gates_t,
      gate_b.reshape(1, N_GATE_B), gate_b.reshape(N_GATE_B, 1))


def _mix_prep_kernel(oaf, oab, obf, obb, ro_ref, gna_ref, gnb_ref, ha_ref, hb_ref):
    branches = ((oaf, oab, gna_ref, ha_ref, 0, GLA_HEADS, GLA_DV, _silu),
                (obf, obb, gnb_ref, hb_ref, V_A, MLSTM_HEADS, MLSTM_DV, jax.nn.sigmoid))
    for of, ob, gn_ref, h_ref, off, heads, dv, gate_fn in branches:
        for h in range(heads):
            sl = slice(h * dv, (h + 1) * dv)
            o = of[:, sl] + ob[:, sl]
            y = o * lax.rsqrt(jnp.mean(o * o, axis=-1, keepdims=True) + EPS) * gn_ref[:, sl]
            gate = gate_fn(ro_ref[:, off + h * dv:off + (h + 1) * dv])
            h_ref[:, sl] = (y * gate).astype(h_ref.dtype)


def _mix_prep(oaf, oab, obf, obb, ro, gna, gnb):
    m = ro.shape[0]
    tm = _pick(m, (256, 128, 64, 32, 16, 8))
    row = lambda c: pl.BlockSpec((tm, c), lambda i: (i, 0))
    vec = lambda c: pl.BlockSpec((1, c), lambda i: (0, 0))
    return pl.pallas_call(
        _mix_prep_kernel, grid=(m // tm,),
        in_specs=[row(V_A), row(V_A), row(V_B), row(V_B), row(V_A + V_B), vec(V_A), vec(V_B)],
        out_specs=[row(V_A), row(V_B)],
        out_shape=[jax.ShapeDtypeStruct((m, V_A), BF16), jax.ShapeDtypeStruct((m, V_B), BF16)],
        compiler_params=_cparams("parallel"), name="mix_prep",
    )(oaf, oab, obf, obb, ro, gna.reshape(1, V_A), gnb.reshape(1, V_B))


def _gather_rows_kernel(src_ref, v_hbm, o_hbm, sem, *, tile):
    base = pl.program_id(0) * tile

    def copy(j):
        return pltpu.make_async_copy(v_hbm.at[pl.ds(src_ref[0, j], 1)], o_hbm.at[pl.ds(base + j, 1)], sem)

    def start(j, c):
        copy(j).start()
        return c

    def wait(j, c):
        copy(j).wait()
        return c

    lax.fori_loop(0, tile, start, 0)
    lax.fori_loop(0, tile, wait, 0)


def _gather_rows(v, src, tile):
    n_tiles = src.shape[0]
    d = v.shape[1]
    return pl.pallas_call(
        functools.partial(_gather_rows_kernel, tile=tile),
        grid=(n_tiles,),
        in_specs=[pl.BlockSpec((None, 1, tile), lambda i: (i, 0, 0), memory_space=pltpu.SMEM),
                  pl.BlockSpec(memory_space=pl.ANY)],
        out_specs=pl.BlockSpec(memory_space=pl.ANY),
        out_shape=jax.ShapeDtypeStruct((n_tiles * tile, d), v.dtype),
        scratch_shapes=[pltpu.SemaphoreType.DMA(())],
        compiler_params=_cparams("arbitrary"), name="moe_gather",
    )(src, v)


def _expert_up_kernel(te_ref, na_ref, a_ref, w1_ref, w3_ref, o_ref, abuf):
    del te_ref
    i = pl.program_id(0)

    @pl.when(i < na_ref[0])
    def _():
        @pl.when(pl.program_id(1) == 0)
        def _():
            abuf[...] = a_ref[...].astype(BF16)

        a = abuf[...]
        o_ref[...] = (_silu(_dot(a, w1_ref[...])) * _dot(a, w3_ref[...])).astype(o_ref.dtype)

    @pl.when(i >= na_ref[0])
    def _():
        o_ref[...] = jnp.zeros_like(o_ref)


def _expert_up(xs, w1, w3, tile_expert, n_active, tm):
    p, d = xs.shape
    f = w1.shape[2]
    tn = _pick(f, (512, 256, 128))
    act = lambda i, na: jnp.where(i < na[0], i, 0)
    wmap = lambda i, j, te, na: (te[i], 0, jnp.where(i < na[0], j, 0))
    return pl.pallas_call(
        _expert_up_kernel,
        grid_spec=pltpu.PrefetchScalarGridSpec(
            num_scalar_prefetch=2, grid=(p // tm, f // tn),
            in_specs=[pl.BlockSpec((tm, d), lambda i, j, te, na: (act(i, na), 0)),
                      pl.BlockSpec((None, d, tn), wmap),
                      pl.BlockSpec((None, d, tn), wmap)],
            out_specs=pl.BlockSpec((tm, tn), lambda i, j, te, na: (i, j)),
            scratch_shapes=[pltpu.VMEM((tm, d), BF16)]),
        out_shape=jax.ShapeDtypeStruct((p, f), BF16),
        compiler_params=_cparams("parallel", "arbitrary"), name="expert_up",
    )(tile_expert, n_active, xs, w1, w3)


def _expert_down_kernel(te_ref, na_ref, a_ref, w_ref, o_ref, acc_ref, *, nk):
    del te_ref
    i = pl.program_id(0)
    k = pl.program_id(2)

    @pl.when(i < na_ref[0])
    def _():
        part = _dot(a_ref[...], w_ref[...])

        @pl.when(k == 0)
        def _():
            acc_ref[...] = part

        @pl.when(k > 0)
        def _():
            acc_ref[...] += part

        @pl.when(k == nk - 1)
        def _():
            o_ref[...] = acc_ref[...]

    @pl.when(i >= na_ref[0])
    def _():
        o_ref[...] = jnp.zeros_like(o_ref)


def _expert_down(hs, w2, tile_expert, n_active, tm):
    p, f = hs.shape
    d = w2.shape[2]
    tn = _pick(d, (1024, 512, 256, 128))
    tk = f if f <= 2048 else _pick(f, (2816, 2048, 1024, 512, 256, 128))
    nk = f // tk
    act = lambda i, na: jnp.where(i < na[0], i, 0)
    return pl.pallas_call(
        functools.partial(_expert_down_kernel, nk=nk),
        grid_spec=pltpu.PrefetchScalarGridSpec(
            num_scalar_prefetch=2, grid=(p // tm, d // tn, nk),
            in_specs=[pl.BlockSpec((tm, tk), lambda i, j, k, te, na: (act(i, na), jnp.where(i < na[0], k, 0))),
                      pl.BlockSpec((None, tk, tn),
                                   lambda i, j, k, te, na: (te[i], jnp.where(i < na[0], k, 0), jnp.where(i < na[0], j, 0)))],
            out_specs=pl.BlockSpec((tm, tn), lambda i, j, k, te, na: (i, j)),
            scratch_shapes=[pltpu.VMEM((tm, tn), F32)]),
        out_shape=jax.ShapeDtypeStruct((p, d), F32),
        compiler_params=_cparams("parallel", "parallel", "arbitrary"), name="expert_down",
    )(tile_expert, n_active, hs, w2)


def _moe_combine_kernel(pos_ref, ys_hbm, x_ref, g_ref, w_ref, o_ref, buf, sem, *, tile):
    def copy(j):
        return pltpu.make_async_copy(ys_hbm.at[pl.ds(pos_ref[0, j], 1)],
                                     buf.at[j // tile, pl.ds(j % tile, 1)], sem)

    def start(j, c):
        copy(j).start()
        return c

    def wait(j, c):
        copy(j).wait()
        return c

    lax.fori_loop(0, TOP_K * tile, start, 0)
    lax.fori_loop(0, TOP_K * tile, wait, 0)
    w = w_ref[...]
    y = w[:, 0:1] * buf[0] + w[:, 1:2] * buf[1]
    o_ref[...] = x_ref[...] + g_ref[...] * y


def _moe_combine(rows, ys, pos, x, mods, which, ew):
    m, d = x.shape
    tile = rows.row_tile((256, 128, 64, 32, 16, 8))
    return pl.pallas_call(
        functools.partial(_moe_combine_kernel, tile=tile),
        grid=(m // tile,),
        in_specs=[pl.BlockSpec((None, 1, TOP_K * tile), lambda i: (i, 0, 0), memory_space=pltpu.SMEM),
                  pl.BlockSpec(memory_space=pl.ANY),
                  pl.BlockSpec((tile, d), lambda i: (i, 0)),
                  _mod_spec(rows, which, tile, d, d),
                  pl.BlockSpec((tile, LANES), lambda i: (i, 0))],
        out_specs=pl.BlockSpec((tile, d), lambda i: (i, 0)),
        out_shape=jax.ShapeDtypeStruct((m, d), F32),
        scratch_shapes=[pltpu.VMEM((TOP_K, tile, d), F32), pltpu.SemaphoreType.DMA(())],
        compiler_params=_cparams("arbitrary"), name="moe_combine",
    )(pos, ys, x, mods, ew)


def _moe_dispatch(e_idx, n_experts, tm, combine_tile):
    m = e_idx.shape[0]
    ex = jnp.concatenate([e_idx[:, 0], e_idx[:, 1]])
    onehot = (ex[:, None] == jnp.arange(n_experts, dtype=I32)[None, :]).astype(I32)
    rank = jnp.sum((jnp.cumsum(onehot, axis=0) - onehot) * onehot, axis=1)
    counts = jnp.sum(onehot, axis=0)
    padded = (counts + tm - 1) // tm * tm
    ends = jnp.cumsum(padded)
    dest = (ends - padded)[ex] + rank
    n_tiles = (TOP_K * m + n_experts * (tm - 1)) // tm
    tok = jnp.concatenate([jnp.arange(m, dtype=I32)] * TOP_K)
    src = jnp.zeros((n_tiles * tm,), I32).at[dest].set(tok).reshape(n_tiles, 1, tm)
    tile_expert = jnp.minimum(
        jnp.searchsorted(ends, jnp.arange(n_tiles, dtype=I32) * tm, side="right"), n_experts - 1).astype(I32)
    n_active = (ends[-1] // tm).astype(I32).reshape(1)
    pos = dest.reshape(TOP_K, m // combine_tile, combine_tile).transpose(1, 0, 2).reshape(
        m // combine_tile, 1, TOP_K * combine_tile).astype(I32)
    return src, tile_expert, n_active, pos


def _moe_ffn(rows, x, gn, mods, router_w, w1, w3, w2):
    n_experts = w1.shape[0]
    m = x.shape[0]
    v, e_idx, e_w = _norm_mod_call(rows, x, gn, mods, 3, 4, F32, router_w=router_w)
    tm = 512 if TOP_K * m >= 8192 else 64
    combine_tile = rows.row_tile((256, 128, 64, 32, 16, 8))
    src, tile_expert, n_active, pos = _moe_dispatch(e_idx, n_experts, tm, combine_tile)
    xs = _gather_rows(v, src, tm)
    hs = _expert_up(xs, w1, w3, tile_expert, n_active, tm)
    ys = _expert_down(hs, w2, tile_expert, n_active, tm)
    return _moe_combine(rows, ys, pos, x, mods, 5, e_w)


def kernel(x, c, ctx, c_ctx, ada_w, ada_b, norm_mix, norm_ffn, w_in, conv_w, gla_gate_w, gla_gate_b,
           mlstm_gate_b, gla_out_norm, mlstm_out_norm, w_up_a, w_up_b, w_o, ffn_w1, ffn_w3, ffn_w2,
           router_w, moe_w1, moe_w3, moe_w2, norm_final):
    batch, seq, d = x.shape
    ctx_len = ctx.shape[1]
    depth = ada_w.shape[0]
    assert seq % GRID_W == 0 and seq % SCAN_CHUNK == 0 and ctx_len % SCAN_CHUNK == 0
    assert seq % (batch * ctx_len) == 0 and (batch * ctx_len) % GRID_W == 0
    rows = _Rows(batch, seq, ctx_len)

    h = jnp.concatenate([x.reshape(batch * seq, d), ctx.reshape(batch * ctx_len, d)], axis=0)

    cvec = jnp.zeros((SUBLANES * ((batch + 1 + SUBLANES - 1) // SUBLANES), d), F32)
    cvec = cvec.at[:batch].set(c).at[batch].set(c_ctx)
    mods_all = _ada_mods(cvec, ada_w, ada_b)
    mods_all = mods_all[:, :batch + 1].reshape(depth, batch + 1, 6, 1, d).transpose(0, 2, 1, 3, 4)

    qk_scale = jnp.ones((QK_COLS,), F32)
    qk_scale = qk_scale.at[:QA].set(GLA_DK ** -0.5).at[2 * QA + QB:].set(MLSTM_DQK ** -0.5).reshape(1, QK_COLS)
    c_v, c_ro = QK_COLS, QK_COLS + V_A + V_B
    c_sm = c_ro + V_A + V_B
    c_g = c_sm + 2 * GLA_RANK + N_GATE_B
    n_small = c_g - c_sm

    for l in range(depth):
        mods = mods_all[l]
        wl = w_in[l]
        w_qk = wl[:, :c_v].astype(BF16)
        w_v = wl[:, c_v:c_ro].astype(BF16)
        w_ro = wl[:, c_ro:c_sm].astype(BF16)
        w_sm = jnp.zeros((d, LANES), BF16).at[:, :n_small].set(wl[:, c_sm:c_g].astype(BF16))
        w_g = wl[:, c_g:].astype(BF16)

        u = _norm_mod_call(rows, h, norm_mix[l], mods, 0, 1, BF16)
        p_qk = _matmul(u, w_qk, F32)
        v = _matmul(u, w_v, BF16)
        ro = _matmul(u, w_ro, F32)
        small = _matmul(u, w_sm, F32)
        sg = _matmul(u, w_g, F32, act="sigmoid")
        qk = _conv_silu(rows, p_qk, conv_w[l], qk_scale)
        gates_t = small[:, 2 * GLA_RANK:n_small].T
        oaf, oab = _gla_scan(rows, qk, v, small, gla_gate_w[l], gla_gate_b[l])
        obf, obb = _mlstm_scan(rows, qk, v, small, gates_t, mlstm_gate_b[l])
        ha, hb = _mix_prep(oaf, oab, obf, obb, ro, gla_out_norm[l], mlstm_out_norm[l])
        merged = _up_merge(ha, hb, w_up_a[l].astype(BF16), w_up_b[l].astype(BF16), sg)
        h = _matmul(merged, w_o[l].astype(BF16), F32, res=h, rows=rows, mods=mods, which=2)

        if l % 2 == 0:
            e = l // 2
            vv = _norm_mod_call(rows, h, norm_ffn[l], mods, 3, 4, BF16)
            hid = _swiglu_up(vv, ffn_w1[e].astype(BF16), ffn_w3[e].astype(BF16))
            h = _matmul(hid, ffn_w2[e].astype(BF16), F32, res=h, rows=rows, mods=mods, which=5)
        else:
            e = l // 2
            h = _moe_ffn(rows, h, norm_ffn[l], mods, router_w[e], moe_w1[e].astype(BF16),
                         moe_w3[e].astype(BF16), moe_w2[e].astype(BF16))

    out = _final_norm(rows, h, norm_final)
    return out.reshape(batch, seq, d)
```

```python
import functools
import math

import jax
import jax.numpy as jnp
from jax import lax
from jax.experimental import pallas as pl
from jax.experimental.pallas import tpu as pltpu

F32 = jnp.float32
BF16 = jnp.bfloat16
I32 = jnp.int32

GRID_W = 64
GLA_HEADS = 4
GLA_DK = 128
GLA_DV = 256
GLA_RANK = 16
GLA_GATE_NORMALIZER = 16.0
MLSTM_HEADS = 4
MLSTM_DQK = 128
MLSTM_DV = 256
GATE_SOFT_CAP = 15.0
N_GATE_B = 4 * MLSTM_HEADS
TOP_K = 2
EPS = 1e-6
QA = GLA_HEADS * GLA_DK
QB = MLSTM_HEADS * MLSTM_DQK
QK_COLS = 2 * QA + 2 * QB
V_A = GLA_HEADS * GLA_DV
V_B = MLSTM_HEADS * MLSTM_DV

LANES = 128
SUBLANES = 8
SCAN_CHUNK = 256
VMEM_LIMIT = 52 * 1024 * 1024
NEG_BIG = -1e30
HIGHEST = lax.Precision.HIGHEST


def _cparams(*sem):
    return pltpu.CompilerParams(dimension_semantics=sem, vmem_limit_bytes=VMEM_LIMIT)


def _pick(dim, prefs):
    for p in prefs:
        if dim % p == 0:
            return p
    return dim


def _dot(a, b, precision=None):
    return jnp.dot(a, b, preferred_element_type=F32, precision=precision)


def _dot_nt(a, b, precision=None):
    return lax.dot_general(a, b, (((1,), (1,)), ((), ())), preferred_element_type=F32, precision=precision)


def _dot_tn(a, b):
    return lax.dot_general(a, b, (((0,), (0,)), ((), ())), preferred_element_type=F32)


def _silu(x):
    return x * jax.nn.sigmoid(x)


def _split3(x):
    hi = x.astype(BF16)
    rest = x - hi.astype(F32)
    mid = rest.astype(BF16)
    lo = (rest - mid.astype(F32)).astype(BF16)
    return hi, mid, lo


def _ada_kernel(c_ref, w_ref, b_ref, o_ref):
    o_ref[...] = _dot(_silu(c_ref[...]), w_ref[...], HIGHEST) + b_ref[...]


def _ada_mods(cvec, ada_w, ada_b):
    depth, d, n = ada_w.shape
    tn = _pick(n, (1024, 512, 256, 128))
    return pl.pallas_call(
        _ada_kernel,
        grid=(depth, n // tn),
        in_specs=[pl.BlockSpec(cvec.shape, lambda l, j: (0, 0)),
                  pl.BlockSpec((None, d, tn), lambda l, j: (l, 0, j)),
                  pl.BlockSpec((None, 1, tn), lambda l, j: (l, 0, j))],
        out_specs=pl.BlockSpec((None, cvec.shape[0], tn), lambda l, j: (l, 0, j)),
        out_shape=jax.ShapeDtypeStruct((depth, cvec.shape[0], n), F32),
        compiler_params=_cparams("parallel", "parallel"),
        name="ada_mods",
    )(cvec, ada_w, ada_b.reshape(depth, 1, n))


class _Rows:
    def __init__(self, batch, seq, ctx_len):
        self.batch, self.seq, self.ctx_len = batch, seq, ctx_len
        self.n_lat = batch * seq
        self.n_ctx = batch * ctx_len
        self.m = self.n_lat + self.n_ctx

    def row_tile(self, prefs):
        return _pick(math.gcd(self.seq, self.n_ctx), prefs)

    def gid(self, i, tm):
        r0 = i * tm
        return jnp.where(r0 < self.n_lat, r0 // self.seq, self.batch)


def _mod_spec(rows, which, tm, tn, d):
    if tn == d:
        return pl.BlockSpec((None, None, 1, d), lambda i, *_: (which, rows.gid(i, tm), 0, 0))
    return pl.BlockSpec((None, None, 1, tn), lambda i, j, *_: (which, rows.gid(i, tm), 0, j))


def _norm_mod(x, gn, sh, sc):
    y = x * lax.rsqrt(jnp.mean(x * x, axis=-1, keepdims=True) + EPS) * gn
    return y * (1.0 + sc) + sh


def _norm_mod_kernel(x_ref, gn_ref, sh_ref, sc_ref, o_ref):
    o_ref[...] = _norm_mod(x_ref[...], gn_ref[...], sh_ref[...], sc_ref[...]).astype(o_ref.dtype)


def _norm_mod_router_kernel(x_ref, gn_ref, sh_ref, sc_ref, rw_ref, o_ref, ei_ref, ew_ref, *, n_experts):
    u = _norm_mod(x_ref[...], gn_ref[...], sh_ref[...], sc_ref[...])
    o_ref[...] = u
    logits = _dot(u, rw_ref[...], HIGHEST)
    lane_i = lax.broadcasted_iota(I32, logits.shape, 1)
    lane = lane_i.astype(F32)
    lg = jnp.where(lane_i < n_experts, logits, -jnp.inf)
    m1 = jnp.max(lg, axis=-1, keepdims=True)
    i1 = jnp.min(jnp.where(lg == m1, lane, float(LANES)), axis=-1, keepdims=True)
    lg2 = jnp.where(lane == i1, -jnp.inf, lg)
    m2 = jnp.max(lg2, axis=-1, keepdims=True)
    i2 = jnp.min(jnp.where(lg2 == m2, lane, float(LANES)), axis=-1, keepdims=True)
    e = jnp.exp(m2 - m1)
    w1 = 1.0 / (1.0 + e)
    w2 = e / (1.0 + e)
    ei_ref[...] = jnp.where(lane_i == 0, i1, jnp.where(lane_i == 1, i2, 0.0)).astype(I32)
    ew_ref[...] = jnp.where(lane_i == 0, w1, jnp.where(lane_i == 1, w2, 0.0))


def _norm_mod_call(rows, x, gn, mods, which_sh, which_sc, out_dtype, router_w=None):
    m, d = x.shape
    tm = rows.row_tile((256, 128, 64, 32, 16, 8))
    in_specs = [pl.BlockSpec((tm, d), lambda i: (i, 0)),
                pl.BlockSpec((1, d), lambda i: (0, 0)),
                _mod_spec(rows, which_sh, tm, d, d),
                _mod_spec(rows, which_sc, tm, d, d)]
    row_spec = pl.BlockSpec((tm, d), lambda i: (i, 0))
    if router_w is None:
        return pl.pallas_call(
            _norm_mod_kernel, grid=(m // tm,), in_specs=in_specs, out_specs=row_spec,
            out_shape=jax.ShapeDtypeStruct((m, d), out_dtype),
            compiler_params=_cparams("parallel"), name="norm_mod",
        )(x, gn.reshape(1, d), mods, mods)
    n_experts = router_w.shape[1]
    rw = jnp.zeros((d, LANES), F32).at[:, :n_experts].set(router_w)
    lane_spec = pl.BlockSpec((tm, LANES), lambda i: (i, 0))
    return pl.pallas_call(
        functools.partial(_norm_mod_router_kernel, n_experts=n_experts),
        grid=(m // tm,),
        in_specs=in_specs + [pl.BlockSpec((d, LANES), lambda i: (0, 0))],
        out_specs=[row_spec, lane_spec, lane_spec],
        out_shape=[jax.ShapeDtypeStruct((m, d), F32), jax.ShapeDtypeStruct((m, LANES), I32),
                   jax.ShapeDtypeStruct((m, LANES), F32)],
        compiler_params=_cparams("parallel"), name="norm_mod_router",
    )(x, gn.reshape(1, d), mods, mods, rw)


def _final_norm_kernel(x_ref, gn_ref, o_ref):
    x = x_ref[...]
    o_ref[...] = x * lax.rsqrt(jnp.mean(x * x, axis=-1, keepdims=True) + EPS) * gn_ref[...]


def _final_norm(rows, x, gn):
    d = x.shape[1]
    tm = rows.row_tile((256, 128, 64, 32, 16, 8))
    return pl.pallas_call(
        _final_norm_kernel, grid=(rows.n_lat // tm,),
        in_specs=[pl.BlockSpec((tm, d), lambda i: (i, 0)), pl.BlockSpec((1, d), lambda i: (0, 0))],
        out_specs=pl.BlockSpec((tm, d), lambda i: (i, 0)),
        out_shape=jax.ShapeDtypeStruct((rows.n_lat, d), F32),
        compiler_params=_cparams("parallel"), name="final_norm",
    )(x, gn.reshape(1, d))


def _mm_kernel(*refs, nk, act, residual):
    if residual:
        a_ref, w_ref, x_ref, g_ref, o_ref = refs[:5]
        rest = refs[5:]
    else:
        a_ref, w_ref, o_ref = refs[:3]
        rest = refs[3:]

    def finish(r):
        if act == "sigmoid":
            r = jax.nn.sigmoid(r)
        if residual:
            r = x_ref[...] + g_ref[...] * r
        o_ref[...] = r.astype(o_ref.dtype)

    part = _dot(a_ref[...], w_ref[...])
    if nk == 1:
        finish(part)
        return
    acc_ref, = rest
    k = pl.program_id(2)

    @pl.when(k == 0)
    def _():
        acc_ref[...] = part

    @pl.when(k > 0)
    def _():
        acc_ref[...] += part

    @pl.when(k == nk - 1)
    def _():
        finish(acc_ref[...])


def _matmul(a, w, out_dtype, *, act=None, res=None, rows=None, mods=None, which=None):
    m, kdim = a.shape
    n = w.shape[1]
    tm = _pick(m, (1024, 512, 256, 128, 64, 32, 16, 8)) if rows is None else rows.row_tile((1024, 512, 256, 128, 64, 32, 16, 8))
    tn = _pick(n, (1024, 512, 256, 128))
    tk = kdim if kdim <= 2048 else _pick(kdim, (2816, 2048, 1024, 512, 256, 128))
    nk = kdim // tk
    residual = res is not None
    in_specs = [pl.BlockSpec((tm, tk), lambda i, j, k: (i, k)),
                pl.BlockSpec((tk, tn), lambda i, j, k: (k, j))]
    args = [a, w]
    if residual:
        in_specs += [pl.BlockSpec((tm, tn), lambda i, j, k: (i, j)), _mod_spec(rows, which, tm, tn, None)]
        args += [res, mods]
    return pl.pallas_call(
        functools.partial(_mm_kernel, nk=nk, act=act, residual=residual),
        grid=(m // tm, n // tn, nk),
        in_specs=in_specs,
        out_specs=pl.BlockSpec((tm, tn), lambda i, j, k: (i, j)),
        out_shape=jax.ShapeDtypeStruct((m, n), out_dtype),
        scratch_shapes=[pltpu.VMEM((tm, tn), F32)] if nk > 1 else [],
        compiler_params=_cparams("parallel", "parallel", "arbitrary"),
        name="matmul",
    )(*args)


def _swiglu_kernel(a_ref, w1_ref, w3_ref, o_ref):
    a = a_ref[...]
    o_ref[...] = (_silu(_dot(a, w1_ref[...])) * _dot(a, w3_ref[...])).astype(o_ref.dtype)


def _swiglu_up(a, w1, w3):
    m, d = a.shape
    f = w1.shape[1]
    tm = _pick(m, (1024, 512, 256, 128, 64, 32, 16, 8))
    tn = _pick(f, (512, 256, 128))
    return pl.pallas_call(
        _swiglu_kernel, grid=(m // tm, f // tn),
        in_specs=[pl.BlockSpec((tm, d), lambda i, j: (i, 0)),
                  pl.BlockSpec((d, tn), lambda i, j: (0, j)),
                  pl.BlockSpec((d, tn), lambda i, j: (0, j))],
        out_specs=pl.BlockSpec((tm, tn), lambda i, j: (i, j)),
        out_shape=jax.ShapeDtypeStruct((m, f), BF16),
        compiler_params=_cparams("parallel", "parallel"), name="swiglu_up",
    )(a, w1, w3)


def _up_merge_kernel(ha_ref, hb_ref, wa_ref, wb_ref, sga_ref, sgb_ref, o_ref):
    up_a = _dot(ha_ref[...], wa_ref[...])
    up_b = _dot(hb_ref[...], wb_ref[...])
    o_ref[...] = (sga_ref[...] * up_a + sgb_ref[...] * up_b).astype(o_ref.dtype)


def _up_merge(ha, hb, wa, wb, sg):
    m, va = ha.shape
    d = wa.shape[1]
    tm = _pick(m, (1024, 512, 256, 128, 64, 32, 16, 8))
    tn = _pick(d, (1024, 512, 256, 128))
    nj = d // tn
    return pl.pallas_call(
        _up_merge_kernel, grid=(m // tm, nj),
        in_specs=[pl.BlockSpec((tm, va), lambda i, j: (i, 0)),
                  pl.BlockSpec((tm, hb.shape[1]), lambda i, j: (i, 0)),
                  pl.BlockSpec((va, tn), lambda i, j: (0, j)),
                  pl.BlockSpec((hb.shape[1], tn), lambda i, j: (0, j)),
                  pl.BlockSpec((tm, tn), lambda i, j: (i, j)),
                  pl.BlockSpec((tm, tn), lambda i, j: (i, j + nj))],
        out_specs=pl.BlockSpec((tm, tn), lambda i, j: (i, j)),
        out_shape=jax.ShapeDtypeStruct((m, d), BF16),
        compiler_params=_cparams("parallel", "parallel"), name="up_merge",
    )(ha, hb, wa, wb, sg, sg)


def _conv_taps(xs, w, dy, not_first, not_last):
    n = xs.shape[0]
    left = jnp.where(not_first, pltpu.roll(xs, 1, 0), 0.0)
    right = jnp.where(not_last, pltpu.roll(xs, n - 1, 0), 0.0)
    return w[3 * dy:3 * dy + 1] * left + w[3 * dy + 1:3 * dy + 2] * xs + w[3 * dy + 2:3 * dy + 3] * right


def _conv_kernel(up_ref, x_ref, dn_ref, w_ref, s_ref, o_ref, pad_ref, *, rows, rb, strip):
    r0 = pl.program_id(0) * rb
    w = w_ref[...]

    @pl.when(r0 < rows.n_lat)
    def _():
        tc = x_ref.shape[-1]
        at_start = lax.rem(r0, rows.seq) == 0
        at_end = lax.rem(r0 + rb, rows.seq) == 0
        pad_ref[pl.ds(0, GRID_W), :] = jnp.where(at_start, 0.0, up_ref[...])
        pad_ref[pl.ds(GRID_W + rb, GRID_W), :] = jnp.where(at_end, 0.0, dn_ref[...])
        pad_ref[pl.ds(GRID_W, rb), :] = x_ref[...]
        col = lax.broadcasted_iota(I32, (strip, tc), 0) % GRID_W
        not_first = col != 0
        not_last = col != GRID_W - 1
        for s in range(rb // strip):
            acc = jnp.zeros((strip, tc), F32)
            for dy in range(3):
                xs = pad_ref[pl.ds(s * strip + dy * GRID_W, strip), :]
                acc = acc + _conv_taps(xs, w, dy, not_first, not_last)
            o_ref[pl.ds(s * strip, strip), :] = _silu(acc) * s_ref[...]

    @pl.when(r0 >= rows.n_lat)
    def _():
        x = x_ref[...]
        pos = lax.broadcasted_iota(I32, x.shape, 0) % rows.ctx_len
        acc = _conv_taps(x, w, 1, pos != 0, pos != rows.ctx_len - 1)
        o_ref[...] = _silu(acc) * s_ref[...]


def _conv_silu(rows, p_qk, conv_w, scale):
    m, c = p_qk.shape
    tc = _pick(c, (256, 128))
    rb = rows.n_ctx
    strip = _pick(rb, (512, 256, 128, 64))
    per = rb // GRID_W
    last = m // GRID_W - 1
    return pl.pallas_call(
        functools.partial(_conv_kernel, rows=rows, rb=rb, strip=strip),
        grid=(m // rb, c // tc),
        in_specs=[pl.BlockSpec((GRID_W, tc), lambda i, j: (jnp.maximum(i * per - 1, 0), j)),
                  pl.BlockSpec((rb, tc), lambda i, j: (i, j)),
                  pl.BlockSpec((GRID_W, tc), lambda i, j: (jnp.minimum((i + 1) * per, last), j)),
                  pl.BlockSpec((9, tc), lambda i, j: (0, j)),
                  pl.BlockSpec((1, tc), lambda i, j: (0, j))],
        out_specs=pl.BlockSpec((rb, tc), lambda i, j: (i, j)),
        out_shape=jax.ShapeDtypeStruct((m, c), F32),
        scratch_shapes=[pltpu.VMEM((rb + 2 * GRID_W, tc), F32)],
        compiler_params=_cparams("parallel", "parallel"), name="conv_silu",
    )(p_qk, p_qk, p_qk, conv_w.reshape(9, c), scale)


def _causal_masks(n):
    t = lax.broadcasted_iota(I32, (n, n), 0)
    s = lax.broadcasted_iota(I32, (n, n), 1)
    return t, s, (s <= t, s >= t)


def _block_ref_rows(x, blk, row):
    n, c = x.shape
    if blk >= SUBLANES:
        x3 = x.reshape(n // blk, blk, c)
        return jnp.broadcast_to(x3[:, row:row + 1, :], x3.shape).reshape(n, c)
    x3 = x.reshape(n // SUBLANES, SUBLANES, c)
    sub = lax.broadcasted_iota(I32, x3.shape, 1)
    out = jnp.zeros_like(x3)
    for g in range(SUBLANES // blk):
        r = g * blk + row
        out = jnp.where(sub // blk == g, jnp.broadcast_to(x3[:, r:r + 1, :], x3.shape), out)
    return out.reshape(n, c)


def _gla_direction(d, q, k, v, lr, wg, bg, s_ref, t_idx, s_idx, causal):
    n = q.shape[0]
    z = _dot(lr, wg, HIGHEST) + bg
    la = jax.nn.log_sigmoid(z) * (1.0 / GLA_GATE_NORMALIZER)
    tri = causal.astype(BF16)
    cum = sum(_dot(tri, part) for part in _split3(la))
    total = cum[n - 1:n] if d == 0 else cum[0:1]
    state = s_ref[...]
    out = _dot((q * jnp.exp(cum)).astype(BF16), state.astype(BF16))
    lev = jnp.where(causal, 31 - lax.clz(t_idx ^ s_idx), -2)
    scores = jnp.where(lev == -1, _dot_nt(q.astype(BF16), k.astype(BF16)), 0.0)
    for l in range(n.bit_length() - 1):
        half = 1 << l
        ref = _block_ref_rows(cum, 2 * half, half - 1 if d == 0 else half)
        e = jnp.exp(-jnp.abs(cum - ref))
        p = _dot_nt((q * e).astype(BF16), (k * e).astype(BF16))
        scores = jnp.where(lev == l, p, scores)
    out = out + _dot(scores.astype(BF16), v)
    k_out = (k * jnp.exp(total - cum)).astype(BF16)
    dk = state.shape[0]
    et = jnp.broadcast_to(jnp.exp(total), (dk, dk)).T
    scale = jnp.concatenate([et] * (state.shape[1] // dk), axis=1)
    s_ref[...] = scale * state + _dot_tn(k_out, v)
    return out


def _gla_kernel(qf, kf, vf, smf, qb, kb, vb, smb, wg_ref, bg_ref, of_ref, ob_ref, sf_ref, sb_ref):
    @pl.when(pl.program_id(2) == 0)
    def _():
        sf_ref[...] = jnp.zeros_like(sf_ref)
        sb_ref[...] = jnp.zeros_like(sb_ref)

    n = qf.shape[0]
    t_idx, s_idx, causal = _causal_masks(n)
    dirs = ((qf, kf, vf, smf, of_ref, sf_ref), (qb, kb, vb, smb, ob_ref, sb_ref))
    for d, (q, k, v, sm, o_ref, s_ref) in enumerate(dirs):
        lr = sm[:, d * GLA_RANK:(d + 1) * GLA_RANK]
        o_ref[...] = _gla_direction(d, q[...], k[...], v[...], lr, wg_ref[d], bg_ref[d], s_ref,
                                    t_idx, s_idx, causal[d])


def _soft_cap(z):
    return GATE_SOFT_CAP * jnp.tanh(z * (1.0 / GATE_SOFT_CAP))


def _mlstm_direction(d, head, q, k, v, g_col, g_row, c_ref, m_ref, causal):
    n = q.shape[0]
    tri = causal.astype(BF16)
    cum_c_all = sum(_dot(tri, part) for part in _split3(jax.nn.log_sigmoid(g_col)))
    cum_r_all = sum(_dot_nt(part, tri) for part in _split3(jax.nn.log_sigmoid(g_row)))
    lane = lax.broadcasted_iota(I32, (1, N_GATE_B), 1)
    sub = lax.broadcasted_iota(I32, (N_GATE_B, 1), 0)
    i_idx = 2 * MLSTM_HEADS * d + head
    f_idx = i_idx + MLSTM_HEADS
    pick_c = lambda a, idx: jnp.sum(jnp.where(lane == idx, a, 0.0), axis=1, keepdims=True)
    pick_r = lambda a, idx: jnp.sum(jnp.where(sub == idx, a, 0.0), axis=0, keepdims=True)
    i_c, cum_c = pick_c(g_col, i_idx), pick_c(cum_c_all, f_idx)
    i_r, cum_r = pick_r(g_row, i_idx), pick_r(cum_r_all, f_idx)
    total = cum_c[n - 1:n] if d == 0 else cum_c[0:1]
    m_prev = m_ref[...]
    dlog = jnp.where(causal, cum_c - cum_r + i_r, NEG_BIG)
    inter_log = cum_c + m_prev
    m_t = jnp.maximum(inter_log, jnp.max(dlog, axis=1, keepdims=True))
    w_inter = jnp.exp(inter_log - m_t)
    qb16 = q.astype(BF16)
    s = _dot_nt(qb16, k.astype(BF16)) * jnp.exp(dlog - m_t)
    one_col = (lax.broadcasted_iota(I32, (n, LANES), 1) == 0).astype(BF16)
    v_ext = jnp.concatenate([v, one_col], axis=1)
    state = c_ref[...]
    acc = w_inter * _dot(qb16, state.astype(BF16)) + _dot(s.astype(BF16), v_ext)
    dv = v.shape[1]
    den = acc[:, dv:dv + 1]
    out = acc[:, :dv] / jnp.maximum(jnp.abs(den), jnp.exp(-m_t))
    g = total - cum_c + i_c
    m_new = jnp.maximum(total + m_prev, jnp.max(g, axis=0, keepdims=True))
    w_c = jnp.exp(total + m_prev - m_new)
    w_k = jnp.exp(g - m_new)
    c_ref[...] = w_c * state + _dot_tn((k * w_k).astype(BF16), v_ext)
    m_ref[...] = m_new
    return out


def _mlstm_kernel(qf, kf, vf, gcf, grf, qb, kb, vb, gcb, grb, brow_ref, bcol_ref,
                  of_ref, ob_ref, cf_ref, cb_ref, mf_ref, mb_ref):
    @pl.when(pl.program_id(2) == 0)
    def _():
        cf_ref[...] = jnp.zeros_like(cf_ref)
        cb_ref[...] = jnp.zeros_like(cb_ref)
        mf_ref[...] = jnp.zeros_like(mf_ref)
        mb_ref[...] = jnp.zeros_like(mb_ref)

    head = pl.program_id(1)
    n = qf.shape[0]
    _, _, causal = _causal_masks(n)
    g0 = 2 * GLA_RANK
    dirs = ((qf, kf, vf, gcf, grf, of_ref, cf_ref, mf_ref), (qb, kb, vb, gcb, grb, ob_ref, cb_ref, mb_ref))
    for d, (q, k, v, gc, gr, o_ref, c_ref, m_ref) in enumerate(dirs):
        g_col = _soft_cap(gc[:, g0:g0 + N_GATE_B] + brow_ref[...])
        g_row = _soft_cap(gr[...] + bcol_ref[...])
        o_ref[...] = _mlstm_direction(d, head, q[...], k[...], v[...], g_col, g_row, c_ref, m_ref, causal[d])


def _scan_row_maps(rows, chunk):
    lat_chunks = rows.seq // chunk
    ctx_chunks = rows.ctx_len // chunk
    ctx0 = rows.n_lat // chunk

    def fwd(b, s):
        return jnp.where(s < ctx_chunks, ctx0 + b * ctx_chunks + s, b * lat_chunks + (s - ctx_chunks))

    def bwd(b, s):
        return jnp.where(s < ctx_chunks, ctx0 + b * ctx_chunks + (ctx_chunks - 1 - s),
                         b * lat_chunks + (lat_chunks - 1 - (s - ctx_chunks)))

    return fwd, bwd, ctx_chunks + lat_chunks


def _gla_scan(rows, qk, v, small, gate_w, gate_b):
    m = qk.shape[0]
    n = SCAN_CHUNK
    fwd, bwd, steps = _scan_row_maps(rows, n)
    hq = QA // GLA_DK

    def specs(rmap):
        return [pl.BlockSpec((n, GLA_DK), lambda b, h, s: (rmap(b, s), h)),
                pl.BlockSpec((n, GLA_DK), lambda b, h, s: (rmap(b, s), hq + h)),
                pl.BlockSpec((n, GLA_DV), lambda b, h, s: (rmap(b, s), h)),
                pl.BlockSpec((n, LANES), lambda b, h, s: (rmap(b, s), 0))]

    out_spec = lambda rmap: pl.BlockSpec((n, GLA_DV), lambda b, h, s: (rmap(b, s), h))
    out_sds = jax.ShapeDtypeStruct((m, V_A), F32)
    return pl.pallas_call(
        _gla_kernel,
        grid=(rows.batch, GLA_HEADS, steps),
        in_specs=specs(fwd) + specs(bwd) + [
            pl.BlockSpec((2, GLA_RANK, GLA_DK), lambda b, h, s: (0, 0, h)),
            pl.BlockSpec((2, 1, GLA_DK), lambda b, h, s: (0, 0, h))],
        out_specs=[out_spec(fwd), out_spec(bwd)],
        out_shape=[out_sds, out_sds],
        scratch_shapes=[pltpu.VMEM((GLA_DK, GLA_DV), F32), pltpu.VMEM((GLA_DK, GLA_DV), F32)],
        compiler_params=_cparams("parallel", "parallel", "arbitrary"), name="gla_scan",
    )(qk, qk, v, small, qk, qk, v, small, gate_w, gate_b.reshape(2, 1, QA))


def _mlstm_scan(rows, qk, v, small, gates_t, gate_b):
    m = qk.shape[0]
    n = SCAN_CHUNK
    fwd, bwd, steps = _scan_row_maps(rows, n)
    q0 = 2 * QA // MLSTM_DQK
    k0 = q0 + QB // MLSTM_DQK
    v0 = V_A // MLSTM_DV

    def specs(rmap):
        return [pl.BlockSpec((n, MLSTM_DQK), lambda b, h, s: (rmap(b, s), q0 + h)),
                pl.BlockSpec((n, MLSTM_DQK), lambda b, h, s: (rmap(b, s), k0 + h)),
                pl.BlockSpec((n, MLSTM_DV), lambda b, h, s: (rmap(b, s), v0 + h)),
                pl.BlockSpec((n, LANES), lambda b, h, s: (rmap(b, s), 0)),
                pl.BlockSpec((N_GATE_B, n), lambda b, h, s: (0, rmap(b, s)))]

    out_spec = lambda rmap: pl.BlockSpec((n, MLSTM_DV), lambda b, h, s: (rmap(b, s), h))
    out_sds = jax.ShapeDtypeStruct((m, V_B), F32)
    ext = MLSTM_DV + LANES
    return pl.pallas_call(
        _mlstm_kernel,
        grid=(rows.batch, MLSTM_HEADS, steps),
        in_specs=specs(fwd) + specs(bwd) + [
            pl.BlockSpec((1, N_GATE_B), lambda b, h, s: (0, 0)),
            pl.BlockSpec((N_GATE_B, 1), lambda b, h, s: (0, 0))],
        out_specs=[out_spec(fwd), out_spec(bwd)],
        out_shape=[out_sds, out_sds],
        scratch_shapes=[pltpu.VMEM((MLSTM_DQK, ext), F32), pltpu.VMEM((MLSTM_DQK, ext), F32),
                        pltpu.VMEM((1, 1), F32), pltpu.VMEM((1, 1), F32)],
        compiler_params=_cparams("parallel", "parallel", "arbitrary"), name="mlstm_scan",
    )(qk, qk, v, small, gates_t, qk, qk, v, small, gates_t,
      gate_b.reshape(1, N_GATE_B), gate_b.reshape(N_GATE_B, 1))


def _mix_prep_kernel(oaf, oab, obf, obb, ro_ref, gna_ref, gnb_ref, ha_ref, hb_ref):
    branches = ((oaf, oab, gna_ref, ha_ref, 0, GLA_HEADS, GLA_DV, _silu),
                (obf, obb, gnb_ref, hb_ref, V_A, MLSTM_HEADS, MLSTM_DV, jax.nn.sigmoid))
    for of, ob, gn_ref, h_ref, off, heads, dv, gate_fn in branches:
        for h in range(heads):
            sl = slice(h * dv, (h + 1) * dv)
            o = of[:, sl] + ob[:, sl]
            y = o * lax.rsqrt(jnp.mean(o * o, axis=-1, keepdims=True) + EPS) * gn_ref[:, sl]
            gate = gate_fn(ro_ref[:, off + h * dv:off + (h + 1) * dv])
            h_ref[:, sl] = (y * gate).astype(h_ref.dtype)


def _mix_prep(oaf, oab, obf, obb, ro, gna, gnb):
    m = ro.shape[0]
    tm = _pick(m, (256, 128, 64, 32, 16, 8))
    row = lambda c: pl.BlockSpec((tm, c), lambda i: (i, 0))
    vec = lambda c: pl.BlockSpec((1, c), lambda i: (0, 0))
    return pl.pallas_call(
        _mix_prep_kernel, grid=(m // tm,),
        in_specs=[row(V_A), row(V_A), row(V_B), row(V_B), row(V_A + V_B), vec(V_A), vec(V_B)],
        out_specs=[row(V_A), row(V_B)],
        out_shape=[jax.ShapeDtypeStruct((m, V_A), BF16), jax.ShapeDtypeStruct((m, V_B), BF16)],
        compiler_params=_cparams("parallel"), name="mix_prep",
    )(oaf, oab, obf, obb, ro, gna.reshape(1, V_A), gnb.reshape(1, V_B))


def _expert_up_kernel(te_ref, na_ref, src0_ref, srcn_ref, v_hbm, w1_ref, w3_ref, o_ref, xbuf, abuf, sem,
                      *, tm, issue_steps):
    del te_ref
    i = pl.program_id(0)
    j = pl.program_id(1)
    slot = lax.rem(i, 2)
    per = tm // issue_steps

    def row_copy(src_ref, r, s):
        return pltpu.make_async_copy(v_hbm.at[pl.ds(src_ref[0, r], 1)], xbuf.at[s, pl.ds(r, 1)], sem.at[s])

    @pl.when(jnp.logical_and(i == 0, j == 0))
    def _():
        def start(r, carry):
            row_copy(src0_ref, r, 0).start()
            return carry

        lax.fori_loop(0, tm, start, 0)

    @pl.when(j == 0)
    def _():
        def wait(r, carry):
            row_copy(src0_ref, 0, slot).wait()
            return carry

        lax.fori_loop(0, tm, wait, 0)
        abuf[...] = xbuf[slot].astype(BF16)

    @pl.when(jnp.logical_and(i + 1 < pl.num_programs(0), j < issue_steps))
    def _():
        for t in range(per):
            row_copy(srcn_ref, j * per + t, 1 - slot).start()

    @pl.when(i < na_ref[0])
    def _():
        a = abuf[...]
        o_ref[...] = (_silu(_dot(a, w1_ref[...])) * _dot(a, w3_ref[...])).astype(o_ref.dtype)

    @pl.when(i >= na_ref[0])
    def _():
        o_ref[...] = jnp.zeros_like(o_ref)


def _expert_up(v, src, w1, w3, tile_expert, n_active, tm):
    n_tiles = src.shape[0]
    d = v.shape[1]
    f = w1.shape[2]
    tn = _pick(f, (512, 256, 128))
    nj = f // tn
    issue_steps = max(s for s in (1, 2, 4, 8) if s <= nj)
    wmap = lambda i, j, te, na: (te[i], 0, jnp.where(i < na[0], j, 0))
    return pl.pallas_call(
        functools.partial(_expert_up_kernel, tm=tm, issue_steps=issue_steps),
        grid_spec=pltpu.PrefetchScalarGridSpec(
            num_scalar_prefetch=2, grid=(n_tiles, nj),
            in_specs=[pl.BlockSpec((None, 1, tm), lambda i, j, te, na: (0, 0, 0), memory_space=pltpu.SMEM),
                      pl.BlockSpec((None, 1, tm), lambda i, j, te, na: (jnp.minimum(i + 1, n_tiles - 1), 0, 0),
                                   memory_space=pltpu.SMEM),
                      pl.BlockSpec(memory_space=pl.ANY),
                      pl.BlockSpec((None, d, tn), wmap),
                      pl.BlockSpec((None, d, tn), wmap)],
            out_specs=pl.BlockSpec((tm, tn), lambda i, j, te, na: (i, j)),
            scratch_shapes=[pltpu.VMEM((2, tm, d), F32), pltpu.VMEM((tm, d), BF16),
                            pltpu.SemaphoreType.DMA((2,))]),
        out_shape=jax.ShapeDtypeStruct((n_tiles * tm, f), BF16),
        compiler_params=_cparams("arbitrary", "arbitrary"), name="expert_up",
    )(tile_expert, n_active, src, src, v, w1, w3)


def _expert_down_kernel(te_ref, na_ref, a_ref, w_ref, o_ref, acc_ref, *, nk):
    del te_ref
    i = pl.program_id(0)
    k = pl.program_id(2)

    @pl.when(i < na_ref[0])
    def _():
        part = _dot(a_ref[...], w_ref[...])

        @pl.when(k == 0)
        def _():
            acc_ref[...] = part

        @pl.when(k > 0)
        def _():
            acc_ref[...] += part

        @pl.when(k == nk - 1)
        def _():
            o_ref[...] = acc_ref[...]

    @pl.when(i >= na_ref[0])
    def _():
        o_ref[...] = jnp.zeros_like(o_ref)


def _expert_down(hs, w2, tile_expert, n_active, tm):
    p, f = hs.shape
    d = w2.shape[2]
    tn = _pick(d, (1024, 512, 256, 128))
    tk = f if f <= 2048 else _pick(f, (2816, 2048, 1024, 512, 256, 128))
    nk = f // tk
    act = lambda i, na: jnp.where(i < na[0], i, 0)
    return pl.pallas_call(
        functools.partial(_expert_down_kernel, nk=nk),
        grid_spec=pltpu.PrefetchScalarGridSpec(
            num_scalar_prefetch=2, grid=(p // tm, d // tn, nk),
            in_specs=[pl.BlockSpec((tm, tk), lambda i, j, k, te, na: (act(i, na), jnp.where(i < na[0], k, 0))),
                      pl.BlockSpec((None, tk, tn),
                                   lambda i, j, k, te, na: (te[i], jnp.where(i < na[0], k, 0), jnp.where(i < na[0], j, 0)))],
            out_specs=pl.BlockSpec((tm, tn), lambda i, j, k, te, na: (i, j)),
            scratch_shapes=[pltpu.VMEM((tm, tn), F32)]),
        out_shape=jax.ShapeDtypeStruct((p, d), F32),
        compiler_params=_cparams("parallel", "parallel", "arbitrary"), name="expert_down",
    )(tile_expert, n_active, hs, w2)


def _moe_combine_kernel(pos_ref, posn_ref, ys_hbm, x_ref, g_ref, w_ref, o_ref, buf, sem, *, tile):
    i = pl.program_id(0)
    slot = lax.rem(i, 2)

    def row_copy(p_ref, k, r, s):
        return pltpu.make_async_copy(ys_hbm.at[pl.ds(p_ref[0, k * tile + r], 1)],
                                     buf.at[s, k, pl.ds(r, 1)], sem.at[s])

    def start_tile(p_ref, s):
        for k in range(TOP_K):
            def start(r, carry):
                row_copy(p_ref, k, r, s).start()
                return carry

            lax.fori_loop(0, tile, start, 0, unroll=8)

    @pl.when(i == 0)
    def _():
        start_tile(pos_ref, 0)

    @pl.when(i + 1 < pl.num_programs(0))
    def _():
        start_tile(posn_ref, 1 - slot)

    def wait(r, carry):
        row_copy(pos_ref, 0, 0, slot).wait()
        return carry

    lax.fori_loop(0, TOP_K * tile, wait, 0, unroll=8)
    w = w_ref[...]
    y = w[:, 0:1] * buf[slot, 0] + w[:, 1:2] * buf[slot, 1]
    o_ref[...] = x_ref[...] + g_ref[...] * y


def _moe_combine(rows, ys, pos, x, mods, which, ew, m):
    d = x.shape[1]
    tile = rows.row_tile((256, 128, 64, 32, 16, 8))
    n = m // tile
    pos_spec = lambda imap: pl.BlockSpec((None, 1, TOP_K * tile), imap, memory_space=pltpu.SMEM)
    return pl.pallas_call(
        functools.partial(_moe_combine_kernel, tile=tile),
        grid=(n,),
        in_specs=[pos_spec(lambda i: (i, 0, 0)),
                  pos_spec(lambda i: (jnp.minimum(i + 1, n - 1), 0, 0)),
                  pl.BlockSpec(memory_space=pl.ANY),
                  pl.BlockSpec((tile, d), lambda i: (i, 0)),
                  _mod_spec(rows, which, tile, d, d),
                  pl.BlockSpec((tile, LANES), lambda i: (i, 0))],
        out_specs=pl.BlockSpec((tile, d), lambda i: (i, 0)),
        out_shape=jax.ShapeDtypeStruct((m, d), F32),
        scratch_shapes=[pltpu.VMEM((2, TOP_K, tile, d), F32), pltpu.SemaphoreType.DMA((2,))],
        compiler_params=_cparams("arbitrary"), name="moe_combine",
    )(pos, pos, ys, x, mods, ew)


def _moe_dispatch(e_idx, n_experts, tm, combine_tile):
    m = e_idx.shape[0]
    ex = jnp.concatenate([e_idx[:, 0], e_idx[:, 1]])
    onehot = (ex[:, None] == jnp.arange(n_experts, dtype=I32)[None, :]).astype(I32)
    rank = jnp.sum((jnp.cumsum(onehot, axis=0) - onehot) * onehot, axis=1)
    counts = jnp.sum(onehot, axis=0)
    padded = (counts + tm - 1) // tm * tm
    ends = jnp.cumsum(padded)
    dest = (ends - padded)[ex] + rank
    n_tiles = (TOP_K * m + n_experts * (tm - 1)) // tm
    tok = jnp.concatenate([jnp.arange(m, dtype=I32)] * TOP_K)
    src = jnp.zeros((n_tiles * tm,), I32).at[dest].set(tok).reshape(n_tiles, 1, tm)
    tile_start = jnp.arange(n_tiles, dtype=I32) * tm
    tile_expert = jnp.minimum(jnp.sum((ends[None, :] <= tile_start[:, None]).astype(I32), axis=1), n_experts - 1)
    n_active = (ends[-1] // tm).astype(I32).reshape(1)
    pos = dest.reshape(TOP_K, m // combine_tile, combine_tile).transpose(1, 0, 2).reshape(
        m // combine_tile, 1, TOP_K * combine_tile).astype(I32)
    return src, tile_expert, n_active, pos


def _moe_ffn(rows, x, gn, mods, router_w, w1, w3, w2, n_tok):
    n_experts = w1.shape[0]
    v, e_idx, e_w = _norm_mod_call(rows, x, gn, mods, 3, 4, F32, router_w=router_w)
    tm = 512 if TOP_K * n_tok >= 8192 else 64
    combine_tile = rows.row_tile((256, 128, 64, 32, 16, 8))
    src, tile_expert, n_active, pos = _moe_dispatch(e_idx[:n_tok], n_experts, tm, combine_tile)
    hs = _expert_up(v, src, w1, w3, tile_expert, n_active, tm)
    ys = _expert_down(hs, w2, tile_expert, n_active, tm)
    return _moe_combine(rows, ys, pos, x, mods, 5, e_w, n_tok)


def kernel(x, c, ctx, c_ctx, ada_w, ada_b, norm_mix, norm_ffn, w_in, conv_w, gla_gate_w, gla_gate_b,
           mlstm_gate_b, gla_out_norm, mlstm_out_norm, w_up_a, w_up_b, w_o, ffn_w1, ffn_w3, ffn_w2,
           router_w, moe_w1, moe_w3, moe_w2, norm_final):
    batch, seq, d = x.shape
    ctx_len = ctx.shape[1]
    depth = ada_w.shape[0]
    assert seq % GRID_W == 0 and seq % SCAN_CHUNK == 0 and ctx_len % SCAN_CHUNK == 0
    assert seq % (batch * ctx_len) == 0 and (batch * ctx_len) % GRID_W == 0
    rows = _Rows(batch, seq, ctx_len)

    h = jnp.concatenate([x.reshape(batch * seq, d), ctx.reshape(batch * ctx_len, d)], axis=0)

    cvec = jnp.zeros((SUBLANES * ((batch + 1 + SUBLANES - 1) // SUBLANES), d), F32)
    cvec = cvec.at[:batch].set(c).at[batch].set(c_ctx)
    mods_all = _ada_mods(cvec, ada_w, ada_b)
    mods_all = mods_all[:, :batch + 1].reshape(depth, batch + 1, 6, 1, d).transpose(0, 2, 1, 3, 4)

    qk_scale = jnp.ones((QK_COLS,), F32)
    qk_scale = qk_scale.at[:QA].set(GLA_DK ** -0.5).at[2 * QA + QB:].set(MLSTM_DQK ** -0.5).reshape(1, QK_COLS)
    c_v, c_ro = QK_COLS, QK_COLS + V_A + V_B
    c_sm = c_ro + V_A + V_B
    c_g = c_sm + 2 * GLA_RANK + N_GATE_B
    n_small = c_g - c_sm

    for l in range(depth):
        mods = mods_all[l]
        wl = w_in[l]
        w_qk = wl[:, :c_v].astype(BF16)
        w_v = wl[:, c_v:c_ro].astype(BF16)
        w_ro = wl[:, c_ro:c_sm].astype(BF16)
        w_sm = jnp.zeros((d, LANES), BF16).at[:, :n_small].set(wl[:, c_sm:c_g].astype(BF16))
        w_g = wl[:, c_g:].astype(BF16)

        u = _norm_mod_call(rows, h, norm_mix[l], mods, 0, 1, BF16)
        p_qk = _matmul(u, w_qk, F32)
        v = _matmul(u, w_v, BF16)
        ro = _matmul(u, w_ro, F32)
        small = _matmul(u, w_sm, F32)
        sg = _matmul(u, w_g, F32, act="sigmoid")
        qk = _conv_silu(rows, p_qk, conv_w[l], qk_scale)
        gates_t = small[:, 2 * GLA_RANK:n_small].T
        oaf, oab = _gla_scan(rows, qk, v, small, gla_gate_w[l], gla_gate_b[l])
        obf, obb = _mlstm_scan(rows, qk, v, small, gates_t, mlstm_gate_b[l])
        ha, hb = _mix_prep(oaf, oab, obf, obb, ro, gla_out_norm[l], mlstm_out_norm[l])
        merged = _up_merge(ha, hb, w_up_a[l].astype(BF16), w_up_b[l].astype(BF16), sg)
        h = _matmul(merged, w_o[l].astype(BF16), F32, res=h, rows=rows, mods=mods, which=2)

        if l % 2 == 0:
            e = l // 2
            vv = _norm_mod_call(rows, h, norm_ffn[l], mods, 3, 4, BF16)
            hid = _swiglu_up(vv, ffn_w1[e].astype(BF16), ffn_w3[e].astype(BF16))
            h = _matmul(hid, ffn_w2[e].astype(BF16), F32, res=h, rows=rows, mods=mods, which=5)
        else:
            e = l // 2
            n_tok = rows.n_lat if l == depth - 1 else rows.m
            h = _moe_ffn(rows, h, norm_ffn[l], mods, router_w[e], moe_w1[e].astype(BF16),
                         moe_w3[e].astype(BF16), moe_w2[e].astype(BF16), n_tok)

    out = _final_norm(rows, h, norm_final)
    return out.reshape(batch, seq, d)
```

```python
import functools
import math

import jax
import jax.numpy as jnp
from jax import lax
from jax.experimental import pallas as pl
from jax.experimental.pallas import tpu as pltpu

F32 = jnp.float32
BF16 = jnp.bfloat16
I32 = jnp.int32

GRID_W = 64
GLA_HEADS = 4
GLA_DK = 128
GLA_DV = 256
GLA_RANK = 16
GLA_GATE_NORMALIZER = 16.0
MLSTM_HEADS = 4
MLSTM_DQK = 128
MLSTM_DV = 256
GATE_SOFT_CAP = 15.0
N_GATE_B = 4 * MLSTM_HEADS
TOP_K = 2
EPS = 1e-6
QA = GLA_HEADS * GLA_DK
QB = MLSTM_HEADS * MLSTM_DQK
QK_COLS = 2 * QA + 2 * QB
V_A = GLA_HEADS * GLA_DV
V_B = MLSTM_HEADS * MLSTM_DV

LANES = 128
SUBLANES = 8
SCAN_CHUNK = 256
VMEM_LIMIT = 52 * 1024 * 1024
NEG_BIG = -1e30
HIGHEST = lax.Precision.HIGHEST


def _cparams(*sem):
    return pltpu.CompilerParams(dimension_semantics=sem, vmem_limit_bytes=VMEM_LIMIT)


def _pick(dim, prefs):
    for p in prefs:
        if dim % p == 0:
            return p
    return dim


def _dot(a, b, precision=None):
    return jnp.dot(a, b, preferred_element_type=F32, precision=precision)


def _dot_nt(a, b, precision=None):
    return lax.dot_general(a, b, (((1,), (1,)), ((), ())), preferred_element_type=F32, precision=precision)


def _dot_tn(a, b):
    return lax.dot_general(a, b, (((0,), (0,)), ((), ())), preferred_element_type=F32)


def _silu(x):
    return x * jax.nn.sigmoid(x)


def _split3(x):
    hi = x.astype(BF16)
    rest = x - hi.astype(F32)
    mid = rest.astype(BF16)
    lo = (rest - mid.astype(F32)).astype(BF16)
    return hi, mid, lo


def _ada_kernel(c_ref, w_ref, b_ref, o_ref):
    o_ref[...] = _dot(_silu(c_ref[...]), w_ref[...], HIGHEST) + b_ref[...]


def _ada_mods(cvec, ada_w, ada_b):
    depth, d, n = ada_w.shape
    tn = _pick(n, (1024, 512, 256, 128))
    return pl.pallas_call(
        _ada_kernel,
        grid=(depth, n // tn),
        in_specs=[pl.BlockSpec(cvec.shape, lambda l, j: (0, 0)),
                  pl.BlockSpec((None, d, tn), lambda l, j: (l, 0, j)),
                  pl.BlockSpec((None, 1, tn), lambda l, j: (l, 0, j))],
        out_specs=pl.BlockSpec((None, cvec.shape[0], tn), lambda l, j: (l, 0, j)),
        out_shape=jax.ShapeDtypeStruct((depth, cvec.shape[0], n), F32),
        compiler_params=_cparams("parallel", "parallel"),
        name="ada_mods",
    )(cvec, ada_w, ada_b.reshape(depth, 1, n))


class _Rows:
    def __init__(self, batch, seq, ctx_len):
        self.batch, self.seq, self.ctx_len = batch, seq, ctx_len
        self.n_lat = batch * seq
        self.n_ctx = batch * ctx_len
        self.m = self.n_lat + self.n_ctx

    def row_tile(self, prefs):
        return _pick(math.gcd(self.seq, self.n_ctx), prefs)

    def gid(self, i, tm):
        r0 = i * tm
        return jnp.where(r0 < self.n_lat, r0 // self.seq, self.batch)


def _mod_spec(rows, which, tm, d):
    return pl.BlockSpec((None, None, 1, d), lambda i: (which, rows.gid(i, tm), 0, 0))


def _norm_mod(x, gn, sh, sc):
    y = x * lax.rsqrt(jnp.mean(x * x, axis=-1, keepdims=True) + EPS) * gn
    return y * (1.0 + sc) + sh


def _norm_mod_kernel(x_ref, gn_ref, sh_ref, sc_ref, o_ref):
    o_ref[...] = _norm_mod(x_ref[...], gn_ref[...], sh_ref[...], sc_ref[...]).astype(o_ref.dtype)


def _norm_mod_router_kernel(x_ref, gn_ref, sh_ref, sc_ref, rw_ref, o_ref, ei_ref, ew_ref, *, n_experts):
    u = _norm_mod(x_ref[...], gn_ref[...], sh_ref[...], sc_ref[...])
    o_ref[...] = u
    logits = _dot(u, rw_ref[...], HIGHEST)
    lane_i = lax.broadcasted_iota(I32, logits.shape, 1)
    lane = lane_i.astype(F32)
    lg = jnp.where(lane_i < n_experts, logits, -jnp.inf)
    m1 = jnp.max(lg, axis=-1, keepdims=True)
    i1 = jnp.min(jnp.where(lg == m1, lane, float(LANES)), axis=-1, keepdims=True)
    lg2 = jnp.where(lane == i1, -jnp.inf, lg)
    m2 = jnp.max(lg2, axis=-1, keepdims=True)
    i2 = jnp.min(jnp.where(lg2 == m2, lane, float(LANES)), axis=-1, keepdims=True)
    e = jnp.exp(m2 - m1)
    w1 = 1.0 / (1.0 + e)
    w2 = e / (1.0 + e)
    ei_ref[...] = jnp.where(lane_i == 0, i1, jnp.where(lane_i == 1, i2, 0.0)).astype(I32)
    ew_ref[...] = jnp.where(lane_i == 0, w1, jnp.where(lane_i == 1, w2, 0.0))


def _norm_mod_call(rows, x, gn, mods, which_sh, which_sc, out_dtype, router_w=None):
    m, d = x.shape
    tm = rows.row_tile((256, 128, 64, 32, 16, 8))
    in_specs = [pl.BlockSpec((tm, d), lambda i: (i, 0)),
                pl.BlockSpec((1, d), lambda i: (0, 0)),
                _mod_spec(rows, which_sh, tm, d),
                _mod_spec(rows, which_sc, tm, d)]
    row_spec = pl.BlockSpec((tm, d), lambda i: (i, 0))
    if router_w is None:
        return pl.pallas_call(
            _norm_mod_kernel, grid=(m // tm,), in_specs=in_specs, out_specs=row_spec,
            out_shape=jax.ShapeDtypeStruct((m, d), out_dtype),
            compiler_params=_cparams("parallel"), name="norm_mod",
        )(x, gn.reshape(1, d), mods, mods)
    n_experts = router_w.shape[1]
    rw = jnp.zeros((d, LANES), F32).at[:, :n_experts].set(router_w)
    lane_spec = pl.BlockSpec((tm, LANES), lambda i: (i, 0))
    return pl.pallas_call(
        functools.partial(_norm_mod_router_kernel, n_experts=n_experts),
        grid=(m // tm,),
        in_specs=in_specs + [pl.BlockSpec((d, LANES), lambda i: (0, 0))],
        out_specs=[row_spec, lane_spec, lane_spec],
        out_shape=[jax.ShapeDtypeStruct((m, d), F32), jax.ShapeDtypeStruct((m, LANES), I32),
                   jax.ShapeDtypeStruct((m, LANES), F32)],
        compiler_params=_cparams("parallel"), name="norm_mod_router",
    )(x, gn.reshape(1, d), mods, mods, rw)


def _final_norm_kernel(x_ref, gn_ref, o_ref):
    x = x_ref[...]
    o_ref[...] = x * lax.rsqrt(jnp.mean(x * x, axis=-1, keepdims=True) + EPS) * gn_ref[...]


def _final_norm(rows, x, gn):
    d = x.shape[1]
    tm = rows.row_tile((256, 128, 64, 32, 16, 8))
    return pl.pallas_call(
        _final_norm_kernel, grid=(rows.n_lat // tm,),
        in_specs=[pl.BlockSpec((tm, d), lambda i: (i, 0)), pl.BlockSpec((1, d), lambda i: (0, 0))],
        out_specs=pl.BlockSpec((tm, d), lambda i: (i, 0)),
        out_shape=jax.ShapeDtypeStruct((rows.n_lat, d), F32),
        compiler_params=_cparams("parallel"), name="final_norm",
    )(x, gn.reshape(1, d))


def _stash_weights(pairs):
    @pl.when(pl.program_id(1) == 0)
    def _():
        for w_ref, wbuf in pairs:
            wbuf[...] = w_ref[...].astype(BF16)


def _w_spec(kdim, tn, layer, col_blk0):
    return pl.BlockSpec((None, kdim, tn), lambda j, i: (layer, 0, col_blk0 + j))


def _mm_kernel(*refs, act, residual):
    if residual:
        a_ref, w_ref, x_ref, g_ref, o_ref, wbuf = refs
    else:
        a_ref, w_ref, o_ref, wbuf = refs
    _stash_weights([(w_ref, wbuf)])
    r = _dot(a_ref[...], wbuf[...])
    if act == "sigmoid":
        r = jax.nn.sigmoid(r)
    if residual:
        r = x_ref[...] + g_ref[...] * r
    o_ref[...] = r.astype(o_ref.dtype)


def _matmul(a, w, layer, col0, n, out_dtype, *, act=None, res=None, rows=None, mods=None, which=None):
    m, kdim = a.shape
    tn = _pick(n, (1024, 512, 256, 128) if kdim <= 2048 else (512, 256, 128))
    assert col0 % tn == 0
    row_prefs = (1024, 512, 256, 128, 64, 32, 16, 8) if kdim <= 2048 else (512, 256, 128, 64, 32, 16, 8)
    tm = _pick(m, row_prefs) if rows is None else rows.row_tile(row_prefs)
    residual = res is not None
    in_specs = [pl.BlockSpec((tm, kdim), lambda j, i: (i, 0)), _w_spec(kdim, tn, layer, col0 // tn)]
    args = [a, w]
    if residual:
        in_specs += [pl.BlockSpec((tm, tn), lambda j, i: (i, j)),
                     pl.BlockSpec((None, None, 1, tn), lambda j, i: (which, rows.gid(i, tm), 0, j))]
        args += [res, mods]
    return pl.pallas_call(
        functools.partial(_mm_kernel, act=act, residual=residual),
        grid=(n // tn, m // tm),
        in_specs=in_specs,
        out_specs=pl.BlockSpec((tm, tn), lambda j, i: (i, j)),
        out_shape=jax.ShapeDtypeStruct((m, n), out_dtype),
        scratch_shapes=[pltpu.VMEM((kdim, tn), BF16)],
        compiler_params=_cparams("parallel", "arbitrary"),
        name="matmul",
    )(*args)


def _swiglu_kernel(a_ref, w1_ref, w3_ref, o_ref, w1buf, w3buf):
    _stash_weights([(w1_ref, w1buf), (w3_ref, w3buf)])
    a = a_ref[...]
    o_ref[...] = (_silu(_dot(a, w1buf[...])) * _dot(a, w3buf[...])).astype(o_ref.dtype)


def _swiglu_up(a, w1, w3, layer):
    m, d = a.shape
    f = w1.shape[2]
    tm = _pick(m, (1024, 512, 256, 128, 64, 32, 16, 8))
    tn = _pick(f, (512, 256, 128))
    return pl.pallas_call(
        _swiglu_kernel, grid=(f // tn, m // tm),
        in_specs=[pl.BlockSpec((tm, d), lambda j, i: (i, 0)),
                  _w_spec(d, tn, layer, 0), _w_spec(d, tn, layer, 0)],
        out_specs=pl.BlockSpec((tm, tn), lambda j, i: (i, j)),
        out_shape=jax.ShapeDtypeStruct((m, f), BF16),
        scratch_shapes=[pltpu.VMEM((d, tn), BF16), pltpu.VMEM((d, tn), BF16)],
        compiler_params=_cparams("parallel", "arbitrary"), name="swiglu_up",
    )(a, w1, w3)


def _up_merge_kernel(ha_ref, hb_ref, wa_ref, wb_ref, sga_ref, sgb_ref, o_ref, wabuf, wbbuf):
    _stash_weights([(wa_ref, wabuf), (wb_ref, wbbuf)])
    up_a = _dot(ha_ref[...], wabuf[...])
    up_b = _dot(hb_ref[...], wbbuf[...])
    o_ref[...] = (sga_ref[...] * up_a + sgb_ref[...] * up_b).astype(o_ref.dtype)


def _up_merge(ha, hb, wa, wb, layer, sg):
    m, va = ha.shape
    vb = hb.shape[1]
    d = wa.shape[2]
    tm = _pick(m, (512, 256, 128, 64, 32, 16, 8))
    tn = _pick(d, (1024, 512, 256, 128))
    nj = d // tn
    return pl.pallas_call(
        _up_merge_kernel, grid=(nj, m // tm),
        in_specs=[pl.BlockSpec((tm, va), lambda j, i: (i, 0)),
                  pl.BlockSpec((tm, vb), lambda j, i: (i, 0)),
                  _w_spec(va, tn, layer, 0), _w_spec(vb, tn, layer, 0),
                  pl.BlockSpec((tm, tn), lambda j, i: (i, j)),
                  pl.BlockSpec((tm, tn), lambda j, i: (i, j + nj))],
        out_specs=pl.BlockSpec((tm, tn), lambda j, i: (i, j)),
        out_shape=jax.ShapeDtypeStruct((m, d), BF16),
        scratch_shapes=[pltpu.VMEM((va, tn), BF16), pltpu.VMEM((vb, tn), BF16)],
        compiler_params=_cparams("parallel", "arbitrary"), name="up_merge",
    )(ha, hb, wa, wb, sg, sg)


def _conv_taps(xs, w, dy, not_first, not_last):
    n = xs.shape[0]
    left = jnp.where(not_first, pltpu.roll(xs, 1, 0), 0.0)
    right = jnp.where(not_last, pltpu.roll(xs, n - 1, 0), 0.0)
    return w[3 * dy:3 * dy + 1] * left + w[3 * dy + 1:3 * dy + 2] * xs + w[3 * dy + 2:3 * dy + 3] * right


def _conv_kernel(up_ref, x_ref, dn_ref, w_ref, s_ref, o_ref, pad_ref, *, rows, rb, strip):
    r0 = pl.program_id(0) * rb
    w = w_ref[...]

    @pl.when(r0 < rows.n_lat)
    def _():
        tc = x_ref.shape[-1]
        at_start = lax.rem(r0, rows.seq) == 0
        at_end = lax.rem(r0 + rb, rows.seq) == 0
        pad_ref[pl.ds(0, GRID_W), :] = jnp.where(at_start, 0.0, up_ref[...])
        pad_ref[pl.ds(GRID_W + rb, GRID_W), :] = jnp.where(at_end, 0.0, dn_ref[...])
        pad_ref[pl.ds(GRID_W, rb), :] = x_ref[...]
        col = lax.broadcasted_iota(I32, (strip, tc), 0) % GRID_W
        not_first = col != 0
        not_last = col != GRID_W - 1
        for s in range(rb // strip):
            acc = jnp.zeros((strip, tc), F32)
            for dy in range(3):
                xs = pad_ref[pl.ds(s * strip + dy * GRID_W, strip), :]
                acc = acc + _conv_taps(xs, w, dy, not_first, not_last)
            o_ref[pl.ds(s * strip, strip), :] = _silu(acc) * s_ref[...]

    @pl.when(r0 >= rows.n_lat)
    def _():
        x = x_ref[...]
        pos = lax.broadcasted_iota(I32, x.shape, 0) % rows.ctx_len
        acc = _conv_taps(x, w, 1, pos != 0, pos != rows.ctx_len - 1)
        o_ref[...] = _silu(acc) * s_ref[...]


def _conv_silu(rows, p_qk, conv_w, scale):
    m, c = p_qk.shape
    tc = _pick(c, (512, 256, 128))
    rb = rows.n_ctx
    strip = _pick(rb, (512, 256, 128, 64))
    per = rb // GRID_W
    last = m // GRID_W - 1
    return pl.pallas_call(
        functools.partial(_conv_kernel, rows=rows, rb=rb, strip=strip),
        grid=(m // rb, c // tc),
        in_specs=[pl.BlockSpec((GRID_W, tc), lambda i, j: (jnp.maximum(i * per - 1, 0), j)),
                  pl.BlockSpec((rb, tc), lambda i, j: (i, j)),
                  pl.BlockSpec((GRID_W, tc), lambda i, j: (jnp.minimum((i + 1) * per, last), j)),
                  pl.BlockSpec((9, tc), lambda i, j: (0, j)),
                  pl.BlockSpec((1, tc), lambda i, j: (0, j))],
        out_specs=pl.BlockSpec((rb, tc), lambda i, j: (i, j)),
        out_shape=jax.ShapeDtypeStruct((m, c), F32),
        scratch_shapes=[pltpu.VMEM((rb + 2 * GRID_W, tc), F32)],
        compiler_params=_cparams("parallel", "parallel"), name="conv_silu",
    )(p_qk, p_qk, p_qk, conv_w.reshape(9, c), scale)


def _causal_masks(n):
    t = lax.broadcasted_iota(I32, (n, n), 0)
    s = lax.broadcasted_iota(I32, (n, n), 1)
    return t, s, (s <= t, s >= t)


def _block_ref_rows(x, blk, row):
    n, c = x.shape
    if blk >= SUBLANES:
        x3 = x.reshape(n // blk, blk, c)
        return jnp.broadcast_to(x3[:, row:row + 1, :], x3.shape).reshape(n, c)
    x3 = x.reshape(n // SUBLANES, SUBLANES, c)
    sub = lax.broadcasted_iota(I32, x3.shape, 1)
    out = jnp.zeros_like(x3)
    for g in range(SUBLANES // blk):
        r = g * blk + row
        out = jnp.where(sub // blk == g, jnp.broadcast_to(x3[:, r:r + 1, :], x3.shape), out)
    return out.reshape(n, c)


def _gla_direction(d, q, k, v, lr, wg, bg, s_ref, t_idx, s_idx, causal):
    n = q.shape[0]
    lh, lm, _ = _split3(lr)
    wh, wm, _ = _split3(wg)
    z = _dot(jnp.concatenate([lh, lh, lm], axis=1), jnp.concatenate([wh, wm, wh], axis=0)) + bg
    la = jax.nn.log_sigmoid(z) * (1.0 / GLA_GATE_NORMALIZER)
    tri = causal.astype(BF16)
    c3 = _dot(tri, jnp.concatenate(_split3(la), axis=1))
    nk = la.shape[1]
    cum = c3[:, :nk] + c3[:, nk:2 * nk] + c3[:, 2 * nk:]
    total = cum[n - 1:n] if d == 0 else cum[0:1]
    state = s_ref[...]
    out = _dot((q * jnp.exp(cum)).astype(BF16), state.astype(BF16))
    lev = jnp.where(causal, 31 - lax.clz(t_idx ^ s_idx), -2).astype(BF16)
    qb = q.astype(BF16)
    kb = k.astype(BF16)
    scores = jnp.where(lev == -1, _dot_nt(qb, kb).astype(BF16), jnp.zeros((), BF16))
    for l in range(n.bit_length() - 1):
        half = 1 << l
        ref = _block_ref_rows(cum, 2 * half, half - 1 if d == 0 else half)
        e = jnp.exp(-jnp.abs(cum - ref)).astype(BF16)
        p = _dot_nt(qb * e, kb * e).astype(BF16)
        scores = jnp.where(lev == l, p, scores)
    out = out + _dot(scores, v)
    k_out = (k * jnp.exp(total - cum)).astype(BF16)
    dk = state.shape[0]
    et = jnp.broadcast_to(jnp.exp(total), (dk, dk)).T
    scale = jnp.concatenate([et] * (state.shape[1] // dk), axis=1)
    s_ref[...] = scale * state + _dot_tn(k_out, v)
    return out


def _gla_kernel(qf, kf, vf, smf, qb, kb, vb, smb, wg_ref, bg_ref, of_ref, ob_ref, sf_ref, sb_ref):
    @pl.when(pl.program_id(1) == 0)
    def _():
        sf_ref[...] = jnp.zeros_like(sf_ref)
        sb_ref[...] = jnp.zeros_like(sb_ref)

    n = qf.shape[0]
    t_idx, s_idx, causal = _causal_masks(n)
    dirs = ((qf, kf, vf, smf, of_ref, sf_ref), (qb, kb, vb, smb, ob_ref, sb_ref))
    for d, (q, k, v, sm, o_ref, s_ref) in enumerate(dirs):
        lr = sm[:, d * GLA_RANK:(d + 1) * GLA_RANK]
        for hd in range(GLA_HEADS):
            ks = slice(hd * GLA_DK, (hd + 1) * GLA_DK)
            vs = slice(hd * GLA_DV, (hd + 1) * GLA_DV)
            o_ref[:, vs] = _gla_direction(d, q[:, ks], k[:, ks], v[:, vs], lr, wg_ref[d, :, ks], bg_ref[d, :, ks],
                                          s_ref.at[hd], t_idx, s_idx, causal[d])


def _soft_cap(z):
    return GATE_SOFT_CAP * jnp.tanh(z * (1.0 / GATE_SOFT_CAP))


def _mlstm_direction(d, head, q, k, v, g_col, g_row, c_ref, m_ref, causal):
    n = q.shape[0]
    tri = causal.astype(BF16)
    i_idx = 2 * MLSTM_HEADS * d + head
    f_idx = i_idx + MLSTM_HEADS
    sel_rows = lax.broadcasted_iota(I32, (3 * N_GATE_B, LANES), 0) % N_GATE_B
    rep = lambda a, idx: _dot(jnp.concatenate(_split3(a), axis=1), (sel_rows == idx).astype(BF16))
    i_rep = rep(g_col, i_idx)
    f_rep = rep(jax.nn.log_sigmoid(g_col), f_idx)
    c3 = _dot(tri, jnp.concatenate(_split3(f_rep), axis=1))
    cum = c3[:, :LANES] + c3[:, LANES:2 * LANES] + c3[:, 2 * LANES:]
    r3 = _dot_nt(jnp.concatenate(_split3(jax.nn.log_sigmoid(g_row)), axis=0), tri)
    cum_r_all = r3[:N_GATE_B] + r3[N_GATE_B:2 * N_GATE_B] + r3[2 * N_GATE_B:]
    sub = lax.broadcasted_iota(I32, (N_GATE_B, 1), 0)
    pick_r = lambda a, idx: jnp.sum(jnp.where(sub == idx, a, 0.0), axis=0, keepdims=True)
    i_r, cum_r = pick_r(g_row, i_idx), pick_r(cum_r_all, f_idx)
    wide = lambda a, reps: jnp.concatenate([a] * reps, axis=1)
    nt = n // LANES
    total = cum[n - 1:n] if d == 0 else cum[0:1]
    m_prev = m_ref[...]
    dlog = jnp.where(causal, wide(cum, nt) - cum_r + i_r, NEG_BIG)
    inter_log = cum + m_prev
    row_max = dlog[:, :LANES]
    for t in range(1, nt):
        row_max = jnp.maximum(row_max, dlog[:, t * LANES:(t + 1) * LANES])
    m_t = jnp.maximum(inter_log, jnp.max(row_max, axis=1, keepdims=True))
    w_inter = jnp.exp(inter_log - m_t)
    qb16 = q.astype(BF16)
    s = _dot_nt(qb16, k.astype(BF16)) * jnp.exp(dlog - wide(m_t, nt))
    dv = v.shape[1]
    v_ext = jnp.concatenate([v, jnp.ones((n, LANES), BF16)], axis=1)
    state = c_ref[...]
    acc = wide(w_inter, dv // LANES + 1) * _dot(qb16, state.astype(BF16)) + _dot(s.astype(BF16), v_ext)
    bound = jnp.maximum(jnp.abs(acc[:, dv:]), jnp.exp(-m_t))
    out = acc[:, :dv] / wide(bound, dv // LANES)
    g = total - cum + i_rep
    m_new = jnp.maximum(total + m_prev, jnp.max(g, axis=0, keepdims=True))
    w_c = jnp.exp(total + m_prev - m_new)
    w_k = jnp.exp(g - m_new)
    c_ref[...] = wide(w_c, dv // LANES + 1) * state + _dot_tn((k * w_k).astype(BF16), v_ext)
    m_ref[...] = m_new
    return out


def _mlstm_kernel(qf, kf, vf, gcf, grf, qb, kb, vb, gcb, grb, brow_ref, bcol_ref,
                  of_ref, ob_ref, cf_ref, cb_ref, mf_ref, mb_ref):
    @pl.when(pl.program_id(1) == 0)
    def _():
        cf_ref[...] = jnp.zeros_like(cf_ref)
        cb_ref[...] = jnp.zeros_like(cb_ref)
        mf_ref[...] = jnp.zeros_like(mf_ref)
        mb_ref[...] = jnp.zeros_like(mb_ref)

    n = qf.shape[0]
    _, _, causal = _causal_masks(n)
    g0 = 2 * GLA_RANK
    dirs = ((qf, kf, vf, gcf, grf, of_ref, cf_ref, mf_ref), (qb, kb, vb, gcb, grb, ob_ref, cb_ref, mb_ref))
    for d, (q, k, v, gc, gr, o_ref, c_ref, m_ref) in enumerate(dirs):
        g_col = _soft_cap(gc[:, g0:g0 + N_GATE_B] + brow_ref[...])
        g_row = _soft_cap(gr[...] + bcol_ref[...])
        for hd in range(MLSTM_HEADS):
            qs = slice(hd * MLSTM_DQK, (hd + 1) * MLSTM_DQK)
            vs = slice(hd * MLSTM_DV, (hd + 1) * MLSTM_DV)
            o_ref[:, vs] = _mlstm_direction(d, hd, q[:, qs], k[:, qs], v[:, vs], g_col, g_row,
                                            c_ref.at[hd], m_ref.at[hd], causal[d])


def _scan_row_maps(rows, chunk):
    lat_chunks = rows.seq // chunk
    ctx_chunks = rows.ctx_len // chunk
    ctx0 = rows.n_lat // chunk

    def fwd(b, s):
        return jnp.where(s < ctx_chunks, ctx0 + b * ctx_chunks + s, b * lat_chunks + (s - ctx_chunks))

    def bwd(b, s):
        return jnp.where(s < ctx_chunks, ctx0 + b * ctx_chunks + (ctx_chunks - 1 - s),
                         b * lat_chunks + (lat_chunks - 1 - (s - ctx_chunks)))

    return fwd, bwd, ctx_chunks + lat_chunks


def _gla_scan(rows, qk, v, small, gate_w, gate_b):
    m = qk.shape[0]
    n = SCAN_CHUNK
    fwd, bwd, steps = _scan_row_maps(rows, n)

    def specs(rmap):
        return [pl.BlockSpec((n, QA), lambda b, s: (rmap(b, s), 0)),
                pl.BlockSpec((n, QA), lambda b, s: (rmap(b, s), 1)),
                pl.BlockSpec((n, V_A), lambda b, s: (rmap(b, s), 0)),
                pl.BlockSpec((n, LANES), lambda b, s: (rmap(b, s), 0))]

    out_spec = lambda rmap: pl.BlockSpec((n, V_A), lambda b, s: (rmap(b, s), 0))
    out_sds = jax.ShapeDtypeStruct((m, V_A), F32)
    state = pltpu.VMEM((GLA_HEADS, GLA_DK, GLA_DV), F32)
    return pl.pallas_call(
        _gla_kernel,
        grid=(rows.batch, steps),
        in_specs=specs(fwd) + specs(bwd) + [
            pl.BlockSpec((2, GLA_RANK, QA), lambda b, s: (0, 0, 0)),
            pl.BlockSpec((2, 1, QA), lambda b, s: (0, 0, 0))],
        out_specs=[out_spec(fwd), out_spec(bwd)],
        out_shape=[out_sds, out_sds],
        scratch_shapes=[state, state],
        compiler_params=_cparams("parallel", "arbitrary"), name="gla_scan",
    )(qk, qk, v, small, qk, qk, v, small, gate_w, gate_b.reshape(2, 1, QA))


def _mlstm_scan(rows, qk, v, small, gates_t, gate_b):
    m = qk.shape[0]
    n = SCAN_CHUNK
    fwd, bwd, steps = _scan_row_maps(rows, n)
    q0 = 2 * QA // QB
    v0 = V_A // V_B

    def specs(rmap):
        return [pl.BlockSpec((n, QB), lambda b, s: (rmap(b, s), q0)),
                pl.BlockSpec((n, QB), lambda b, s: (rmap(b, s), q0 + 1)),
                pl.BlockSpec((n, V_B), lambda b, s: (rmap(b, s), v0)),
                pl.BlockSpec((n, LANES), lambda b, s: (rmap(b, s), 0)),
                pl.BlockSpec((N_GATE_B, n), lambda b, s: (0, rmap(b, s)))]

    out_spec = lambda rmap: pl.BlockSpec((n, V_B), lambda b, s: (rmap(b, s), 0))
    out_sds = jax.ShapeDtypeStruct((m, V_B), F32)
    state = pltpu.VMEM((MLSTM_HEADS, MLSTM_DQK, MLSTM_DV + LANES), F32)
    stab = pltpu.VMEM((MLSTM_HEADS, 1, LANES), F32)
    return pl.pallas_call(
        _mlstm_kernel,
        grid=(rows.batch, steps),
        in_specs=specs(fwd) + specs(bwd) + [
            pl.BlockSpec((1, N_GATE_B), lambda b, s: (0, 0)),
            pl.BlockSpec((N_GATE_B, 1), lambda b, s: (0, 0))],
        out_specs=[out_spec(fwd), out_spec(bwd)],
        out_shape=[out_sds, out_sds],
        scratch_shapes=[state, state, stab, stab],
        compiler_params=_cparams("parallel", "arbitrary"), name="mlstm_scan",
    )(qk, qk, v, small, gates_t, qk, qk, v, small, gates_t,
      gate_b.reshape(1, N_GATE_B), gate_b.reshape(N_GATE_B, 1))


def _mix_prep_kernel(oaf, oab, obf, obb, ro_ref, gna_ref, gnb_ref, ha_ref, hb_ref):
    branches = ((oaf, oab, gna_ref, ha_ref, 0, GLA_HEADS, GLA_DV, _silu),
                (obf, obb, gnb_ref, hb_ref, V_A, MLSTM_HEADS, MLSTM_DV, jax.nn.sigmoid))
    for of, ob, gn_ref, h_ref, off, heads, dv, gate_fn in branches:
        for h in range(heads):
            sl = slice(h * dv, (h + 1) * dv)
            o = of[:, sl] + ob[:, sl]
            y = o * lax.rsqrt(jnp.mean(o * o, axis=-1, keepdims=True) + EPS) * gn_ref[:, sl]
            gate = gate_fn(ro_ref[:, off + h * dv:off + (h + 1) * dv])
            h_ref[:, sl] = (y * gate).astype(h_ref.dtype)


def _mix_prep(oaf, oab, obf, obb, ro, gna, gnb):
    m = ro.shape[0]
    tm = _pick(m, (256, 128, 64, 32, 16, 8))
    row = lambda c: pl.BlockSpec((tm, c), lambda i: (i, 0))
    vec = lambda c: pl.BlockSpec((1, c), lambda i: (0, 0))
    return pl.pallas_call(
        _mix_prep_kernel, grid=(m // tm,),
        in_specs=[row(V_A), row(V_A), row(V_B), row(V_B), row(V_A + V_B), vec(V_A), vec(V_B)],
        out_specs=[row(V_A), row(V_B)],
        out_shape=[jax.ShapeDtypeStruct((m, V_A), BF16), jax.ShapeDtypeStruct((m, V_B), BF16)],
        compiler_params=_cparams("parallel"), name="mix_prep",
    )(oaf, oab, obf, obb, ro, gna.reshape(1, V_A), gnb.reshape(1, V_B))


def _expert_up_kernel(te_ref, na_ref, src0_ref, srcn_ref, v_hbm, w1_ref, w3_ref, o_ref, xbuf, abuf, sem,
                      *, tm, issue_steps):
    del te_ref
    i = pl.program_id(0)
    j = pl.program_id(1)
    slot = lax.rem(i, 2)
    per = tm // issue_steps

    def row_copy(src_ref, r, s):
        return pltpu.make_async_copy(v_hbm.at[pl.ds(src_ref[0, r], 1)], xbuf.at[s, pl.ds(r, 1)], sem.at[s])

    @pl.when(jnp.logical_and(i == 0, j == 0))
    def _():
        def start(r, carry):
            row_copy(src0_ref, r, 0).start()
            return carry

        lax.fori_loop(0, tm, start, 0)

    @pl.when(j == 0)
    def _():
        def wait(r, carry):
            row_copy(src0_ref, 0, slot).wait()
            return carry

        lax.fori_loop(0, tm, wait, 0)
        abuf[...] = xbuf[slot].astype(BF16)

    @pl.when(jnp.logical_and(i + 1 < pl.num_programs(0), j < issue_steps))
    def _():
        for t in range(per):
            row_copy(srcn_ref, j * per + t, 1 - slot).start()

    @pl.when(i < na_ref[0])
    def _():
        a = abuf[...]
        o_ref[...] = (_silu(_dot(a, w1_ref[...])) * _dot(a, w3_ref[...])).astype(o_ref.dtype)

    @pl.when(i >= na_ref[0])
    def _():
        o_ref[...] = jnp.zeros_like(o_ref)


def _expert_up(v, src, w1, w3, tile_expert, n_active, tm):
    n_tiles = src.shape[0]
    d = v.shape[1]
    f = w1.shape[2]
    tn = _pick(f, (1408, 512, 256, 128))
    nj = f // tn
    issue_steps = max(s for s in (1, 2, 4, 8) if s <= nj)
    wmap = lambda i, j, te, na: (te[i], 0, jnp.where(i < na[0], j, 0))
    return pl.pallas_call(
        functools.partial(_expert_up_kernel, tm=tm, issue_steps=issue_steps),
        grid_spec=pltpu.PrefetchScalarGridSpec(
            num_scalar_prefetch=2, grid=(n_tiles, nj),
            in_specs=[pl.BlockSpec((None, 1, tm), lambda i, j, te, na: (0, 0, 0), memory_space=pltpu.SMEM),
                      pl.BlockSpec((None, 1, tm), lambda i, j, te, na: (jnp.minimum(i + 1, n_tiles - 1), 0, 0),
                                   memory_space=pltpu.SMEM),
                      pl.BlockSpec(memory_space=pl.ANY),
                      pl.BlockSpec((None, d, tn), wmap),
                      pl.BlockSpec((None, d, tn), wmap)],
            out_specs=pl.BlockSpec((tm, tn), lambda i, j, te, na: (i, j)),
            scratch_shapes=[pltpu.VMEM((2, tm, d), F32), pltpu.VMEM((tm, d), BF16),
                            pltpu.SemaphoreType.DMA((2,))]),
        out_shape=jax.ShapeDtypeStruct((n_tiles * tm, f), BF16),
        compiler_params=_cparams("arbitrary", "arbitrary"), name="expert_up",
    )(tile_expert, n_active, src, src, v, w1, w3)


def _expert_down_kernel(te_ref, na_ref, a_ref, w_ref, o_ref):
    del te_ref
    i = pl.program_id(0)

    @pl.when(i < na_ref[0])
    def _():
        o_ref[...] = _dot(a_ref[...], w_ref[...])

    @pl.when(i >= na_ref[0])
    def _():
        o_ref[...] = jnp.zeros_like(o_ref)


def _expert_down(hs, w2, tile_expert, n_active, tm):
    p, f = hs.shape
    d = w2.shape[2]
    tn = _pick(d, (1024, 512, 256, 128))
    return pl.pallas_call(
        _expert_down_kernel,
        grid_spec=pltpu.PrefetchScalarGridSpec(
            num_scalar_prefetch=2, grid=(p // tm, d // tn),
            in_specs=[pl.BlockSpec((tm, f), lambda i, j, te, na: (jnp.where(i < na[0], i, 0), 0)),
                      pl.BlockSpec((None, f, tn), lambda i, j, te, na: (te[i], 0, jnp.where(i < na[0], j, 0)))],
            out_specs=pl.BlockSpec((tm, tn), lambda i, j, te, na: (i, j))),
        out_shape=jax.ShapeDtypeStruct((p, d), F32),
        compiler_params=_cparams("parallel", "parallel"), name="expert_down",
    )(tile_expert, n_active, hs, w2)


def _moe_combine_kernel(pos_ref, posn_ref, ys_hbm, x_ref, g_ref, w_ref, o_ref, buf, sem, *, tile):
    i = pl.program_id(0)
    slot = lax.rem(i, 2)

    def row_copy(p_ref, k, r, s):
        return pltpu.make_async_copy(ys_hbm.at[pl.ds(p_ref[0, k * tile + r], 1)],
                                     buf.at[s, k, pl.ds(r, 1)], sem.at[s])

    def start_tile(p_ref, s):
        for k in range(TOP_K):
            def start(r, carry):
                row_copy(p_ref, k, r, s).start()
                return carry

            lax.fori_loop(0, tile, start, 0, unroll=8)

    @pl.when(i == 0)
    def _():
        start_tile(pos_ref, 0)

    @pl.when(i + 1 < pl.num_programs(0))
    def _():
        start_tile(posn_ref, 1 - slot)

    def wait(r, carry):
        row_copy(pos_ref, 0, 0, slot).wait()
        return carry

    lax.fori_loop(0, TOP_K * tile, wait, 0, unroll=8)
    w = w_ref[...]
    y = w[:, 0:1] * buf[slot, 0] + w[:, 1:2] * buf[slot, 1]
    o_ref[...] = x_ref[...] + g_ref[...] * y


def _moe_combine(rows, ys, pos, x, mods, which, ew, m):
    d = x.shape[1]
    tile = rows.row_tile((256, 128, 64, 32, 16, 8))
    n = m // tile
    pos_spec = lambda imap: pl.BlockSpec((None, 1, TOP_K * tile), imap, memory_space=pltpu.SMEM)
    return pl.pallas_call(
        functools.partial(_moe_combine_kernel, tile=tile),
        grid=(n,),
        in_specs=[pos_spec(lambda i: (i, 0, 0)),
                  pos_spec(lambda i: (jnp.minimum(i + 1, n - 1), 0, 0)),
                  pl.BlockSpec(memory_space=pl.ANY),
                  pl.BlockSpec((tile, d), lambda i: (i, 0)),
                  _mod_spec(rows, which, tile, d),
                  pl.BlockSpec((tile, LANES), lambda i: (i, 0))],
        out_specs=pl.BlockSpec((tile, d), lambda i: (i, 0)),
        out_shape=jax.ShapeDtypeStruct((m, d), F32),
        scratch_shapes=[pltpu.VMEM((2, TOP_K, tile, d), F32), pltpu.SemaphoreType.DMA((2,))],
        compiler_params=_cparams("arbitrary"), name="moe_combine",
    )(pos, pos, ys, x, mods, ew)


def _moe_dispatch(e_idx, n_experts, tm, combine_tile):
    m = e_idx.shape[0]
    ex = jnp.concatenate([e_idx[:, 0], e_idx[:, 1]])
    onehot = (ex[:, None] == jnp.arange(n_experts, dtype=I32)[None, :]).astype(I32)
    rank = jnp.sum((jnp.cumsum(onehot, axis=0) - onehot) * onehot, axis=1)
    counts = jnp.sum(onehot, axis=0)
    padded = (counts + tm - 1) // tm * tm
    ends = jnp.cumsum(padded)
    dest = (ends - padded)[ex] + rank
    n_tiles = (TOP_K * m + n_experts * (tm - 1)) // tm
    tok = jnp.concatenate([jnp.arange(m, dtype=I32)] * TOP_K)
    src = jnp.zeros((n_tiles * tm,), I32).at[dest].set(tok).reshape(n_tiles, 1, tm)
    tile_start = jnp.arange(n_tiles, dtype=I32) * tm
    tile_expert = jnp.minimum(jnp.sum((ends[None, :] <= tile_start[:, None]).astype(I32), axis=1), n_experts - 1)
    n_active = (ends[-1] // tm).astype(I32).reshape(1)
    pos = dest.reshape(TOP_K, m // combine_tile, combine_tile).transpose(1, 0, 2).reshape(
        m // combine_tile, 1, TOP_K * combine_tile).astype(I32)
    return src, tile_expert, n_active, pos


def _moe_ffn(rows, x, gn, mods, router_w, w1, w3, w2, n_tok):
    n_experts = w1.shape[0]
    v, e_idx, e_w = _norm_mod_call(rows, x, gn, mods, 3, 4, F32, router_w=router_w)
    tm = 512 if TOP_K * n_tok >= 8192 else 64
    combine_tile = rows.row_tile((256, 128, 64, 32, 16, 8))
    src, tile_expert, n_active, pos = _moe_dispatch(e_idx[:n_tok], n_experts, tm, combine_tile)
    hs = _expert_up(v, src, w1, w3, tile_expert, n_active, tm)
    ys = _expert_down(hs, w2, tile_expert, n_active, tm)
    return _moe_combine(rows, ys, pos, x, mods, 5, e_w, n_tok)


def kernel(x, c, ctx, c_ctx, ada_w, ada_b, norm_mix, norm_ffn, w_in, conv_w, gla_gate_w, gla_gate_b,
           mlstm_gate_b, gla_out_norm, mlstm_out_norm, w_up_a, w_up_b, w_o, ffn_w1, ffn_w3, ffn_w2,
           router_w, moe_w1, moe_w3, moe_w2, norm_final):
    batch, seq, d = x.shape
    ctx_len = ctx.shape[1]
    depth = ada_w.shape[0]
    assert seq % GRID_W == 0 and seq % SCAN_CHUNK == 0 and ctx_len % SCAN_CHUNK == 0
    assert seq % (batch * ctx_len) == 0 and (batch * ctx_len) % GRID_W == 0
    rows = _Rows(batch, seq, ctx_len)

    h = jnp.concatenate([x.reshape(batch * seq, d), ctx.reshape(batch * ctx_len, d)], axis=0)

    cvec = jnp.zeros((SUBLANES * ((batch + 1 + SUBLANES - 1) // SUBLANES), d), F32)
    cvec = cvec.at[:batch].set(c).at[batch].set(c_ctx)
    mods_all = _ada_mods(cvec, ada_w, ada_b)
    mods_all = mods_all[:, :batch + 1].reshape(depth, batch + 1, 6, 1, d).transpose(0, 2, 1, 3, 4)

    qk_scale = jnp.ones((QK_COLS,), F32)
    qk_scale = qk_scale.at[:QA].set(GLA_DK ** -0.5).at[2 * QA + QB:].set(MLSTM_DQK ** -0.5).reshape(1, QK_COLS)
    c_v, c_ro = QK_COLS, QK_COLS + V_A + V_B
    c_sm = c_ro + V_A + V_B
    c_g = c_sm + 2 * GLA_RANK + N_GATE_B
    n_small = c_g - c_sm

    for l in range(depth):
        mods = mods_all[l]
        w_sm = jnp.zeros((1, d, LANES), F32).at[0, :, :n_small].set(w_in[l, :, c_sm:c_g])
        w_g = w_in[l, :, c_g:][None]

        u = _norm_mod_call(rows, h, norm_mix[l], mods, 0, 1, BF16)
        p_qk = _matmul(u, w_in, l, 0, QK_COLS, F32)
        v = _matmul(u, w_in, l, c_v, V_A + V_B, BF16)
        ro = _matmul(u, w_in, l, c_ro, V_A + V_B, F32)
        small = _matmul(u, w_sm, 0, 0, LANES, F32)
        sg = _matmul(u, w_g, 0, 0, 2 * d, F32, act="sigmoid")
        qk = _conv_silu(rows, p_qk, conv_w[l], qk_scale)
        gates_t = small[:, 2 * GLA_RANK:n_small].T
        oaf, oab = _gla_scan(rows, qk, v, small, gla_gate_w[l], gla_gate_b[l])
        obf, obb = _mlstm_scan(rows, qk, v, small, gates_t, mlstm_gate_b[l])
        ha, hb = _mix_prep(oaf, oab, obf, obb, ro, gla_out_norm[l], mlstm_out_norm[l])
        merged = _up_merge(ha, hb, w_up_a, w_up_b, l, sg)
        h = _matmul(merged, w_o, l, 0, d, F32, res=h, rows=rows, mods=mods, which=2)

        if l % 2 == 0:
            e = l // 2
            vv = _norm_mod_call(rows, h, norm_ffn[l], mods, 3, 4, BF16)
            hid = _swiglu_up(vv, ffn_w1, ffn_w3, e)
            h = _matmul(hid, ffn_w2, e, 0, d, F32, res=h, rows=rows, mods=mods, which=5)
        else:
            e = l // 2
            n_tok = rows.n_lat if l == depth - 1 else rows.m
            h = _moe_ffn(rows, h, norm_ffn[l], mods, router_w[e], moe_w1[e].astype(BF16),
                         moe_w3[e].astype(BF16), moe_w2[e].astype(BF16), n_tok)

    out = _final_norm(rows, h, norm_final)
    return out.reshape(batch, seq, d)
```

```python
import functools
import math

import jax
import jax.numpy as jnp
from jax import lax
from jax.experimental import pallas as pl
from jax.experimental.pallas import tpu as pltpu

F32 = jnp.float32
BF16 = jnp.bfloat16
I32 = jnp.int32

GRID_W = 64
GLA_HEADS = 4
GLA_DK = 128
GLA_DV = 256
GLA_RANK = 16
GLA_GATE_NORMALIZER = 16.0
MLSTM_HEADS = 4
MLSTM_DQK = 128
MLSTM_DV = 256
GATE_SOFT_CAP = 15.0
N_GATE_B = 4 * MLSTM_HEADS
TOP_K = 2
EPS = 1e-6
QA = GLA_HEADS * GLA_DK
QB = MLSTM_HEADS * MLSTM_DQK
QK_COLS = 2 * QA + 2 * QB
V_A = GLA_HEADS * GLA_DV
V_B = MLSTM_HEADS * MLSTM_DV

LANES = 128
SUBLANES = 8
SCAN_CHUNK = 256
VMEM_LIMIT = 52 * 1024 * 1024
NEG_BIG = -1e30
HIGHEST = lax.Precision.HIGHEST


def _cparams(*sem):
    return pltpu.CompilerParams(dimension_semantics=sem, vmem_limit_bytes=VMEM_LIMIT)


def _pick(dim, prefs):
    for p in prefs:
        if dim % p == 0:
            return p
    return dim


def _dot(a, b, precision=None):
    return jnp.dot(a, b, preferred_element_type=F32, precision=precision)


def _dot_nt(a, b, precision=None):
    return lax.dot_general(a, b, (((1,), (1,)), ((), ())), preferred_element_type=F32, precision=precision)


def _dot_tn(a, b):
    return lax.dot_general(a, b, (((0,), (0,)), ((), ())), preferred_element_type=F32)


def _silu(x):
    return x * jax.nn.sigmoid(x)


def _split3(x):
    hi = x.astype(BF16)
    rest = x - hi.astype(F32)
    mid = rest.astype(BF16)
    lo = (rest - mid.astype(F32)).astype(BF16)
    return hi, mid, lo


def _ada_kernel(c_ref, w_ref, b_ref, o_ref):
    o_ref[...] = _dot(_silu(c_ref[...]), w_ref[...], HIGHEST) + b_ref[...]


def _ada_mods(cvec, ada_w, ada_b):
    depth, d, n = ada_w.shape
    tn = _pick(n, (1024, 512, 256, 128))
    return pl.pallas_call(
        _ada_kernel,
        grid=(depth, n // tn),
        in_specs=[pl.BlockSpec(cvec.shape, lambda l, j: (0, 0)),
                  pl.BlockSpec((None, d, tn), lambda l, j: (l, 0, j)),
                  pl.BlockSpec((None, 1, tn), lambda l, j: (l, 0, j))],
        out_specs=pl.BlockSpec((None, cvec.shape[0], tn), lambda l, j: (l, 0, j)),
        out_shape=jax.ShapeDtypeStruct((depth, cvec.shape[0], n), F32),
        compiler_params=_cparams("parallel", "parallel"),
        name="ada_mods",
    )(cvec, ada_w, ada_b.reshape(depth, 1, n))


class _Rows:
    def __init__(self, batch, seq, ctx_len):
        self.batch, self.seq, self.ctx_len = batch, seq, ctx_len
        self.n_lat = batch * seq
        self.n_ctx = batch * ctx_len
        self.m = self.n_lat + self.n_ctx

    def row_tile(self, prefs):
        return _pick(math.gcd(self.seq, self.n_ctx), prefs)

    def gid(self, i, tm):
        r0 = i * tm
        return jnp.where(r0 < self.n_lat, r0 // self.seq, self.batch)


def _mod_spec(rows, which, tm, d):
    return pl.BlockSpec((None, None, 1, d), lambda i: (which, rows.gid(i, tm), 0, 0))


def _norm_mod(x, gn, sh, sc):
    y = x * lax.rsqrt(jnp.mean(x * x, axis=-1, keepdims=True) + EPS) * gn
    return y * (1.0 + sc) + sh


def _norm_mod_kernel(x_ref, gn_ref, sh_ref, sc_ref, o_ref):
    o_ref[...] = _norm_mod(x_ref[...], gn_ref[...], sh_ref[...], sc_ref[...]).astype(o_ref.dtype)


def _norm_mod_router_kernel(x_ref, gn_ref, sh_ref, sc_ref, rw_ref, o_ref, ei_ref, ew_ref, *, n_experts):
    u = _norm_mod(x_ref[...], gn_ref[...], sh_ref[...], sc_ref[...])
    o_ref[...] = u
    logits = _dot(u, rw_ref[...], HIGHEST)
    lane_i = lax.broadcasted_iota(I32, logits.shape, 1)
    lane = lane_i.astype(F32)
    lg = jnp.where(lane_i < n_experts, logits, -jnp.inf)
    m1 = jnp.max(lg, axis=-1, keepdims=True)
    i1 = jnp.min(jnp.where(lg == m1, lane, float(LANES)), axis=-1, keepdims=True)
    lg2 = jnp.where(lane == i1, -jnp.inf, lg)
    m2 = jnp.max(lg2, axis=-1, keepdims=True)
    i2 = jnp.min(jnp.where(lg2 == m2, lane, float(LANES)), axis=-1, keepdims=True)
    e = jnp.exp(m2 - m1)
    w1 = 1.0 / (1.0 + e)
    w2 = e / (1.0 + e)
    ei_ref[...] = jnp.where(lane_i == 0, i1, jnp.where(lane_i == 1, i2, 0.0)).astype(I32)
    ew_ref[...] = jnp.where(lane_i == 0, w1, jnp.where(lane_i == 1, w2, 0.0))


def _norm_mod_call(rows, x, gn, mods, which_sh, which_sc, out_dtype, router_w=None):
    m, d = x.shape
    tm = rows.row_tile((256, 128, 64, 32, 16, 8))
    in_specs = [pl.BlockSpec((tm, d), lambda i: (i, 0)),
                pl.BlockSpec((1, d), lambda i: (0, 0)),
                _mod_spec(rows, which_sh, tm, d),
                _mod_spec(rows, which_sc, tm, d)]
    row_spec = pl.BlockSpec((tm, d), lambda i: (i, 0))
    if router_w is None:
        return pl.pallas_call(
            _norm_mod_kernel, grid=(m // tm,), in_specs=in_specs, out_specs=row_spec,
            out_shape=jax.ShapeDtypeStruct((m, d), out_dtype),
            compiler_params=_cparams("parallel"), name="norm_mod",
        )(x, gn.reshape(1, d), mods, mods)
    n_experts = router_w.shape[1]
    rw = jnp.zeros((d, LANES), F32).at[:, :n_experts].set(router_w)
    lane_spec = pl.BlockSpec((tm, LANES), lambda i: (i, 0))
    return pl.pallas_call(
        functools.partial(_norm_mod_router_kernel, n_experts=n_experts),
        grid=(m // tm,),
        in_specs=in_specs + [pl.BlockSpec((d, LANES), lambda i: (0, 0))],
        out_specs=[row_spec, lane_spec, lane_spec],
        out_shape=[jax.ShapeDtypeStruct((m, d), F32), jax.ShapeDtypeStruct((m, LANES), I32),
                   jax.ShapeDtypeStruct((m, LANES), F32)],
        compiler_params=_cparams("parallel"), name="norm_mod_router",
    )(x, gn.reshape(1, d), mods, mods, rw)


def _final_norm_kernel(x_ref, gn_ref, o_ref):
    x = x_ref[...]
    o_ref[...] = x * lax.rsqrt(jnp.mean(x * x, axis=-1, keepdims=True) + EPS) * gn_ref[...]


def _final_norm(rows, x, gn):
    d = x.shape[1]
    tm = rows.row_tile((256, 128, 64, 32, 16, 8))
    return pl.pallas_call(
        _final_norm_kernel, grid=(rows.n_lat // tm,),
        in_specs=[pl.BlockSpec((tm, d), lambda i: (i, 0)), pl.BlockSpec((1, d), lambda i: (0, 0))],
        out_specs=pl.BlockSpec((tm, d), lambda i: (i, 0)),
        out_shape=jax.ShapeDtypeStruct((rows.n_lat, d), F32),
        compiler_params=_cparams("parallel"), name="final_norm",
    )(x, gn.reshape(1, d))


def _stash_weights(pairs):
    @pl.when(pl.program_id(1) == 0)
    def _():
        for w_ref, wbuf in pairs:
            wbuf[...] = w_ref[...].astype(BF16)


def _w_spec(kdim, tn, layer, col_blk0):
    return pl.BlockSpec((None, kdim, tn), lambda j, i: (layer, 0, col_blk0 + j))


def _mm_kernel(*refs, act, residual):
    if residual:
        a_ref, w_ref, x_ref, g_ref, o_ref, wbuf = refs
    else:
        a_ref, w_ref, o_ref, wbuf = refs
    _stash_weights([(w_ref, wbuf)])
    r = _dot(a_ref[...], wbuf[...])
    if act == "sigmoid":
        r = jax.nn.sigmoid(r)
    if residual:
        r = x_ref[...] + g_ref[...] * r
    o_ref[...] = r.astype(o_ref.dtype)


def _matmul(a, w, layer, col0, n, out_dtype, *, act=None, res=None, rows=None, mods=None, which=None):
    m, kdim = a.shape
    tn = _pick(n, (1024, 512, 256, 128) if kdim <= 2048 else (512, 256, 128))
    assert col0 % tn == 0
    row_prefs = (1024, 512, 256, 128, 64, 32, 16, 8) if kdim <= 2048 else (512, 256, 128, 64, 32, 16, 8)
    tm = _pick(m, row_prefs) if rows is None else rows.row_tile(row_prefs)
    residual = res is not None
    in_specs = [pl.BlockSpec((tm, kdim), lambda j, i: (i, 0)), _w_spec(kdim, tn, layer, col0 // tn)]
    args = [a, w]
    if residual:
        in_specs += [pl.BlockSpec((tm, tn), lambda j, i: (i, j)),
                     pl.BlockSpec((None, None, 1, tn), lambda j, i: (which, rows.gid(i, tm), 0, j))]
        args += [res, mods]
    return pl.pallas_call(
        functools.partial(_mm_kernel, act=act, residual=residual),
        grid=(n // tn, m // tm),
        in_specs=in_specs,
        out_specs=pl.BlockSpec((tm, tn), lambda j, i: (i, j)),
        out_shape=jax.ShapeDtypeStruct((m, n), out_dtype),
        scratch_shapes=[pltpu.VMEM((kdim, tn), BF16)],
        compiler_params=_cparams("parallel", "arbitrary"),
        name="matmul",
    )(*args)


def _swiglu_kernel(a_ref, w1_ref, w3_ref, o_ref, w1buf, w3buf):
    _stash_weights([(w1_ref, w1buf), (w3_ref, w3buf)])
    a = a_ref[...]
    o_ref[...] = (_silu(_dot(a, w1buf[...])) * _dot(a, w3buf[...])).astype(o_ref.dtype)


def _swiglu_up(a, w1, w3, layer):
    m, d = a.shape
    f = w1.shape[2]
    tm = _pick(m, (1024, 512, 256, 128, 64, 32, 16, 8))
    tn = _pick(f, (512, 256, 128))
    return pl.pallas_call(
        _swiglu_kernel, grid=(f // tn, m // tm),
        in_specs=[pl.BlockSpec((tm, d), lambda j, i: (i, 0)),
                  _w_spec(d, tn, layer, 0), _w_spec(d, tn, layer, 0)],
        out_specs=pl.BlockSpec((tm, tn), lambda j, i: (i, j)),
        out_shape=jax.ShapeDtypeStruct((m, f), BF16),
        scratch_shapes=[pltpu.VMEM((d, tn), BF16), pltpu.VMEM((d, tn), BF16)],
        compiler_params=_cparams("parallel", "arbitrary"), name="swiglu_up",
    )(a, w1, w3)


def _up_merge_kernel(ha_ref, hb_ref, wa_ref, wb_ref, sga_ref, sgb_ref, o_ref, wabuf, wbbuf):
    _stash_weights([(wa_ref, wabuf), (wb_ref, wbbuf)])
    up_a = _dot(ha_ref[...], wabuf[...])
    up_b = _dot(hb_ref[...], wbbuf[...])
    o_ref[...] = (sga_ref[...] * up_a + sgb_ref[...] * up_b).astype(o_ref.dtype)


def _up_merge(ha, hb, wa, wb, layer, sg):
    m, va = ha.shape
    vb = hb.shape[1]
    d = wa.shape[2]
    tm = _pick(m, (512, 256, 128, 64, 32, 16, 8))
    tn = _pick(d, (1024, 512, 256, 128))
    nj = d // tn
    return pl.pallas_call(
        _up_merge_kernel, grid=(nj, m // tm),
        in_specs=[pl.BlockSpec((tm, va), lambda j, i: (i, 0)),
                  pl.BlockSpec((tm, vb), lambda j, i: (i, 0)),
                  _w_spec(va, tn, layer, 0), _w_spec(vb, tn, layer, 0),
                  pl.BlockSpec((tm, tn), lambda j, i: (i, j)),
                  pl.BlockSpec((tm, tn), lambda j, i: (i, j + nj))],
        out_specs=pl.BlockSpec((tm, tn), lambda j, i: (i, j)),
        out_shape=jax.ShapeDtypeStruct((m, d), BF16),
        scratch_shapes=[pltpu.VMEM((va, tn), BF16), pltpu.VMEM((vb, tn), BF16)],
        compiler_params=_cparams("parallel", "arbitrary"), name="up_merge",
    )(ha, hb, wa, wb, sg, sg)


def _conv_taps(xs, w, dy, not_first, not_last):
    n = xs.shape[0]
    left = jnp.where(not_first, pltpu.roll(xs, 1, 0), 0.0)
    right = jnp.where(not_last, pltpu.roll(xs, n - 1, 0), 0.0)
    return w[3 * dy:3 * dy + 1] * left + w[3 * dy + 1:3 * dy + 2] * xs + w[3 * dy + 2:3 * dy + 3] * right


def _conv_kernel(up_ref, x_ref, dn_ref, w_ref, s_ref, o_ref, pad_ref, *, rows, rb, strip):
    r0 = pl.program_id(0) * rb
    w = w_ref[...]

    @pl.when(r0 < rows.n_lat)
    def _():
        tc = x_ref.shape[-1]
        at_start = lax.rem(r0, rows.seq) == 0
        at_end = lax.rem(r0 + rb, rows.seq) == 0
        pad_ref[pl.ds(0, GRID_W), :] = jnp.where(at_start, 0.0, up_ref[...])
        pad_ref[pl.ds(GRID_W + rb, GRID_W), :] = jnp.where(at_end, 0.0, dn_ref[...])
        pad_ref[pl.ds(GRID_W, rb), :] = x_ref[...]
        col = lax.broadcasted_iota(I32, (strip, tc), 0) % GRID_W
        not_first = col != 0
        not_last = col != GRID_W - 1
        for s in range(rb // strip):
            acc = jnp.zeros((strip, tc), F32)
            for dy in range(3):
                xs = pad_ref[pl.ds(s * strip + dy * GRID_W, strip), :]
                acc = acc + _conv_taps(xs, w, dy, not_first, not_last)
            o_ref[pl.ds(s * strip, strip), :] = _silu(acc) * s_ref[...]

    @pl.when(r0 >= rows.n_lat)
    def _():
        x = x_ref[...]
        pos = lax.broadcasted_iota(I32, x.shape, 0) % rows.ctx_len
        acc = _conv_taps(x, w, 1, pos != 0, pos != rows.ctx_len - 1)
        o_ref[...] = _silu(acc) * s_ref[...]


def _conv_silu(rows, p_qk, conv_w, scale):
    m, c = p_qk.shape
    tc = _pick(c, (512, 256, 128))
    rb = rows.n_ctx
    strip = _pick(rb, (512, 256, 128, 64))
    per = rb // GRID_W
    last = m // GRID_W - 1
    return pl.pallas_call(
        functools.partial(_conv_kernel, rows=rows, rb=rb, strip=strip),
        grid=(m // rb, c // tc),
        in_specs=[pl.BlockSpec((GRID_W, tc), lambda i, j: (jnp.maximum(i * per - 1, 0), j)),
                  pl.BlockSpec((rb, tc), lambda i, j: (i, j)),
                  pl.BlockSpec((GRID_W, tc), lambda i, j: (jnp.minimum((i + 1) * per, last), j)),
                  pl.BlockSpec((9, tc), lambda i, j: (0, j)),
                  pl.BlockSpec((1, tc), lambda i, j: (0, j))],
        out_specs=pl.BlockSpec((rb, tc), lambda i, j: (i, j)),
        out_shape=jax.ShapeDtypeStruct((m, c), F32),
        scratch_shapes=[pltpu.VMEM((rb + 2 * GRID_W, tc), F32)],
        compiler_params=_cparams("parallel", "parallel"), name="conv_silu",
    )(p_qk, p_qk, p_qk, conv_w.reshape(9, c), scale)


def _causal_masks(n):
    t = lax.broadcasted_iota(I32, (n, n), 0)
    s = lax.broadcasted_iota(I32, (n, n), 1)
    return t, s, (s <= t, s >= t)


def _block_ref_rows(x, blk, row):
    n, c = x.shape
    if blk >= SUBLANES:
        x3 = x.reshape(n // blk, blk, c)
        return jnp.broadcast_to(x3[:, row:row + 1, :], x3.shape).reshape(n, c)
    x3 = x.reshape(n // SUBLANES, SUBLANES, c)
    sub = lax.broadcasted_iota(I32, x3.shape, 1)
    out = jnp.zeros_like(x3)
    for g in range(SUBLANES // blk):
        r = g * blk + row
        out = jnp.where(sub // blk == g, jnp.broadcast_to(x3[:, r:r + 1, :], x3.shape), out)
    return out.reshape(n, c)


def _gla_direction(d, q, k, v, lr, wg, bg, s_ref, t_idx, s_idx, causal):
    n = q.shape[0]
    lh, lm, _ = _split3(lr)
    wh, wm, _ = _split3(wg)
    z = _dot(jnp.concatenate([lh, lh, lm], axis=1), jnp.concatenate([wh, wm, wh], axis=0)) + bg
    la = jax.nn.log_sigmoid(z) * (1.0 / GLA_GATE_NORMALIZER)
    tri = causal.astype(BF16)
    c3 = _dot(tri, jnp.concatenate(_split3(la), axis=1))
    nk = la.shape[1]
    cum = c3[:, :nk] + c3[:, nk:2 * nk] + c3[:, 2 * nk:]
    total = cum[n - 1:n] if d == 0 else cum[0:1]
    state = s_ref[...]
    out = _dot((q * jnp.exp(cum)).astype(BF16), state.astype(BF16))
    lev = jnp.where(causal, 31 - lax.clz(t_idx ^ s_idx), -2).astype(BF16)
    qb = q.astype(BF16)
    kb = k.astype(BF16)
    scores = jnp.where(lev == -1, _dot_nt(qb, kb).astype(BF16), jnp.zeros((), BF16))
    for l in range(n.bit_length() - 1):
        half = 1 << l
        ref = _block_ref_rows(cum, 2 * half, half - 1 if d == 0 else half)
        e = jnp.exp(-jnp.abs(cum - ref)).astype(BF16)
        p = _dot_nt(qb * e, kb * e).astype(BF16)
        scores = jnp.where(lev == l, p, scores)
    out = out + _dot(scores, v)
    k_out = (k * jnp.exp(total - cum)).astype(BF16)
    dk = state.shape[0]
    et = jnp.broadcast_to(jnp.exp(total), (dk, dk)).T
    scale = jnp.concatenate([et] * (state.shape[1] // dk), axis=1)
    s_ref[...] = scale * state + _dot_tn(k_out, v)
    return out


def _gla_kernel(qf, kf, vf, smf, qb, kb, vb, smb, wg_ref, bg_ref, of_ref, ob_ref, sf_ref, sb_ref):
    @pl.when(pl.program_id(1) == 0)
    def _():
        sf_ref[...] = jnp.zeros_like(sf_ref)
        sb_ref[...] = jnp.zeros_like(sb_ref)

    n = qf.shape[0]
    t_idx, s_idx, causal = _causal_masks(n)
    dirs = ((qf, kf, vf, smf, of_ref, sf_ref), (qb, kb, vb, smb, ob_ref, sb_ref))
    for d, (q, k, v, sm, o_ref, s_ref) in enumerate(dirs):
        lr = sm[:, d * GLA_RANK:(d + 1) * GLA_RANK]
        for hd in range(GLA_HEADS):
            ks = slice(hd * GLA_DK, (hd + 1) * GLA_DK)
            vs = slice(hd * GLA_DV, (hd + 1) * GLA_DV)
            o_ref[:, vs] = _gla_direction(d, q[:, ks], k[:, ks], v[:, vs], lr, wg_ref[d, :, ks], bg_ref[d, :, ks],
                                          s_ref.at[hd], t_idx, s_idx, causal[d])


def _soft_cap(z):
    return GATE_SOFT_CAP * jnp.tanh(z * (1.0 / GATE_SOFT_CAP))


def _mlstm_direction(d, head, q, k, v, g_col, g_row, c_ref, m_ref, causal):
    n = q.shape[0]
    tri = causal.astype(BF16)
    i_idx = 2 * MLSTM_HEADS * d + head
    f_idx = i_idx + MLSTM_HEADS
    sel_rows = lax.broadcasted_iota(I32, (3 * N_GATE_B, LANES), 0) % N_GATE_B
    rep = lambda a, idx: _dot(jnp.concatenate(_split3(a), axis=1), (sel_rows == idx).astype(BF16))
    i_rep = rep(g_col, i_idx)
    f_rep = rep(jax.nn.log_sigmoid(g_col), f_idx)
    c3 = _dot(tri, jnp.concatenate(_split3(f_rep), axis=1))
    cum = c3[:, :LANES] + c3[:, LANES:2 * LANES] + c3[:, 2 * LANES:]
    r3 = _dot_nt(jnp.concatenate(_split3(jax.nn.log_sigmoid(g_row)), axis=0), tri)
    cum_r_all = r3[:N_GATE_B] + r3[N_GATE_B:2 * N_GATE_B] + r3[2 * N_GATE_B:]
    sub = lax.broadcasted_iota(I32, (N_GATE_B, 1), 0)
    pick_r = lambda a, idx: jnp.sum(jnp.where(sub == idx, a, 0.0), axis=0, keepdims=True)
    i_r, cum_r = pick_r(g_row, i_idx), pick_r(cum_r_all, f_idx)
    wide = lambda a, reps: jnp.concatenate([a] * reps, axis=1)
    nt = n // LANES
    total = cum[n - 1:n] if d == 0 else cum[0:1]
    m_prev = m_ref[...]
    dlog = jnp.where(causal, wide(cum, nt) - cum_r + i_r, NEG_BIG)
    inter_log = cum + m_prev
    row_max = dlog[:, :LANES]
    for t in range(1, nt):
        row_max = jnp.maximum(row_max, dlog[:, t * LANES:(t + 1) * LANES])
    m_t = jnp.maximum(inter_log, jnp.max(row_max, axis=1, keepdims=True))
    w_inter = jnp.exp(inter_log - m_t)
    qb16 = q.astype(BF16)
    s = _dot_nt(qb16, k.astype(BF16)) * jnp.exp(dlog - wide(m_t, nt))
    dv = v.shape[1]
    v_ext = jnp.concatenate([v, jnp.ones((n, LANES), BF16)], axis=1)
    state = c_ref[...]
    acc = wide(w_inter, dv // LANES + 1) * _dot(qb16, state.astype(BF16)) + _dot(s.astype(BF16), v_ext)
    bound = jnp.maximum(jnp.abs(acc[:, dv:]), jnp.exp(-m_t))
    out = acc[:, :dv] / wide(bound, dv // LANES)
    g = total - cum + i_rep
    m_new = jnp.maximum(total + m_prev, jnp.max(g, axis=0, keepdims=True))
    w_c = jnp.exp(total + m_prev - m_new)
    w_k = jnp.exp(g - m_new)
    c_ref[...] = wide(w_c, dv // LANES + 1) * state + _dot_tn((k * w_k).astype(BF16), v_ext)
    m_ref[...] = m_new
    return out


def _mlstm_kernel(qf, kf, vf, gcf, grf, qb, kb, vb, gcb, grb, brow_ref, bcol_ref,
                  of_ref, ob_ref, cf_ref, cb_ref, mf_ref, mb_ref):
    @pl.when(pl.program_id(1) == 0)
    def _():
        cf_ref[...] = jnp.zeros_like(cf_ref)
        cb_ref[...] = jnp.zeros_like(cb_ref)
        mf_ref[...] = jnp.zeros_like(mf_ref)
        mb_ref[...] = jnp.zeros_like(mb_ref)

    n = qf.shape[0]
    _, _, causal = _causal_masks(n)
    g0 = 2 * GLA_RANK
    dirs = ((qf, kf, vf, gcf, grf, of_ref, cf_ref, mf_ref), (qb, kb, vb, gcb, grb, ob_ref, cb_ref, mb_ref))
    for d, (q, k, v, gc, gr, o_ref, c_ref, m_ref) in enumerate(dirs):
        g_col = _soft_cap(gc[:, g0:g0 + N_GATE_B] + brow_ref[...])
        g_row = _soft_cap(gr[...] + bcol_ref[...])
        for hd in range(MLSTM_HEADS):
            qs = slice(hd * MLSTM_DQK, (hd + 1) * MLSTM_DQK)
            vs = slice(hd * MLSTM_DV, (hd + 1) * MLSTM_DV)
            o_ref[:, vs] = _mlstm_direction(d, hd, q[:, qs], k[:, qs], v[:, vs], g_col, g_row,
                                            c_ref.at[hd], m_ref.at[hd], causal[d])


def _scan_row_maps(rows, chunk):
    lat_chunks = rows.seq // chunk
    ctx_chunks = rows.ctx_len // chunk
    ctx0 = rows.n_lat // chunk

    def fwd(b, s):
        return jnp.where(s < ctx_chunks, ctx0 + b * ctx_chunks + s, b * lat_chunks + (s - ctx_chunks))

    def bwd(b, s):
        return jnp.where(s < ctx_chunks, ctx0 + b * ctx_chunks + (ctx_chunks - 1 - s),
                         b * lat_chunks + (lat_chunks - 1 - (s - ctx_chunks)))

    return fwd, bwd, ctx_chunks + lat_chunks


def _gla_scan(rows, qk, v, small, gate_w, gate_b):
    m = qk.shape[0]
    n = SCAN_CHUNK
    fwd, bwd, steps = _scan_row_maps(rows, n)

    def specs(rmap):
        return [pl.BlockSpec((n, QA), lambda b, s: (rmap(b, s), 0)),
                pl.BlockSpec((n, QA), lambda b, s: (rmap(b, s), 1)),
                pl.BlockSpec((n, V_A), lambda b, s: (rmap(b, s), 0)),
                pl.BlockSpec((n, LANES), lambda b, s: (rmap(b, s), 0))]

    out_spec = lambda rmap: pl.BlockSpec((n, V_A), lambda b, s: (rmap(b, s), 0))
    out_sds = jax.ShapeDtypeStruct((m, V_A), F32)
    state = pltpu.VMEM((GLA_HEADS, GLA_DK, GLA_DV), F32)
    return pl.pallas_call(
        _gla_kernel,
        grid=(rows.batch, steps),
        in_specs=specs(fwd) + specs(bwd) + [
            pl.BlockSpec((2, GLA_RANK, QA), lambda b, s: (0, 0, 0)),
            pl.BlockSpec((2, 1, QA), lambda b, s: (0, 0, 0))],
        out_specs=[out_spec(fwd), out_spec(bwd)],
        out_shape=[out_sds, out_sds],
        scratch_shapes=[state, state],
        compiler_params=_cparams("parallel", "arbitrary"), name="gla_scan",
    )(qk, qk, v, small, qk, qk, v, small, gate_w, gate_b.reshape(2, 1, QA))


def _mlstm_scan(rows, qk, v, small, gates_t, gate_b):
    m = qk.shape[0]
    n = SCAN_CHUNK
    fwd, bwd, steps = _scan_row_maps(rows, n)
    q0 = 2 * QA // QB
    v0 = V_A // V_B

    def specs(rmap):
        return [pl.BlockSpec((n, QB), lambda b, s: (rmap(b, s), q0)),
                pl.BlockSpec((n, QB), lambda b, s: (rmap(b, s), q0 + 1)),
                pl.BlockSpec((n, V_B), lambda b, s: (rmap(b, s), v0)),
                pl.BlockSpec((n, LANES), lambda b, s: (rmap(b, s), 0)),
                pl.BlockSpec((N_GATE_B, n), lambda b, s: (0, rmap(b, s)))]

    out_spec = lambda rmap: pl.BlockSpec((n, V_B), lambda b, s: (rmap(b, s), 0))
    out_sds = jax.ShapeDtypeStruct((m, V_B), F32)
    state = pltpu.VMEM((MLSTM_HEADS, MLSTM_DQK, MLSTM_DV + LANES), F32)
    stab = pltpu.VMEM((MLSTM_HEADS, 1, LANES), F32)
    return pl.pallas_call(
        _mlstm_kernel,
        grid=(rows.batch, steps),
        in_specs=specs(fwd) + specs(bwd) + [
            pl.BlockSpec((1, N_GATE_B), lambda b, s: (0, 0)),
            pl.BlockSpec((N_GATE_B, 1), lambda b, s: (0, 0))],
        out_specs=[out_spec(fwd), out_spec(bwd)],
        out_shape=[out_sds, out_sds],
        scratch_shapes=[state, state, stab, stab],
        compiler_params=_cparams("parallel", "arbitrary"), name="mlstm_scan",
    )(qk, qk, v, small, gates_t, qk, qk, v, small, gates_t,
      gate_b.reshape(1, N_GATE_B), gate_b.reshape(N_GATE_B, 1))


def _mix_prep_kernel(oaf, oab, obf, obb, ro_ref, gna_ref, gnb_ref, ha_ref, hb_ref):
    branches = ((oaf, oab, gna_ref, ha_ref, 0, GLA_HEADS, GLA_DV, _silu),
                (obf, obb, gnb_ref, hb_ref, V_A, MLSTM_HEADS, MLSTM_DV, jax.nn.sigmoid))
    for of, ob, gn_ref, h_ref, off, heads, dv, gate_fn in branches:
        for h in range(heads):
            sl = slice(h * dv, (h + 1) * dv)
            o = of[:, sl] + ob[:, sl]
            y = o * lax.rsqrt(jnp.mean(o * o, axis=-1, keepdims=True) + EPS) * gn_ref[:, sl]
            gate = gate_fn(ro_ref[:, off + h * dv:off + (h + 1) * dv])
            h_ref[:, sl] = (y * gate).astype(h_ref.dtype)


def _mix_prep(oaf, oab, obf, obb, ro, gna, gnb):
    m = ro.shape[0]
    tm = _pick(m, (256, 128, 64, 32, 16, 8))
    row = lambda c: pl.BlockSpec((tm, c), lambda i: (i, 0))
    vec = lambda c: pl.BlockSpec((1, c), lambda i: (0, 0))
    return pl.pallas_call(
        _mix_prep_kernel, grid=(m // tm,),
        in_specs=[row(V_A), row(V_A), row(V_B), row(V_B), row(V_A + V_B), vec(V_A), vec(V_B)],
        out_specs=[row(V_A), row(V_B)],
        out_shape=[jax.ShapeDtypeStruct((m, V_A), BF16), jax.ShapeDtypeStruct((m, V_B), BF16)],
        compiler_params=_cparams("parallel"), name="mix_prep",
    )(oaf, oab, obf, obb, ro, gna.reshape(1, V_A), gnb.reshape(1, V_B))


def _expert_up_kernel(te_ref, na_ref, src0_ref, srcn_ref, v_hbm, w1_ref, w3_ref, o_ref, xbuf, abuf, sem,
                      *, tm, issue_steps):
    del te_ref
    i = pl.program_id(0)
    j = pl.program_id(1)
    slot = lax.rem(i, 2)
    per = tm // issue_steps

    def row_copy(src_ref, r, s):
        return pltpu.make_async_copy(v_hbm.at[pl.ds(src_ref[0, r], 1)], xbuf.at[s, pl.ds(r, 1)], sem.at[s])

    @pl.when(jnp.logical_and(i == 0, j == 0))
    def _():
        def start(r, carry):
            row_copy(src0_ref, r, 0).start()
            return carry

        lax.fori_loop(0, tm, start, 0)

    @pl.when(j == 0)
    def _():
        def wait(r, carry):
            row_copy(src0_ref, 0, slot).wait()
            return carry

        lax.fori_loop(0, tm, wait, 0, unroll=32)
        abuf[...] = xbuf[slot].astype(BF16)

    @pl.when(jnp.logical_and(i + 1 < pl.num_programs(0), j < issue_steps))
    def _():
        for t in range(per):
            row_copy(srcn_ref, j * per + t, 1 - slot).start()

    @pl.when(i < na_ref[0])
    def _():
        a = abuf[...]
        o_ref[...] = (_silu(_dot(a, w1_ref[...])) * _dot(a, w3_ref[...])).astype(o_ref.dtype)

    @pl.when(i >= na_ref[0])
    def _():
        o_ref[...] = jnp.zeros_like(o_ref)


def _expert_up(v, src, w1, w3, layer, tile_expert, n_active, tm):
    n_tiles = src.shape[0]
    d = v.shape[1]
    f = w1.shape[3]
    tn = _pick(f, (1408, 512, 256, 128))
    nj = f // tn
    issue_steps = max(s for s in (1, 2, 4, 8) if s <= nj)
    wmap = lambda i, j, te, na: (layer, te[i], 0, jnp.where(i < na[0], j, 0))
    return pl.pallas_call(
        functools.partial(_expert_up_kernel, tm=tm, issue_steps=issue_steps),
        grid_spec=pltpu.PrefetchScalarGridSpec(
            num_scalar_prefetch=2, grid=(n_tiles, nj),
            in_specs=[pl.BlockSpec((None, 1, tm), lambda i, j, te, na: (0, 0, 0), memory_space=pltpu.SMEM),
                      pl.BlockSpec((None, 1, tm), lambda i, j, te, na: (jnp.minimum(i + 1, n_tiles - 1), 0, 0),
                                   memory_space=pltpu.SMEM),
                      pl.BlockSpec(memory_space=pl.ANY),
                      pl.BlockSpec((None, None, d, tn), wmap),
                      pl.BlockSpec((None, None, d, tn), wmap)],
            out_specs=pl.BlockSpec((tm, tn), lambda i, j, te, na: (i, j)),
            scratch_shapes=[pltpu.VMEM((2, tm, d), F32), pltpu.VMEM((tm, d), BF16),
                            pltpu.SemaphoreType.DMA((2,))]),
        out_shape=jax.ShapeDtypeStruct((n_tiles * tm, f), BF16),
        compiler_params=_cparams("arbitrary", "arbitrary"), name="expert_up",
    )(tile_expert, n_active, src, src, v, w1, w3)


def _expert_down_kernel(te_ref, na_ref, a_ref, w_ref, o_ref):
    del te_ref
    i = pl.program_id(0)

    @pl.when(i < na_ref[0])
    def _():
        o_ref[...] = _dot(a_ref[...], w_ref[...])

    @pl.when(i >= na_ref[0])
    def _():
        o_ref[...] = jnp.zeros_like(o_ref)


def _expert_down(hs, w2, layer, tile_expert, n_active, tm):
    p, f = hs.shape
    d = w2.shape[3]
    tn = _pick(d, (1024, 512, 256, 128))
    return pl.pallas_call(
        _expert_down_kernel,
        grid_spec=pltpu.PrefetchScalarGridSpec(
            num_scalar_prefetch=2, grid=(p // tm, d // tn),
            in_specs=[pl.BlockSpec((tm, f), lambda i, j, te, na: (jnp.where(i < na[0], i, 0), 0)),
                      pl.BlockSpec((None, None, f, tn),
                                   lambda i, j, te, na: (layer, te[i], 0, jnp.where(i < na[0], j, 0)))],
            out_specs=pl.BlockSpec((tm, tn), lambda i, j, te, na: (i, j))),
        out_shape=jax.ShapeDtypeStruct((p, d), F32),
        compiler_params=_cparams("parallel", "parallel"), name="expert_down",
    )(tile_expert, n_active, hs, w2)


def _moe_combine_kernel(pos_ref, posn_ref, ys_hbm, x_ref, g_ref, w_ref, o_ref, buf, sem, *, tile):
    i = pl.program_id(0)
    slot = lax.rem(i, 2)

    def row_copy(p_ref, k, r, s):
        return pltpu.make_async_copy(ys_hbm.at[pl.ds(p_ref[0, k * tile + r], 1)],
                                     buf.at[s, k, pl.ds(r, 1)], sem.at[s])

    def start_tile(p_ref, s):
        for k in range(TOP_K):
            def start(r, carry):
                row_copy(p_ref, k, r, s).start()
                return carry

            lax.fori_loop(0, tile, start, 0, unroll=8)

    @pl.when(i == 0)
    def _():
        start_tile(pos_ref, 0)

    @pl.when(i + 1 < pl.num_programs(0))
    def _():
        start_tile(posn_ref, 1 - slot)

    def wait(r, carry):
        row_copy(pos_ref, 0, 0, slot).wait()
        return carry

    lax.fori_loop(0, TOP_K * tile, wait, 0, unroll=32)
    w = w_ref[...]
    y = w[:, 0:1] * buf[slot, 0] + w[:, 1:2] * buf[slot, 1]
    o_ref[...] = x_ref[...] + g_ref[...] * y


def _moe_combine(rows, ys, pos, x, mods, which, ew, m):
    d = x.shape[1]
    tile = rows.row_tile((256, 128, 64, 32, 16, 8))
    n = m // tile
    pos_spec = lambda imap: pl.BlockSpec((None, 1, TOP_K * tile), imap, memory_space=pltpu.SMEM)
    return pl.pallas_call(
        functools.partial(_moe_combine_kernel, tile=tile),
        grid=(n,),
        in_specs=[pos_spec(lambda i: (i, 0, 0)),
                  pos_spec(lambda i: (jnp.minimum(i + 1, n - 1), 0, 0)),
                  pl.BlockSpec(memory_space=pl.ANY),
                  pl.BlockSpec((tile, d), lambda i: (i, 0)),
                  _mod_spec(rows, which, tile, d),
                  pl.BlockSpec((tile, LANES), lambda i: (i, 0))],
        out_specs=pl.BlockSpec((tile, d), lambda i: (i, 0)),
        out_shape=jax.ShapeDtypeStruct((m, d), F32),
        scratch_shapes=[pltpu.VMEM((2, TOP_K, tile, d), F32), pltpu.SemaphoreType.DMA((2,))],
        compiler_params=_cparams("arbitrary"), name="moe_combine",
    )(pos, pos, ys, x, mods, ew)


def _moe_dispatch(e_idx, n_experts, tm, combine_tile):
    m = e_idx.shape[0]
    ex = jnp.concatenate([e_idx[:, 0], e_idx[:, 1]])
    onehot = (ex[:, None] == jnp.arange(n_experts, dtype=I32)[None, :]).astype(I32)
    rank = jnp.sum((jnp.cumsum(onehot, axis=0) - onehot) * onehot, axis=1)
    counts = jnp.sum(onehot, axis=0)
    padded = (counts + tm - 1) // tm * tm
    ends = jnp.cumsum(padded)
    dest = (ends - padded)[ex] + rank
    n_tiles = (TOP_K * m + n_experts * (tm - 1)) // tm
    tok = jnp.concatenate([jnp.arange(m, dtype=I32)] * TOP_K)
    src = jnp.zeros((n_tiles * tm,), I32).at[dest].set(tok).reshape(n_tiles, 1, tm)
    tile_start = jnp.arange(n_tiles, dtype=I32) * tm
    tile_expert = jnp.minimum(jnp.sum((ends[None, :] <= tile_start[:, None]).astype(I32), axis=1), n_experts - 1)
    n_active = (ends[-1] // tm).astype(I32).reshape(1)
    pos = dest.reshape(TOP_K, m // combine_tile, combine_tile).transpose(1, 0, 2).reshape(
        m // combine_tile, 1, TOP_K * combine_tile).astype(I32)
    return src, tile_expert, n_active, pos


def _moe_ffn(rows, x, gn, mods, router_w, w1, w3, w2, layer, n_tok):
    n_experts = w1.shape[1]
    v, e_idx, e_w = _norm_mod_call(rows, x, gn, mods, 3, 4, F32, router_w=router_w)
    tm = 512 if TOP_K * n_tok >= 8192 else 64
    combine_tile = rows.row_tile((256, 128, 64, 32, 16, 8))
    src, tile_expert, n_active, pos = _moe_dispatch(e_idx[:n_tok], n_experts, tm, combine_tile)
    hs = _expert_up(v, src, w1, w3, layer, tile_expert, n_active, tm)
    ys = _expert_down(hs, w2, layer, tile_expert, n_active, tm)
    return _moe_combine(rows, ys, pos, x, mods, 5, e_w, n_tok)


def kernel(x, c, ctx, c_ctx, ada_w, ada_b, norm_mix, norm_ffn, w_in, conv_w, gla_gate_w, gla_gate_b,
           mlstm_gate_b, gla_out_norm, mlstm_out_norm, w_up_a, w_up_b, w_o, ffn_w1, ffn_w3, ffn_w2,
           router_w, moe_w1, moe_w3, moe_w2, norm_final):
    batch, seq, d = x.shape
    ctx_len = ctx.shape[1]
    depth = ada_w.shape[0]
    assert seq % GRID_W == 0 and seq % SCAN_CHUNK == 0 and ctx_len % SCAN_CHUNK == 0
    assert seq % (batch * ctx_len) == 0 and (batch * ctx_len) % GRID_W == 0
    rows = _Rows(batch, seq, ctx_len)

    h = jnp.concatenate([x.reshape(batch * seq, d), ctx.reshape(batch * ctx_len, d)], axis=0)

    cvec = jnp.zeros((SUBLANES * ((batch + 1 + SUBLANES - 1) // SUBLANES), d), F32)
    cvec = cvec.at[:batch].set(c).at[batch].set(c_ctx)
    mods_all = _ada_mods(cvec, ada_w, ada_b)
    mods_all = mods_all[:, :batch + 1].reshape(depth, batch + 1, 6, 1, d).transpose(0, 2, 1, 3, 4)

    qk_scale = jnp.ones((QK_COLS,), F32)
    qk_scale = qk_scale.at[:QA].set(GLA_DK ** -0.5).at[2 * QA + QB:].set(MLSTM_DQK ** -0.5).reshape(1, QK_COLS)
    c_v, c_ro = QK_COLS, QK_COLS + V_A + V_B
    c_sm = c_ro + V_A + V_B
    c_g = c_sm + 2 * GLA_RANK + N_GATE_B
    n_small = c_g - c_sm
    moe_w1_bf, moe_w3_bf, moe_w2_bf = moe_w1.astype(BF16), moe_w3.astype(BF16), moe_w2.astype(BF16)

    for l in range(depth):
        mods = mods_all[l]
        w_sm = jnp.zeros((1, d, LANES), F32).at[0, :, :n_small].set(w_in[l, :, c_sm:c_g])
        w_g = w_in[l, :, c_g:][None]

        u = _norm_mod_call(rows, h, norm_mix[l], mods, 0, 1, BF16)
        p_qk = _matmul(u, w_in, l, 0, QK_COLS, F32)
        v = _matmul(u, w_in, l, c_v, V_A + V_B, BF16)
        ro = _matmul(u, w_in, l, c_ro, V_A + V_B, F32)
        small = _matmul(u, w_sm, 0, 0, LANES, F32)
        sg = _matmul(u, w_g, 0, 0, 2 * d, BF16, act="sigmoid")
        qk = _conv_silu(rows, p_qk, conv_w[l], qk_scale)
        gates_t = small[:, 2 * GLA_RANK:n_small].T
        oaf, oab = _gla_scan(rows, qk, v, small, gla_gate_w[l], gla_gate_b[l])
        obf, obb = _mlstm_scan(rows, qk, v, small, gates_t, mlstm_gate_b[l])
        ha, hb = _mix_prep(oaf, oab, obf, obb, ro, gla_out_norm[l], mlstm_out_norm[l])
        merged = _up_merge(ha, hb, w_up_a, w_up_b, l, sg)
        h = _matmul(merged, w_o, l, 0, d, F32, res=h, rows=rows, mods=mods, which=2)

        if l % 2 == 0:
            e = l // 2
            vv = _norm_mod_call(rows, h, norm_ffn[l], mods, 3, 4, BF16)
            hid = _swiglu_up(vv, ffn_w1, ffn_w3, e)
            h = _matmul(hid, ffn_w2, e, 0, d, F32, res=h, rows=rows, mods=mods, which=5)
        else:
            e = l // 2
            n_tok = rows.n_lat if l == depth - 1 else rows.m
            h = _moe_ffn(rows, h, norm_ffn[l], mods, router_w[e], moe_w1_bf, moe_w3_bf, moe_w2_bf, e, n_tok)

    out = _final_norm(rows, h, norm_final)
    return out.reshape(batch, seq, d)
```

```python
import functools
import math

import jax
import jax.numpy as jnp
from jax import lax
from jax.experimental import pallas as pl
from jax.experimental.pallas import tpu as pltpu

F32 = jnp.float32
BF16 = jnp.bfloat16
I32 = jnp.int32

GRID_W = 64
GLA_HEADS = 4
GLA_DK = 128
GLA_DV = 256
GLA_RANK = 16
GLA_GATE_NORMALIZER = 16.0
MLSTM_HEADS = 4
MLSTM_DQK = 128
MLSTM_DV = 256
GATE_SOFT_CAP = 15.0
N_GATE_B = 4 * MLSTM_HEADS
TOP_K = 2
EPS = 1e-6
QA = GLA_HEADS * GLA_DK
QB = MLSTM_HEADS * MLSTM_DQK
QK_COLS = 2 * QA + 2 * QB
V_A = GLA_HEADS * GLA_DV
V_B = MLSTM_HEADS * MLSTM_DV

LANES = 128
SUBLANES = 8
SCAN_CHUNK = 256
VMEM_LIMIT = 52 * 1024 * 1024
NEG_BIG = -1e30
LOG2_E = 1.4426950408889634
HIGHEST = lax.Precision.HIGHEST


def _cparams(*sem):
    return pltpu.CompilerParams(dimension_semantics=sem, vmem_limit_bytes=VMEM_LIMIT)


def _pick(dim, prefs):
    for p in prefs:
        if dim % p == 0:
            return p
    return dim


def _dot(a, b, precision=None):
    return jnp.dot(a, b, preferred_element_type=F32, precision=precision)


def _dot_nt(a, b, precision=None):
    return lax.dot_general(a, b, (((1,), (1,)), ((), ())), preferred_element_type=F32, precision=precision)


def _dot_tn(a, b):
    return lax.dot_general(a, b, (((0,), (0,)), ((), ())), preferred_element_type=F32)


def _silu(x):
    return x * jax.nn.sigmoid(x)


def _split3(x):
    hi = x.astype(BF16)
    rest = x - hi.astype(F32)
    mid = rest.astype(BF16)
    lo = (rest - mid.astype(F32)).astype(BF16)
    return hi, mid, lo


def _ada_kernel(c_ref, w_ref, b_ref, o_ref):
    o_ref[...] = _dot(_silu(c_ref[...]), w_ref[...], HIGHEST) + b_ref[...]


def _ada_mods(cvec, ada_w, ada_b):
    depth, d, n = ada_w.shape
    tn = _pick(n, (1024, 512, 256, 128))
    return pl.pallas_call(
        _ada_kernel,
        grid=(depth, n // tn),
        in_specs=[pl.BlockSpec(cvec.shape, lambda l, j: (0, 0)),
                  pl.BlockSpec((None, d, tn), lambda l, j: (l, 0, j)),
                  pl.BlockSpec((None, 1, tn), lambda l, j: (l, 0, j))],
        out_specs=pl.BlockSpec((None, cvec.shape[0], tn), lambda l, j: (l, 0, j)),
        out_shape=jax.ShapeDtypeStruct((depth, cvec.shape[0], n), F32),
        compiler_params=_cparams("parallel", "parallel"),
        name="ada_mods",
    )(cvec, ada_w, ada_b.reshape(depth, 1, n))


class _Rows:
    def __init__(self, batch, seq, ctx_len):
        self.batch, self.seq, self.ctx_len = batch, seq, ctx_len
        self.n_lat = batch * seq
        self.n_ctx = batch * ctx_len
        self.m = self.n_lat + self.n_ctx

    def row_tile(self, prefs):
        return _pick(math.gcd(self.seq, self.n_ctx), prefs)

    def gid(self, i, tm):
        r0 = i * tm
        return jnp.where(r0 < self.n_lat, r0 // self.seq, self.batch)


def _mod_spec(rows, which, tm, d):
    return pl.BlockSpec((None, None, 1, d), lambda i: (which, rows.gid(i, tm), 0, 0))


def _norm_mod(x, gn, sh, sc):
    y = x * lax.rsqrt(jnp.mean(x * x, axis=-1, keepdims=True) + EPS) * gn
    return y * (1.0 + sc) + sh


def _norm_mod_kernel(x_ref, gn_ref, sh_ref, sc_ref, o_ref):
    o_ref[...] = _norm_mod(x_ref[...], gn_ref[...], sh_ref[...], sc_ref[...]).astype(o_ref.dtype)


def _norm_mod_router_kernel(x_ref, gn_ref, sh_ref, sc_ref, rw_ref, o_ref, ei_ref, ew_ref, *, n_experts):
    u = _norm_mod(x_ref[...], gn_ref[...], sh_ref[...], sc_ref[...])
    o_ref[...] = u
    logits = _dot(u, rw_ref[...], HIGHEST)
    lane_i = lax.broadcasted_iota(I32, logits.shape, 1)
    lane = lane_i.astype(F32)
    lg = jnp.where(lane_i < n_experts, logits, -jnp.inf)
    m1 = jnp.max(lg, axis=-1, keepdims=True)
    i1 = jnp.min(jnp.where(lg == m1, lane, float(LANES)), axis=-1, keepdims=True)
    lg2 = jnp.where(lane == i1, -jnp.inf, lg)
    m2 = jnp.max(lg2, axis=-1, keepdims=True)
    i2 = jnp.min(jnp.where(lg2 == m2, lane, float(LANES)), axis=-1, keepdims=True)
    e = jnp.exp(m2 - m1)
    w1 = 1.0 / (1.0 + e)
    w2 = e / (1.0 + e)
    ei_ref[...] = jnp.where(lane_i == 0, i1, jnp.where(lane_i == 1, i2, 0.0)).astype(I32)
    ew_ref[...] = jnp.where(lane_i == 0, w1, jnp.where(lane_i == 1, w2, 0.0))


def _norm_mod_call(rows, x, gn, mods, which_sh, which_sc, out_dtype, router_w=None):
    m, d = x.shape
    tm = rows.row_tile((256, 128, 64, 32, 16, 8))
    in_specs = [pl.BlockSpec((tm, d), lambda i: (i, 0)),
                pl.BlockSpec((1, d), lambda i: (0, 0)),
                _mod_spec(rows, which_sh, tm, d),
                _mod_spec(rows, which_sc, tm, d)]
    row_spec = pl.BlockSpec((tm, d), lambda i: (i, 0))
    if router_w is None:
        return pl.pallas_call(
            _norm_mod_kernel, grid=(m // tm,), in_specs=in_specs, out_specs=row_spec,
            out_shape=jax.ShapeDtypeStruct((m, d), out_dtype),
            compiler_params=_cparams("parallel"), name="norm_mod",
        )(x, gn.reshape(1, d), mods, mods)
    n_experts = router_w.shape[1]
    rw = jnp.zeros((d, LANES), F32).at[:, :n_experts].set(router_w)
    lane_spec = pl.BlockSpec((tm, LANES), lambda i: (i, 0))
    return pl.pallas_call(
        functools.partial(_norm_mod_router_kernel, n_experts=n_experts),
        grid=(m // tm,),
        in_specs=in_specs + [pl.BlockSpec((d, LANES), lambda i: (0, 0))],
        out_specs=[row_spec, lane_spec, lane_spec],
        out_shape=[jax.ShapeDtypeStruct((m, d), F32), jax.ShapeDtypeStruct((m, LANES), I32),
                   jax.ShapeDtypeStruct((m, LANES), F32)],
        compiler_params=_cparams("parallel"), name="norm_mod_router",
    )(x, gn.reshape(1, d), mods, mods, rw)


def _final_norm_kernel(x_ref, gn_ref, o_ref):
    x = x_ref[...]
    o_ref[...] = x * lax.rsqrt(jnp.mean(x * x, axis=-1, keepdims=True) + EPS) * gn_ref[...]


def _final_norm(rows, x, gn):
    d = x.shape[1]
    tm = rows.row_tile((256, 128, 64, 32, 16, 8))
    return pl.pallas_call(
        _final_norm_kernel, grid=(rows.n_lat // tm,),
        in_specs=[pl.BlockSpec((tm, d), lambda i: (i, 0)), pl.BlockSpec((1, d), lambda i: (0, 0))],
        out_specs=pl.BlockSpec((tm, d), lambda i: (i, 0)),
        out_shape=jax.ShapeDtypeStruct((rows.n_lat, d), F32),
        compiler_params=_cparams("parallel"), name="final_norm",
    )(x, gn.reshape(1, d))


def _stash_weights(pairs):
    @pl.when(pl.program_id(1) == 0)
    def _():
        for w_ref, wbuf in pairs:
            wbuf[...] = w_ref[...].astype(BF16)


def _w_spec(kdim, tn, layer, col_blk0):
    return pl.BlockSpec((None, kdim, tn), lambda j, i: (layer, 0, col_blk0 + j))


def _mm_kernel(*refs, act, residual):
    if residual:
        a_ref, w_ref, x_ref, g_ref, o_ref, wbuf = refs
    else:
        a_ref, w_ref, o_ref, wbuf = refs
    _stash_weights([(w_ref, wbuf)])
    r = _dot(a_ref[...], wbuf[...])
    if act == "sigmoid":
        r = jax.nn.sigmoid(r)
    if residual:
        r = x_ref[...] + g_ref[...] * r
    o_ref[...] = r.astype(o_ref.dtype)


def _matmul(a, w, layer, col0, n, out_dtype, *, act=None, res=None, rows=None, mods=None, which=None):
    m, kdim = a.shape
    tn = _pick(n, (1024, 512, 256, 128) if kdim <= 2048 else (512, 256, 128))
    assert col0 % tn == 0
    row_prefs = (1024, 512, 256, 128, 64, 32, 16, 8) if kdim <= 2048 else (512, 256, 128, 64, 32, 16, 8)
    tm = _pick(m, row_prefs) if rows is None else rows.row_tile(row_prefs)
    residual = res is not None
    in_specs = [pl.BlockSpec((tm, kdim), lambda j, i: (i, 0)), _w_spec(kdim, tn, layer, col0 // tn)]
    args = [a, w]
    if residual:
        in_specs += [pl.BlockSpec((tm, tn), lambda j, i: (i, j)),
                     pl.BlockSpec((None, None, 1, tn), lambda j, i: (which, rows.gid(i, tm), 0, j))]
        args += [res, mods]
    return pl.pallas_call(
        functools.partial(_mm_kernel, act=act, residual=residual),
        grid=(n // tn, m // tm),
        in_specs=in_specs,
        out_specs=pl.BlockSpec((tm, tn), lambda j, i: (i, j)),
        out_shape=jax.ShapeDtypeStruct((m, n), out_dtype),
        scratch_shapes=[pltpu.VMEM((kdim, tn), BF16)],
        compiler_params=_cparams("parallel", "arbitrary"),
        name="matmul",
    )(*args)


def _swiglu_kernel(a_ref, w1_ref, w3_ref, o_ref, w1buf, w3buf):
    _stash_weights([(w1_ref, w1buf), (w3_ref, w3buf)])
    a = a_ref[...]
    o_ref[...] = (_silu(_dot(a, w1buf[...])) * _dot(a, w3buf[...])).astype(o_ref.dtype)


def _swiglu_up(a, w1, w3, layer):
    m, d = a.shape
    f = w1.shape[2]
    tm = _pick(m, (1024, 512, 256, 128, 64, 32, 16, 8))
    tn = _pick(f, (512, 256, 128))
    return pl.pallas_call(
        _swiglu_kernel, grid=(f // tn, m // tm),
        in_specs=[pl.BlockSpec((tm, d), lambda j, i: (i, 0)),
                  _w_spec(d, tn, layer, 0), _w_spec(d, tn, layer, 0)],
        out_specs=pl.BlockSpec((tm, tn), lambda j, i: (i, j)),
        out_shape=jax.ShapeDtypeStruct((m, f), BF16),
        scratch_shapes=[pltpu.VMEM((d, tn), BF16), pltpu.VMEM((d, tn), BF16)],
        compiler_params=_cparams("parallel", "arbitrary"), name="swiglu_up",
    )(a, w1, w3)


def _up_merge_kernel(ha_ref, hb_ref, wa_ref, wb_ref, sga_ref, sgb_ref, o_ref, wabuf, wbbuf):
    _stash_weights([(wa_ref, wabuf), (wb_ref, wbbuf)])
    up_a = _dot(ha_ref[...], wabuf[...])
    up_b = _dot(hb_ref[...], wbbuf[...])
    o_ref[...] = (sga_ref[...] * up_a + sgb_ref[...] * up_b).astype(o_ref.dtype)


def _up_merge(ha, hb, wa, wb, layer, sg):
    m, va = ha.shape
    vb = hb.shape[1]
    d = wa.shape[2]
    tm = _pick(m, (512, 256, 128, 64, 32, 16, 8))
    tn = _pick(d, (1024, 512, 256, 128))
    nj = d // tn
    return pl.pallas_call(
        _up_merge_kernel, grid=(nj, m // tm),
        in_specs=[pl.BlockSpec((tm, va), lambda j, i: (i, 0)),
                  pl.BlockSpec((tm, vb), lambda j, i: (i, 0)),
                  _w_spec(va, tn, layer, 0), _w_spec(vb, tn, layer, 0),
                  pl.BlockSpec((tm, tn), lambda j, i: (i, j)),
                  pl.BlockSpec((tm, tn), lambda j, i: (i, j + nj))],
        out_specs=pl.BlockSpec((tm, tn), lambda j, i: (i, j)),
        out_shape=jax.ShapeDtypeStruct((m, d), BF16),
        scratch_shapes=[pltpu.VMEM((va, tn), BF16), pltpu.VMEM((vb, tn), BF16)],
        compiler_params=_cparams("parallel", "arbitrary"), name="up_merge",
    )(ha, hb, wa, wb, sg, sg)


def _conv_taps(xs, w, dy, not_first, not_last):
    n = xs.shape[0]
    left = jnp.where(not_first, pltpu.roll(xs, 1, 0), 0.0)
    right = jnp.where(not_last, pltpu.roll(xs, n - 1, 0), 0.0)
    return w[3 * dy:3 * dy + 1] * left + w[3 * dy + 1:3 * dy + 2] * xs + w[3 * dy + 2:3 * dy + 3] * right


def _conv_kernel(up_ref, x_ref, dn_ref, w_ref, s_ref, o_ref, pad_ref, *, rows, rb, strip):
    r0 = pl.program_id(0) * rb
    w = w_ref[...]

    @pl.when(r0 < rows.n_lat)
    def _():
        tc = x_ref.shape[-1]
        at_start = lax.rem(r0, rows.seq) == 0
        at_end = lax.rem(r0 + rb, rows.seq) == 0
        pad_ref[pl.ds(0, GRID_W), :] = jnp.where(at_start, 0.0, up_ref[...])
        pad_ref[pl.ds(GRID_W + rb, GRID_W), :] = jnp.where(at_end, 0.0, dn_ref[...])
        pad_ref[pl.ds(GRID_W, rb), :] = x_ref[...]
        col = lax.broadcasted_iota(I32, (strip, tc), 0) % GRID_W
        not_first = col != 0
        not_last = col != GRID_W - 1
        for s in range(rb // strip):
            acc = jnp.zeros((strip, tc), F32)
            for dy in range(3):
                xs = pad_ref[pl.ds(s * strip + dy * GRID_W, strip), :]
                acc = acc + _conv_taps(xs, w, dy, not_first, not_last)
            o_ref[pl.ds(s * strip, strip), :] = _silu(acc) * s_ref[...]

    @pl.when(r0 >= rows.n_lat)
    def _():
        x = x_ref[...]
        pos = lax.broadcasted_iota(I32, x.shape, 0) % rows.ctx_len
        acc = _conv_taps(x, w, 1, pos != 0, pos != rows.ctx_len - 1)
        o_ref[...] = _silu(acc) * s_ref[...]


def _conv_silu(rows, p_qk, conv_w, scale):
    m, c = p_qk.shape
    tc = _pick(c, (512, 256, 128))
    rb = rows.n_ctx
    strip = _pick(rb, (512, 256, 128, 64))
    per = rb // GRID_W
    last = m // GRID_W - 1
    return pl.pallas_call(
        functools.partial(_conv_kernel, rows=rows, rb=rb, strip=strip),
        grid=(m // rb, c // tc),
        in_specs=[pl.BlockSpec((GRID_W, tc), lambda i, j: (jnp.maximum(i * per - 1, 0), j)),
                  pl.BlockSpec((rb, tc), lambda i, j: (i, j)),
                  pl.BlockSpec((GRID_W, tc), lambda i, j: (jnp.minimum((i + 1) * per, last), j)),
                  pl.BlockSpec((9, tc), lambda i, j: (0, j)),
                  pl.BlockSpec((1, tc), lambda i, j: (0, j))],
        out_specs=pl.BlockSpec((rb, tc), lambda i, j: (i, j)),
        out_shape=jax.ShapeDtypeStruct((m, c), F32),
        scratch_shapes=[pltpu.VMEM((rb + 2 * GRID_W, tc), F32)],
        compiler_params=_cparams("parallel", "parallel"), name="conv_silu",
    )(p_qk, p_qk, p_qk, conv_w.reshape(9, c), scale)


def _causal_masks(n):
    t = lax.broadcasted_iota(I32, (n, n), 0)
    s = lax.broadcasted_iota(I32, (n, n), 1)
    return t, s, (s <= t, s >= t)


def _block_ref_rows(x, blk, row):
    n, c = x.shape
    if blk >= SUBLANES:
        x3 = x.reshape(n // blk, blk, c)
        return jnp.broadcast_to(x3[:, row:row + 1, :], x3.shape).reshape(n, c)
    x3 = x.reshape(n // SUBLANES, SUBLANES, c)
    sub = lax.broadcasted_iota(I32, x3.shape, 1)
    out = jnp.zeros_like(x3)
    for g in range(SUBLANES // blk):
        r = g * blk + row
        out = jnp.where(sub // blk == g, jnp.broadcast_to(x3[:, r:r + 1, :], x3.shape), out)
    return out.reshape(n, c)


def _gla_direction(d, q, k, v, lr, wg, bg, s_ref, causal, lev_blk):
    n = q.shape[0]
    hn = n // 2
    lh, lm, _ = _split3(lr)
    wh, wm, _ = _split3(wg)
    z = _dot(jnp.concatenate([lh, lh, lm], axis=1), jnp.concatenate([wh, wm, wh], axis=0)) + bg
    la = jax.nn.log_sigmoid(z) * (LOG2_E / GLA_GATE_NORMALIZER)
    tri = causal.astype(BF16)
    c3 = _dot(tri, jnp.concatenate(_split3(la), axis=1))
    nk = la.shape[1]
    cum = c3[:, :nk] + c3[:, nk:2 * nk] + c3[:, 2 * nk:]
    total = cum[n - 1:n] if d == 0 else cum[0:1]
    state = s_ref[...]
    out = _dot((q * jnp.exp2(cum)).astype(BF16), state.astype(BF16))
    qb = q.astype(BF16)
    kb = k.astype(BF16)
    lo, hi = slice(0, hn), slice(hn, n)
    diag = [jnp.where(lev_blk == -1, _dot_nt(qb[h], kb[h]).astype(BF16), jnp.zeros((), BF16)) for h in (lo, hi)]
    top = n.bit_length() - 2
    for l in range(top + 1):
        half = 1 << l
        ref = _block_ref_rows(cum, 2 * half, half - 1 if d == 0 else half)
        e = jnp.exp2(-jnp.abs(cum - ref)).astype(BF16)
        qe = qb * e
        ke = kb * e
        if l < top:
            diag = [jnp.where(lev_blk == l, _dot_nt(qe[h], ke[h]).astype(BF16), sc)
                    for h, sc in zip((lo, hi), diag)]
        else:
            cross = (_dot_nt(qe[hi], ke[lo]) if d == 0 else _dot_nt(qe[lo], ke[hi])).astype(BF16)
    zero = jnp.zeros((hn, hn), BF16)
    upper, lower = ([diag[0], zero], [cross, diag[1]]) if d == 0 else ([diag[0], cross], [zero, diag[1]])
    scores = jnp.concatenate([jnp.concatenate(upper, axis=1), jnp.concatenate(lower, axis=1)], axis=0)
    out = out + _dot(scores, v)
    k_out = (k * jnp.exp2(total - cum)).astype(BF16)
    dk = state.shape[0]
    et = jnp.broadcast_to(jnp.exp2(total), (dk, dk)).T
    scale = jnp.concatenate([et] * (state.shape[1] // dk), axis=1)
    s_ref[...] = scale * state + _dot_tn(k_out, v)
    return out


def _gla_kernel(qf, kf, vf, smf, qb, kb, vb, smb, wg_ref, bg_ref, of_ref, ob_ref, sf_ref, sb_ref):
    @pl.when(pl.program_id(1) == 0)
    def _():
        sf_ref[...] = jnp.zeros_like(sf_ref)
        sb_ref[...] = jnp.zeros_like(sb_ref)

    n = qf.shape[0]
    _, _, causal = _causal_masks(n)
    t_idx, s_idx, causal_blk = _causal_masks(n // 2)
    lev = 31 - lax.clz(t_idx ^ s_idx)
    dirs = ((qf, kf, vf, smf, of_ref, sf_ref), (qb, kb, vb, smb, ob_ref, sb_ref))
    for d, (q, k, v, sm, o_ref, s_ref) in enumerate(dirs):
        lr = sm[:, d * GLA_RANK:(d + 1) * GLA_RANK]
        lev_blk = jnp.where(causal_blk[d], lev, -2).astype(BF16)
        for hd in range(GLA_HEADS):
            ks = slice(hd * GLA_DK, (hd + 1) * GLA_DK)
            vs = slice(hd * GLA_DV, (hd + 1) * GLA_DV)
            o_ref[:, vs] = _gla_direction(d, q[:, ks], k[:, ks], v[:, vs], lr, wg_ref[d, :, ks], bg_ref[d, :, ks],
                                          s_ref.at[hd], causal[d], lev_blk)


def _soft_cap(z):
    return GATE_SOFT_CAP * jnp.tanh(z * (1.0 / GATE_SOFT_CAP))


def _mlstm_direction(d, head, q, k, v, g_col, g_row, c_ref, m_ref, causal):
    n = q.shape[0]
    tri = causal.astype(BF16)
    i_idx = 2 * MLSTM_HEADS * d + head
    f_idx = i_idx + MLSTM_HEADS
    sel_rows = lax.broadcasted_iota(I32, (3 * N_GATE_B, LANES), 0) % N_GATE_B
    rep = lambda a, idx: _dot(jnp.concatenate(_split3(a), axis=1), (sel_rows == idx).astype(BF16))
    i_rep = rep(g_col, i_idx)
    f_rep = rep(jax.nn.log_sigmoid(g_col), f_idx)
    c3 = _dot(tri, jnp.concatenate(_split3(f_rep), axis=1))
    cum = c3[:, :LANES] + c3[:, LANES:2 * LANES] + c3[:, 2 * LANES:]
    r3 = _dot_nt(jnp.concatenate(_split3(jax.nn.log_sigmoid(g_row)), axis=0), tri)
    cum_r_all = r3[:N_GATE_B] + r3[N_GATE_B:2 * N_GATE_B] + r3[2 * N_GATE_B:]
    sub = lax.broadcasted_iota(I32, (N_GATE_B, 1), 0)
    pick_r = lambda a, idx: jnp.sum(jnp.where(sub == idx, a, 0.0), axis=0, keepdims=True)
    i_r, cum_r = pick_r(g_row, i_idx), pick_r(cum_r_all, f_idx)
    wide = lambda a, reps: jnp.concatenate([a] * reps, axis=1)
    nt = n // LANES
    total = cum[n - 1:n] if d == 0 else cum[0:1]
    m_prev = m_ref[...]
    dlog = jnp.where(causal, wide(cum, nt) - cum_r + i_r, NEG_BIG)
    inter_log = cum + m_prev
    row_max = dlog[:, :LANES]
    for t in range(1, nt):
        row_max = jnp.maximum(row_max, dlog[:, t * LANES:(t + 1) * LANES])
    m_t = jnp.maximum(inter_log, jnp.max(row_max, axis=1, keepdims=True))
    w_inter = jnp.exp(inter_log - m_t)
    qb16 = q.astype(BF16)
    s = _dot_nt(qb16, k.astype(BF16)) * jnp.exp(dlog - wide(m_t, nt))
    dv = v.shape[1]
    v_ext = jnp.concatenate([v, jnp.ones((n, LANES), BF16)], axis=1)
    state = c_ref[...]
    acc = wide(w_inter, dv // LANES + 1) * _dot(qb16, state.astype(BF16)) + _dot(s.astype(BF16), v_ext)
    bound = jnp.maximum(jnp.abs(acc[:, dv:]), jnp.exp(-m_t))
    out = acc[:, :dv] / wide(bound, dv // LANES)
    g = total - cum + i_rep
    m_new = jnp.maximum(total + m_prev, jnp.max(g, axis=0, keepdims=True))
    w_c = jnp.exp(total + m_prev - m_new)
    w_k = jnp.exp(g - m_new)
    c_ref[...] = wide(w_c, dv // LANES + 1) * state + _dot_tn((k * w_k).astype(BF16), v_ext)
    m_ref[...] = m_new
    return out


def _mlstm_kernel(qf, kf, vf, gcf, grf, qb, kb, vb, gcb, grb, brow_ref, bcol_ref,
                  of_ref, ob_ref, cf_ref, cb_ref, mf_ref, mb_ref):
    @pl.when(pl.program_id(1) == 0)
    def _():
        cf_ref[...] = jnp.zeros_like(cf_ref)
        cb_ref[...] = jnp.zeros_like(cb_ref)
        mf_ref[...] = jnp.zeros_like(mf_ref)
        mb_ref[...] = jnp.zeros_like(mb_ref)

    n = qf.shape[0]
    _, _, causal = _causal_masks(n)
    g0 = 2 * GLA_RANK
    dirs = ((qf, kf, vf, gcf, grf, of_ref, cf_ref, mf_ref), (qb, kb, vb, gcb, grb, ob_ref, cb_ref, mb_ref))
    for d, (q, k, v, gc, gr, o_ref, c_ref, m_ref) in enumerate(dirs):
        g_col = _soft_cap(gc[:, g0:g0 + N_GATE_B] + brow_ref[...])
        g_row = _soft_cap(gr[...] + bcol_ref[...])
        for hd in range(MLSTM_HEADS):
            qs = slice(hd * MLSTM_DQK, (hd + 1) * MLSTM_DQK)
            vs = slice(hd * MLSTM_DV, (hd + 1) * MLSTM_DV)
            o_ref[:, vs] = _mlstm_direction(d, hd, q[:, qs], k[:, qs], v[:, vs], g_col, g_row,
                                            c_ref.at[hd], m_ref.at[hd], causal[d])


def _scan_row_maps(rows, chunk):
    lat_chunks = rows.seq // chunk
    ctx_chunks = rows.ctx_len // chunk
    ctx0 = rows.n_lat // chunk

    def fwd(b, s):
        return jnp.where(s < ctx_chunks, ctx0 + b * ctx_chunks + s, b * lat_chunks + (s - ctx_chunks))

    def bwd(b, s):
        return jnp.where(s < ctx_chunks, ctx0 + b * ctx_chunks + (ctx_chunks - 1 - s),
                         b * lat_chunks + (lat_chunks - 1 - (s - ctx_chunks)))

    return fwd, bwd, ctx_chunks + lat_chunks


def _gla_scan(rows, qk, v, small, gate_w, gate_b):
    m = qk.shape[0]
    n = SCAN_CHUNK
    fwd, bwd, steps = _scan_row_maps(rows, n)

    def specs(rmap):
        return [pl.BlockSpec((n, QA), lambda b, s: (rmap(b, s), 0)),
                pl.BlockSpec((n, QA), lambda b, s: (rmap(b, s), 1)),
                pl.BlockSpec((n, V_A), lambda b, s: (rmap(b, s), 0)),
                pl.BlockSpec((n, LANES), lambda b, s: (rmap(b, s), 0))]

    out_spec = lambda rmap: pl.BlockSpec((n, V_A), lambda b, s: (rmap(b, s), 0))
    out_sds = jax.ShapeDtypeStruct((m, V_A), F32)
    state = pltpu.VMEM((GLA_HEADS, GLA_DK, GLA_DV), F32)
    return pl.pallas_call(
        _gla_kernel,
        grid=(rows.batch, steps),
        in_specs=specs(fwd) + specs(bwd) + [
            pl.BlockSpec((2, GLA_RANK, QA), lambda b, s: (0, 0, 0)),
            pl.BlockSpec((2, 1, QA), lambda b, s: (0, 0, 0))],
        out_specs=[out_spec(fwd), out_spec(bwd)],
        out_shape=[out_sds, out_sds],
        scratch_shapes=[state, state],
        compiler_params=_cparams("parallel", "arbitrary"), name="gla_scan",
    )(qk, qk, v, small, qk, qk, v, small, gate_w, gate_b.reshape(2, 1, QA))


def _mlstm_scan(rows, qk, v, small, gates_t, gate_b):
    m = qk.shape[0]
    n = SCAN_CHUNK
    fwd, bwd, steps = _scan_row_maps(rows, n)
    q0 = 2 * QA // QB
    v0 = V_A // V_B

    def specs(rmap):
        return [pl.BlockSpec((n, QB), lambda b, s: (rmap(b, s), q0)),
                pl.BlockSpec((n, QB), lambda b, s: (rmap(b, s), q0 + 1)),
                pl.BlockSpec((n, V_B), lambda b, s: (rmap(b, s), v0)),
                pl.BlockSpec((n, LANES), lambda b, s: (rmap(b, s), 0)),
                pl.BlockSpec((N_GATE_B, n), lambda b, s: (0, rmap(b, s)))]

    out_spec = lambda rmap: pl.BlockSpec((n, V_B), lambda b, s: (rmap(b, s), 0))
    out_sds = jax.ShapeDtypeStruct((m, V_B), F32)
    state = pltpu.VMEM((MLSTM_HEADS, MLSTM_DQK, MLSTM_DV + LANES), F32)
    stab = pltpu.VMEM((MLSTM_HEADS, 1, LANES), F32)
    return pl.pallas_call(
        _mlstm_kernel,
        grid=(rows.batch, steps),
        in_specs=specs(fwd) + specs(bwd) + [
            pl.BlockSpec((1, N_GATE_B), lambda b, s: (0, 0)),
            pl.BlockSpec((N_GATE_B, 1), lambda b, s: (0, 0))],
        out_specs=[out_spec(fwd), out_spec(bwd)],
        out_shape=[out_sds, out_sds],
        scratch_shapes=[state, state, stab, stab],
        compiler_params=_cparams("parallel", "arbitrary"), name="mlstm_scan",
    )(qk, qk, v, small, gates_t, qk, qk, v, small, gates_t,
      gate_b.reshape(1, N_GATE_B), gate_b.reshape(N_GATE_B, 1))


def _mix_prep_kernel(oaf, oab, obf, obb, ro_ref, gna_ref, gnb_ref, ha_ref, hb_ref):
    branches = ((oaf, oab, gna_ref, ha_ref, 0, GLA_HEADS, GLA_DV, _silu),
                (obf, obb, gnb_ref, hb_ref, V_A, MLSTM_HEADS, MLSTM_DV, jax.nn.sigmoid))
    for of, ob, gn_ref, h_ref, off, heads, dv, gate_fn in branches:
        for h in range(heads):
            sl = slice(h * dv, (h + 1) * dv)
            o = of[:, sl] + ob[:, sl]
            y = o * lax.rsqrt(jnp.mean(o * o, axis=-1, keepdims=True) + EPS) * gn_ref[:, sl]
            gate = gate_fn(ro_ref[:, off + h * dv:off + (h + 1) * dv])
            h_ref[:, sl] = (y * gate).astype(h_ref.dtype)


def _mix_prep(oaf, oab, obf, obb, ro, gna, gnb):
    m = ro.shape[0]
    tm = _pick(m, (256, 128, 64, 32, 16, 8))
    row = lambda c: pl.BlockSpec((tm, c), lambda i: (i, 0))
    vec = lambda c: pl.BlockSpec((1, c), lambda i: (0, 0))
    return pl.pallas_call(
        _mix_prep_kernel, grid=(m // tm,),
        in_specs=[row(V_A), row(V_A), row(V_B), row(V_B), row(V_A + V_B), vec(V_A), vec(V_B)],
        out_specs=[row(V_A), row(V_B)],
        out_shape=[jax.ShapeDtypeStruct((m, V_A), BF16), jax.ShapeDtypeStruct((m, V_B), BF16)],
        compiler_params=_cparams("parallel"), name="mix_prep",
    )(oaf, oab, obf, obb, ro, gna.reshape(1, V_A), gnb.reshape(1, V_B))


def _expert_up_kernel(te_ref, na_ref, src0_ref, srcn_ref, v_hbm, w1_ref, w3_ref, o_ref, xbuf, abuf, sem,
                      *, tm, issue_steps):
    del te_ref
    i = pl.program_id(0)
    j = pl.program_id(1)
    slot = lax.rem(i, 2)
    per = tm // issue_steps

    def row_copy(src_ref, r, s):
        return pltpu.make_async_copy(v_hbm.at[pl.ds(src_ref[0, r], 1)], xbuf.at[s, pl.ds(r, 1)], sem.at[s])

    @pl.when(jnp.logical_and(i == 0, j == 0))
    def _():
        def start(r, carry):
            row_copy(src0_ref, r, 0).start()
            return carry

        lax.fori_loop(0, tm, start, 0)

    @pl.when(j == 0)
    def _():
        def wait(r, carry):
            row_copy(src0_ref, 0, slot).wait()
            return carry

        lax.fori_loop(0, tm, wait, 0, unroll=32)
        abuf[...] = xbuf[slot].astype(BF16)

    @pl.when(jnp.logical_and(i + 1 < pl.num_programs(0), j < issue_steps))
    def _():
        for t in range(per):
            row_copy(srcn_ref, j * per + t, 1 - slot).start()

    @pl.when(i < na_ref[0])
    def _():
        a = abuf[...]
        o_ref[...] = (_silu(_dot(a, w1_ref[...])) * _dot(a, w3_ref[...])).astype(o_ref.dtype)

    @pl.when(i >= na_ref[0])
    def _():
        o_ref[...] = jnp.zeros_like(o_ref)


def _expert_up(v, src, w1, w3, layer, tile_expert, n_active, tm):
    n_tiles = src.shape[0]
    d = v.shape[1]
    f = w1.shape[3]
    tn = _pick(f, (1408, 512, 256, 128))
    nj = f // tn
    issue_steps = max(s for s in (1, 2, 4, 8) if s <= nj)
    wmap = lambda i, j, te, na: (layer, te[i], 0, jnp.where(i < na[0], j, 0))
    return pl.pallas_call(
        functools.partial(_expert_up_kernel, tm=tm, issue_steps=issue_steps),
        grid_spec=pltpu.PrefetchScalarGridSpec(
            num_scalar_prefetch=2, grid=(n_tiles, nj),
            in_specs=[pl.BlockSpec((None, 1, tm), lambda i, j, te, na: (0, 0, 0), memory_space=pltpu.SMEM),
                      pl.BlockSpec((None, 1, tm), lambda i, j, te, na: (jnp.minimum(i + 1, n_tiles - 1), 0, 0),
                                   memory_space=pltpu.SMEM),
                      pl.BlockSpec(memory_space=pl.ANY),
                      pl.BlockSpec((None, None, d, tn), wmap),
                      pl.BlockSpec((None, None, d, tn), wmap)],
            out_specs=pl.BlockSpec((tm, tn), lambda i, j, te, na: (i, j)),
            scratch_shapes=[pltpu.VMEM((2, tm, d), F32), pltpu.VMEM((tm, d), BF16),
                            pltpu.SemaphoreType.DMA((2,))]),
        out_shape=jax.ShapeDtypeStruct((n_tiles * tm, f), BF16),
        compiler_params=_cparams("arbitrary", "arbitrary"), name="expert_up",
    )(tile_expert, n_active, src, src, v, w1, w3)


def _expert_down_kernel(te_ref, na_ref, a_ref, w_ref, o_ref):
    del te_ref
    i = pl.program_id(0)

    @pl.when(i < na_ref[0])
    def _():
        o_ref[...] = _dot(a_ref[...], w_ref[...])

    @pl.when(i >= na_ref[0])
    def _():
        o_ref[...] = jnp.zeros_like(o_ref)


def _expert_down(hs, w2, layer, tile_expert, n_active, tm):
    p, f = hs.shape
    d = w2.shape[3]
    tn = _pick(d, (1024, 512, 256, 128))
    return pl.pallas_call(
        _expert_down_kernel,
        grid_spec=pltpu.PrefetchScalarGridSpec(
            num_scalar_prefetch=2, grid=(p // tm, d // tn),
            in_specs=[pl.BlockSpec((tm, f), lambda i, j, te, na: (jnp.where(i < na[0], i, 0), 0)),
                      pl.BlockSpec((None, None, f, tn),
                                   lambda i, j, te, na: (layer, te[i], 0, jnp.where(i < na[0], j, 0)))],
            out_specs=pl.BlockSpec((tm, tn), lambda i, j, te, na: (i, j))),
        out_shape=jax.ShapeDtypeStruct((p, d), F32),
        compiler_params=_cparams("parallel", "parallel"), name="expert_down",
    )(tile_expert, n_active, hs, w2)


def _moe_combine_kernel(pos_ref, posn_ref, ys_hbm, x_ref, g_ref, w_ref, fn_ref, o_ref, buf, sem,
                        *, tile, final_norm):
    i = pl.program_id(0)
    slot = lax.rem(i, 2)

    def row_copy(p_ref, k, r, s):
        return pltpu.make_async_copy(ys_hbm.at[pl.ds(p_ref[0, k * tile + r], 1)],
                                     buf.at[s, k, pl.ds(r, 1)], sem.at[s])

    def start_tile(p_ref, s):
        for k in range(TOP_K):
            def start(r, carry):
                row_copy(p_ref, k, r, s).start()
                return carry

            lax.fori_loop(0, tile, start, 0, unroll=8)

    @pl.when(i == 0)
    def _():
        start_tile(pos_ref, 0)

    @pl.when(i + 1 < pl.num_programs(0))
    def _():
        start_tile(posn_ref, 1 - slot)

    def wait(r, carry):
        row_copy(pos_ref, 0, 0, slot).wait()
        return carry

    lax.fori_loop(0, TOP_K * tile, wait, 0, unroll=32)
    w = w_ref[...]
    y = w[:, 0:1] * buf[slot, 0] + w[:, 1:2] * buf[slot, 1]
    r = x_ref[...] + g_ref[...] * y
    if final_norm:
        r = r * lax.rsqrt(jnp.mean(r * r, axis=-1, keepdims=True) + EPS) * fn_ref[...]
    o_ref[...] = r


def _moe_combine(rows, ys, pos, x, mods, which, ew, m, final_gn):
    d = x.shape[1]
    tile = rows.row_tile((256, 128, 64, 32, 16, 8))
    n = m // tile
    pos_spec = lambda imap: pl.BlockSpec((None, 1, TOP_K * tile), imap, memory_space=pltpu.SMEM)
    final_norm = final_gn is not None
    fn = final_gn.reshape(1, d) if final_norm else jnp.ones((1, d), F32)
    return pl.pallas_call(
        functools.partial(_moe_combine_kernel, tile=tile, final_norm=final_norm),
        grid=(n,),
        in_specs=[pos_spec(lambda i: (i, 0, 0)),
                  pos_spec(lambda i: (jnp.minimum(i + 1, n - 1), 0, 0)),
                  pl.BlockSpec(memory_space=pl.ANY),
                  pl.BlockSpec((tile, d), lambda i: (i, 0)),
                  _mod_spec(rows, which, tile, d),
                  pl.BlockSpec((tile, LANES), lambda i: (i, 0)),
                  pl.BlockSpec((1, d), lambda i: (0, 0))],
        out_specs=pl.BlockSpec((tile, d), lambda i: (i, 0)),
        out_shape=jax.ShapeDtypeStruct((m, d), F32),
        scratch_shapes=[pltpu.VMEM((2, TOP_K, tile, d), F32), pltpu.SemaphoreType.DMA((2,))],
        compiler_params=_cparams("arbitrary"), name="moe_combine",
    )(pos, pos, ys, x, mods, ew, fn)


def _moe_dispatch(e_idx, n_experts, tm, combine_tile):
    m = e_idx.shape[0]
    ex = jnp.concatenate([e_idx[:, 0], e_idx[:, 1]])
    onehot = (ex[:, None] == jnp.arange(n_experts, dtype=I32)[None, :]).astype(I32)
    rank = jnp.sum((jnp.cumsum(onehot, axis=0) - onehot) * onehot, axis=1)
    counts = jnp.sum(onehot, axis=0)
    padded = (counts + tm - 1) // tm * tm
    ends = jnp.cumsum(padded)
    dest = (ends - padded)[ex] + rank
    n_tiles = (TOP_K * m + n_experts * (tm - 1)) // tm
    tok = jnp.concatenate([jnp.arange(m, dtype=I32)] * TOP_K)
    src = jnp.zeros((n_tiles * tm,), I32).at[dest].set(tok).reshape(n_tiles, 1, tm)
    tile_start = jnp.arange(n_tiles, dtype=I32) * tm
    tile_expert = jnp.minimum(jnp.sum((ends[None, :] <= tile_start[:, None]).astype(I32), axis=1), n_experts - 1)
    n_active = (ends[-1] // tm).astype(I32).reshape(1)
    pos = dest.reshape(TOP_K, m // combine_tile, combine_tile).transpose(1, 0, 2).reshape(
        m // combine_tile, 1, TOP_K * combine_tile).astype(I32)
    return src, tile_expert, n_active, pos


def _moe_ffn(rows, x, gn, mods, router_w, w1, w3, w2, layer, n_tok, final_gn):
    n_experts = w1.shape[1]
    v, e_idx, e_w = _norm_mod_call(rows, x, gn, mods, 3, 4, F32, router_w=router_w)
    tm = 512 if TOP_K * n_tok >= 8192 else 64
    combine_tile = rows.row_tile((256, 128, 64, 32, 16, 8))
    src, tile_expert, n_active, pos = _moe_dispatch(e_idx[:n_tok], n_experts, tm, combine_tile)
    hs = _expert_up(v, src, w1, w3, layer, tile_expert, n_active, tm)
    ys = _expert_down(hs, w2, layer, tile_expert, n_active, tm)
    return _moe_combine(rows, ys, pos, x, mods, 5, e_w, n_tok, final_gn)


def kernel(x, c, ctx, c_ctx, ada_w, ada_b, norm_mix, norm_ffn, w_in, conv_w, gla_gate_w, gla_gate_b,
           mlstm_gate_b, gla_out_norm, mlstm_out_norm, w_up_a, w_up_b, w_o, ffn_w1, ffn_w3, ffn_w2,
           router_w, moe_w1, moe_w3, moe_w2, norm_final):
    batch, seq, d = x.shape
    ctx_len = ctx.shape[1]
    depth = ada_w.shape[0]
    assert seq % GRID_W == 0 and seq % SCAN_CHUNK == 0 and ctx_len % SCAN_CHUNK == 0
    assert seq % (batch * ctx_len) == 0 and (batch * ctx_len) % GRID_W == 0
    rows = _Rows(batch, seq, ctx_len)

    h = jnp.concatenate([x.reshape(batch * seq, d), ctx.reshape(batch * ctx_len, d)], axis=0)

    cvec = jnp.zeros((SUBLANES * ((batch + 1 + SUBLANES - 1) // SUBLANES), d), F32)
    cvec = cvec.at[:batch].set(c).at[batch].set(c_ctx)
    mods_all = _ada_mods(cvec, ada_w, ada_b)
    mods_all = mods_all[:, :batch + 1].reshape(depth, batch + 1, 6, 1, d).transpose(0, 2, 1, 3, 4)

    qk_scale = jnp.ones((QK_COLS,), F32)
    qk_scale = qk_scale.at[:QA].set(GLA_DK ** -0.5).at[2 * QA + QB:].set(MLSTM_DQK ** -0.5).reshape(1, QK_COLS)
    c_v, c_ro = QK_COLS, QK_COLS + V_A + V_B
    c_sm = c_ro + V_A + V_B
    c_g = c_sm + 2 * GLA_RANK + N_GATE_B
    n_small = c_g - c_sm
    moe_w1_bf, moe_w3_bf, moe_w2_bf = moe_w1.astype(BF16), moe_w3.astype(BF16), moe_w2.astype(BF16)
    w_main = w_in[:, :, :c_sm].astype(BF16)

    for l in range(depth):
        mods = mods_all[l]
        w_sm = jnp.zeros((1, d, LANES), F32).at[0, :, :n_small].set(w_in[l, :, c_sm:c_g])
        w_g = w_in[l, :, c_g:][None]

        u = _norm_mod_call(rows, h, norm_mix[l], mods, 0, 1, BF16)
        p_qk = _matmul(u, w_main, l, 0, QK_COLS, F32)
        v = _matmul(u, w_main, l, c_v, V_A + V_B, BF16)
        ro = _matmul(u, w_main, l, c_ro, V_A + V_B, F32)
        small = _matmul(u, w_sm, 0, 0, LANES, F32)
        sg = _matmul(u, w_g, 0, 0, 2 * d, BF16, act="sigmoid")
        qk = _conv_silu(rows, p_qk, conv_w[l], qk_scale)
        gates_t = small[:, 2 * GLA_RANK:n_small].T
        oaf, oab = _gla_scan(rows, qk, v, small, gla_gate_w[l], gla_gate_b[l])
        obf, obb = _mlstm_scan(rows, qk, v, small, gates_t, mlstm_gate_b[l])
        ha, hb = _mix_prep(oaf, oab, obf, obb, ro, gla_out_norm[l], mlstm_out_norm[l])
        merged = _up_merge(ha, hb, w_up_a, w_up_b, l, sg)
        h = _matmul(merged, w_o, l, 0, d, F32, res=h, rows=rows, mods=mods, which=2)

        if l % 2 == 0:
            e = l // 2
            vv = _norm_mod_call(rows, h, norm_ffn[l], mods, 3, 4, BF16)
            hid = _swiglu_up(vv, ffn_w1, ffn_w3, e)
            h = _matmul(hid, ffn_w2, e, 0, d, F32, res=h, rows=rows, mods=mods, which=5)
        else:
            e = l // 2
            n_tok = rows.n_lat if l == depth - 1 else rows.m
            h = _moe_ffn(rows, h, norm_ffn[l], mods, router_w[e], moe_w1_bf, moe_w3_bf, moe_w2_bf, e, n_tok,
                         final_gn=norm_final if l == depth - 1 else None)

    out = h if depth % 2 == 0 else _final_norm(rows, h, norm_final)
    return out.reshape(batch, seq, d)
```

```python
import functools
import math

import jax
import jax.numpy as jnp
from jax import lax
from jax.experimental import pallas as pl
from jax.experimental.pallas import tpu as pltpu

F32 = jnp.float32
BF16 = jnp.bfloat16
I32 = jnp.int32

GRID_W = 64
GLA_HEADS = 4
GLA_DK = 128
GLA_DV = 256
GLA_RANK = 16
GLA_GATE_NORMALIZER = 16.0
MLSTM_HEADS = 4
MLSTM_DQK = 128
MLSTM_DV = 256
GATE_SOFT_CAP = 15.0
N_GATE_B = 4 * MLSTM_HEADS
TOP_K = 2
EPS = 1e-6
QA = GLA_HEADS * GLA_DK
QB = MLSTM_HEADS * MLSTM_DQK
QK_COLS = 2 * QA + 2 * QB
V_A = GLA_HEADS * GLA_DV
V_B = MLSTM_HEADS * MLSTM_DV

LANES = 128
SUBLANES = 8
BF16_ROWS = 16
SCAN_CHUNK = 256
VMEM_LIMIT = 56 * 1024 * 1024
NEG_BIG = -1e30
LOG2_E = 1.4426950408889634
HIGHEST = lax.Precision.HIGHEST


def _cparams(*sem):
    return pltpu.CompilerParams(dimension_semantics=sem, vmem_limit_bytes=VMEM_LIMIT)


def _pick(dim, prefs):
    for p in prefs:
        if dim % p == 0:
            return p
    return dim


def _dot(a, b, precision=None):
    return jnp.dot(a, b, preferred_element_type=F32, precision=precision)


def _dot_nt(a, b, precision=None):
    return lax.dot_general(a, b, (((1,), (1,)), ((), ())), preferred_element_type=F32, precision=precision)


def _dot_tn(a, b):
    return lax.dot_general(a, b, (((0,), (0,)), ((), ())), preferred_element_type=F32)


def _silu(x):
    return x * jax.nn.sigmoid(x)


def _split3(x):
    hi = x.astype(BF16)
    rest = x - hi.astype(F32)
    mid = rest.astype(BF16)
    lo = (rest - mid.astype(F32)).astype(BF16)
    return hi, mid, lo


def _ada_kernel(c_ref, w_ref, b_ref, o_ref):
    o_ref[...] = _dot(_silu(c_ref[...]), w_ref[...], HIGHEST) + b_ref[...]


def _ada_mods(cvec, ada_w, ada_b):
    depth, d, n = ada_w.shape
    tn = _pick(n, (1024, 512, 256, 128))
    return pl.pallas_call(
        _ada_kernel,
        grid=(depth, n // tn),
        in_specs=[pl.BlockSpec(cvec.shape, lambda l, j: (0, 0)),
                  pl.BlockSpec((None, d, tn), lambda l, j: (l, 0, j)),
                  pl.BlockSpec((None, 1, tn), lambda l, j: (l, 0, j))],
        out_specs=pl.BlockSpec((None, cvec.shape[0], tn), lambda l, j: (l, 0, j)),
        out_shape=jax.ShapeDtypeStruct((depth, cvec.shape[0], n), F32),
        compiler_params=_cparams("parallel", "parallel"),
        name="ada_mods",
    )(cvec, ada_w, ada_b.reshape(depth, 1, n))


class _Rows:
    def __init__(self, batch, seq, ctx_len):
        self.batch, self.seq, self.ctx_len = batch, seq, ctx_len
        self.n_lat = batch * seq
        self.n_ctx = batch * ctx_len
        self.m = self.n_lat + self.n_ctx

    def row_tile(self, prefs):
        return _pick(math.gcd(self.seq, self.n_ctx), prefs)

    def gid(self, i, tm):
        r0 = i * tm
        return jnp.where(r0 < self.n_lat, r0 // self.seq, self.batch)


def _mod_spec(rows, which, tm, d):
    return pl.BlockSpec((None, None, 1, d), lambda i: (which, rows.gid(i, tm), 0, 0))


def _norm_mod(x, gn, sh, sc):
    y = x * lax.rsqrt(jnp.mean(x * x, axis=-1, keepdims=True) + EPS) * gn
    return y * (1.0 + sc) + sh


def _norm_mod_kernel(x_ref, gn_ref, sh_ref, sc_ref, o_ref):
    o_ref[...] = _norm_mod(x_ref[...], gn_ref[...], sh_ref[...], sc_ref[...]).astype(o_ref.dtype)


def _norm_mod_router_kernel(x_ref, gn_ref, sh_ref, sc_ref, rw_ref, o_ref, ei_ref, ew_ref, *, n_experts):
    u = _norm_mod(x_ref[...], gn_ref[...], sh_ref[...], sc_ref[...])
    o_ref[...] = u
    logits = _dot(u, rw_ref[...], HIGHEST)
    lane_i = lax.broadcasted_iota(I32, logits.shape, 1)
    lane = lane_i.astype(F32)
    lg = jnp.where(lane_i < n_experts, logits, -jnp.inf)
    m1 = jnp.max(lg, axis=-1, keepdims=True)
    i1 = jnp.min(jnp.where(lg == m1, lane, float(LANES)), axis=-1, keepdims=True)
    lg2 = jnp.where(lane == i1, -jnp.inf, lg)
    m2 = jnp.max(lg2, axis=-1, keepdims=True)
    i2 = jnp.min(jnp.where(lg2 == m2, lane, float(LANES)), axis=-1, keepdims=True)
    e = jnp.exp(m2 - m1)
    w1 = 1.0 / (1.0 + e)
    w2 = e / (1.0 + e)
    ei_ref[...] = jnp.where(lane_i == 0, i1, jnp.where(lane_i == 1, i2, 0.0)).astype(I32)
    ew_ref[...] = jnp.where(lane_i == 0, w1, jnp.where(lane_i == 1, w2, 0.0))


def _norm_mod_call(rows, x, gn, mods, which_sh, which_sc, out_dtype, router_w=None):
    m, d = x.shape
    tm = rows.row_tile((256, 128, 64, 32, 16, 8))
    in_specs = [pl.BlockSpec((tm, d), lambda i: (i, 0)),
                pl.BlockSpec((1, d), lambda i: (0, 0)),
                _mod_spec(rows, which_sh, tm, d),
                _mod_spec(rows, which_sc, tm, d)]
    row_spec = pl.BlockSpec((tm, d), lambda i: (i, 0))
    if router_w is None:
        return pl.pallas_call(
            _norm_mod_kernel, grid=(m // tm,), in_specs=in_specs, out_specs=row_spec,
            out_shape=jax.ShapeDtypeStruct((m, d), out_dtype),
            compiler_params=_cparams("parallel"), name="norm_mod",
        )(x, gn.reshape(1, d), mods, mods)
    n_experts = router_w.shape[1]
    rw = jnp.zeros((d, LANES), F32).at[:, :n_experts].set(router_w)
    lane_spec = pl.BlockSpec((tm, LANES), lambda i: (i, 0))
    return pl.pallas_call(
        functools.partial(_norm_mod_router_kernel, n_experts=n_experts),
        grid=(m // tm,),
        in_specs=in_specs + [pl.BlockSpec((d, LANES), lambda i: (0, 0))],
        out_specs=[row_spec, lane_spec, lane_spec],
        out_shape=[jax.ShapeDtypeStruct((m, d), F32), jax.ShapeDtypeStruct((m, LANES), I32),
                   jax.ShapeDtypeStruct((m, LANES), F32)],
        compiler_params=_cparams("parallel"), name="norm_mod_router",
    )(x, gn.reshape(1, d), mods, mods, rw)


def _final_norm_kernel(x_ref, gn_ref, o_ref):
    x = x_ref[...]
    o_ref[...] = x * lax.rsqrt(jnp.mean(x * x, axis=-1, keepdims=True) + EPS) * gn_ref[...]


def _final_norm(rows, x, gn):
    d = x.shape[1]
    tm = rows.row_tile((256, 128, 64, 32, 16, 8))
    return pl.pallas_call(
        _final_norm_kernel, grid=(rows.n_lat // tm,),
        in_specs=[pl.BlockSpec((tm, d), lambda i: (i, 0)), pl.BlockSpec((1, d), lambda i: (0, 0))],
        out_specs=pl.BlockSpec((tm, d), lambda i: (i, 0)),
        out_shape=jax.ShapeDtypeStruct((rows.n_lat, d), F32),
        compiler_params=_cparams("parallel"), name="final_norm",
    )(x, gn.reshape(1, d))


def _stash_weights(pairs):
    @pl.when(pl.program_id(1) == 0)
    def _():
        for w_ref, wbuf in pairs:
            wbuf[...] = w_ref[...].astype(BF16)


def _w_spec(kdim, tn, layer, col_blk0):
    return pl.BlockSpec((None, kdim, tn), lambda j, i: (layer, 0, col_blk0 + j))


class _SideCast:
    def __init__(self, src, row0, rows, steps):
        self.src = src
        self.cols = src.shape[1]
        self.n_blocks = next(nb for nb in range(min(steps, rows // BF16_ROWS), 0, -1)
                             if rows % nb == 0 and (rows // nb) % BF16_ROWS == 0)
        self.block_rows = rows // self.n_blocks
        assert row0 % self.block_rows == 0
        self.blk0 = row0 // self.block_rows
        self.out_shape = jax.ShapeDtypeStruct((rows, self.cols), BF16)

    def specs(self, n_inner):
        step = lambda j, i: jnp.minimum(j * n_inner + i, self.n_blocks - 1)
        blk = (self.block_rows, self.cols)
        return (pl.BlockSpec(blk, lambda j, i: (self.blk0 + step(j, i), 0)),
                pl.BlockSpec(blk, lambda j, i: (step(j, i), 0)))


def _mm_kernel(*refs, act, residual, side):
    a_ref, w_ref = refs[:2]
    x_ref, g_ref = refs[2:4] if residual else (None, None)
    pos = 4 if residual else 2
    if side:
        side_in, o_ref, side_out, wbuf = refs[pos:pos + 4]
        side_out[...] = side_in[...].astype(BF16)
    else:
        o_ref, wbuf = refs[pos:pos + 2]
    _stash_weights([(w_ref, wbuf)])
    r = _dot(a_ref[...], wbuf[...])
    if act == "sigmoid":
        r = jax.nn.sigmoid(r)
    if residual:
        r = x_ref[...] + g_ref[...] * r
    o_ref[...] = r.astype(o_ref.dtype)


def _mm_tiles(m, kdim, n, rows):
    tn = _pick(n, (1024, 512, 256, 128) if kdim <= 2048 else (512, 256, 128))
    row_prefs = (1024, 512, 256, 128, 64, 32, 16, 8) if kdim <= 2048 else (512, 256, 128, 64, 32, 16, 8)
    tm = _pick(m, row_prefs) if rows is None else rows.row_tile(row_prefs)
    return tm, tn


def _matmul(a, w, layer, col0, n, out_dtype, *, act=None, res=None, rows=None, mods=None, which=None,
            side=None):
    m, kdim = a.shape
    tm, tn = _mm_tiles(m, kdim, n, rows)
    assert col0 % tn == 0
    residual = res is not None
    in_specs = [pl.BlockSpec((tm, kdim), lambda j, i: (i, 0)), _w_spec(kdim, tn, layer, col0 // tn)]
    args = [a, w]
    if residual:
        in_specs += [pl.BlockSpec((tm, tn), lambda j, i: (i, j)),
                     pl.BlockSpec((None, None, 1, tn), lambda j, i: (which, rows.gid(i, tm), 0, j))]
        args += [res, mods]
    out_specs = [pl.BlockSpec((tm, tn), lambda j, i: (i, j))]
    out_shape = [jax.ShapeDtypeStruct((m, n), out_dtype)]
    if side is not None:
        side_in, side_out = side.specs(m // tm)
        in_specs.append(side_in)
        args.append(side.src)
        out_specs.append(side_out)
        out_shape.append(side.out_shape)
    outs = pl.pallas_call(
        functools.partial(_mm_kernel, act=act, residual=residual, side=side is not None),
        grid=(n // tn, m // tm),
        in_specs=in_specs,
        out_specs=out_specs,
        out_shape=out_shape,
        scratch_shapes=[pltpu.VMEM((kdim, tn), BF16)],
        compiler_params=_cparams("parallel", "arbitrary"),
        name="matmul",
    )(*args)
    return outs[0] if side is None else outs


def _swiglu_kernel(*refs, side):
    if side:
        a_ref, w1_ref, w3_ref, side_in, o_ref, side_out, w1buf, w3buf = refs
        side_out[...] = side_in[...].astype(BF16)
    else:
        a_ref, w1_ref, w3_ref, o_ref, w1buf, w3buf = refs
    _stash_weights([(w1_ref, w1buf), (w3_ref, w3buf)])
    a = a_ref[...]
    o_ref[...] = (_silu(_dot(a, w1buf[...])) * _dot(a, w3buf[...])).astype(o_ref.dtype)


def _swiglu_tiles(m, f):
    return _pick(m, (1024, 512, 256, 128, 64, 32, 16, 8)), _pick(f, (512, 256, 128))


def _swiglu_up(a, w1, w3, layer, side=None):
    m, d = a.shape
    f = w1.shape[2]
    tm, tn = _swiglu_tiles(m, f)
    in_specs = [pl.BlockSpec((tm, d), lambda j, i: (i, 0)), _w_spec(d, tn, layer, 0), _w_spec(d, tn, layer, 0)]
    args = [a, w1, w3]
    out_specs = [pl.BlockSpec((tm, tn), lambda j, i: (i, j))]
    out_shape = [jax.ShapeDtypeStruct((m, f), BF16)]
    if side is not None:
        side_in, side_out = side.specs(m // tm)
        in_specs.append(side_in)
        args.append(side.src)
        out_specs.append(side_out)
        out_shape.append(side.out_shape)
    outs = pl.pallas_call(
        functools.partial(_swiglu_kernel, side=side is not None), grid=(f // tn, m // tm),
        in_specs=in_specs, out_specs=out_specs, out_shape=out_shape,
        scratch_shapes=[pltpu.VMEM((d, tn), BF16), pltpu.VMEM((d, tn), BF16)],
        compiler_params=_cparams("parallel", "arbitrary"), name="swiglu_up",
    )(*args)
    return outs[0] if side is None else outs


def _up_merge_kernel(ha_ref, hb_ref, wa_ref, wb_ref, sga_ref, sgb_ref, o_ref, wabuf, wbbuf):
    _stash_weights([(wa_ref, wabuf), (wb_ref, wbbuf)])
    up_a = _dot(ha_ref[...], wabuf[...])
    up_b = _dot(hb_ref[...], wbbuf[...])
    o_ref[...] = (sga_ref[...] * up_a + sgb_ref[...] * up_b).astype(o_ref.dtype)


def _up_merge(ha, hb, wa, wb, layer, sg):
    m, va = ha.shape
    vb = hb.shape[1]
    d = wa.shape[2]
    tm = _pick(m, (512, 256, 128, 64, 32, 16, 8))
    tn = _pick(d, (1024, 512, 256, 128))
    nj = d // tn
    return pl.pallas_call(
        _up_merge_kernel, grid=(nj, m // tm),
        in_specs=[pl.BlockSpec((tm, va), lambda j, i: (i, 0)),
                  pl.BlockSpec((tm, vb), lambda j, i: (i, 0)),
                  _w_spec(va, tn, layer, 0), _w_spec(vb, tn, layer, 0),
                  pl.BlockSpec((tm, tn), lambda j, i: (i, j)),
                  pl.BlockSpec((tm, tn), lambda j, i: (i, j + nj))],
        out_specs=pl.BlockSpec((tm, tn), lambda j, i: (i, j)),
        out_shape=jax.ShapeDtypeStruct((m, d), BF16),
        scratch_shapes=[pltpu.VMEM((va, tn), BF16), pltpu.VMEM((vb, tn), BF16)],
        compiler_params=_cparams("parallel", "arbitrary"), name="up_merge",
    )(ha, hb, wa, wb, sg, sg)


def _conv_taps(xs, w, dy, not_first, not_last):
    n = xs.shape[0]
    left = jnp.where(not_first, pltpu.roll(xs, 1, 0), 0.0)
    right = jnp.where(not_last, pltpu.roll(xs, n - 1, 0), 0.0)
    return w[3 * dy:3 * dy + 1] * left + w[3 * dy + 1:3 * dy + 2] * xs + w[3 * dy + 2:3 * dy + 3] * right


def _conv_kernel(up_ref, x_ref, dn_ref, w_ref, s_ref, o_ref, pad_ref, *, rows, rb, strip):
    r0 = pl.program_id(0) * rb
    w = w_ref[...]

    @pl.when(r0 < rows.n_lat)
    def _():
        tc = x_ref.shape[-1]
        at_start = lax.rem(r0, rows.seq) == 0
        at_end = lax.rem(r0 + rb, rows.seq) == 0
        pad_ref[pl.ds(0, GRID_W), :] = jnp.where(at_start, 0.0, up_ref[...])
        pad_ref[pl.ds(GRID_W + rb, GRID_W), :] = jnp.where(at_end, 0.0, dn_ref[...])
        pad_ref[pl.ds(GRID_W, rb), :] = x_ref[...]
        col = lax.broadcasted_iota(I32, (strip, tc), 0) % GRID_W
        not_first = col != 0
        not_last = col != GRID_W - 1
        for s in range(rb // strip):
            acc = jnp.zeros((strip, tc), F32)
            for dy in range(3):
                xs = pad_ref[pl.ds(s * strip + dy * GRID_W, strip), :]
                acc = acc + _conv_taps(xs, w, dy, not_first, not_last)
            o_ref[pl.ds(s * strip, strip), :] = _silu(acc) * s_ref[...]

    @pl.when(r0 >= rows.n_lat)
    def _():
        x = x_ref[...]
        pos = lax.broadcasted_iota(I32, x.shape, 0) % rows.ctx_len
        acc = _conv_taps(x, w, 1, pos != 0, pos != rows.ctx_len - 1)
        o_ref[...] = _silu(acc) * s_ref[...]


def _conv_silu(rows, p_qk, conv_w, scale):
    m, c = p_qk.shape
    tc = _pick(c, (512, 256, 128))
    rb = rows.n_ctx
    strip = _pick(rb, (512, 256, 128, 64))
    per = rb // GRID_W
    last = m // GRID_W - 1
    return pl.pallas_call(
        functools.partial(_conv_kernel, rows=rows, rb=rb, strip=strip),
        grid=(m // rb, c // tc),
        in_specs=[pl.BlockSpec((GRID_W, tc), lambda i, j: (jnp.maximum(i * per - 1, 0), j)),
                  pl.BlockSpec((rb, tc), lambda i, j: (i, j)),
                  pl.BlockSpec((GRID_W, tc), lambda i, j: (jnp.minimum((i + 1) * per, last), j)),
                  pl.BlockSpec((9, tc), lambda i, j: (0, j)),
                  pl.BlockSpec((1, tc), lambda i, j: (0, j))],
        out_specs=pl.BlockSpec((rb, tc), lambda i, j: (i, j)),
        out_shape=jax.ShapeDtypeStruct((m, c), F32),
        scratch_shapes=[pltpu.VMEM((rb + 2 * GRID_W, tc), F32)],
        compiler_params=_cparams("parallel", "parallel"), name="conv_silu",
    )(p_qk, p_qk, p_qk, conv_w.reshape(9, c), scale)


def _causal_masks(n):
    t = lax.broadcasted_iota(I32, (n, n), 0)
    s = lax.broadcasted_iota(I32, (n, n), 1)
    return t, s, (s <= t, s >= t)


def _block_ref_rows(x, blk, row):
    n, c = x.shape
    if blk >= SUBLANES:
        x3 = x.reshape(n // blk, blk, c)
        return jnp.broadcast_to(x3[:, row:row + 1, :], x3.shape).reshape(n, c)
    x3 = x.reshape(n // SUBLANES, SUBLANES, c)
    sub = lax.broadcasted_iota(I32, x3.shape, 1)
    out = jnp.zeros_like(x3)
    for g in range(SUBLANES // blk):
        r = g * blk + row
        out = jnp.where(sub // blk == g, jnp.broadcast_to(x3[:, r:r + 1, :], x3.shape), out)
    return out.reshape(n, c)


def _gla_direction(d, q, k, v, lr, wg, bg, s_ref, causal, lev_blk):
    n = q.shape[0]
    hn = n // 2
    lh, lm, _ = _split3(lr)
    wh, wm, _ = _split3(wg)
    z = _dot(jnp.concatenate([lh, lh, lm], axis=1), jnp.concatenate([wh, wm, wh], axis=0)) + bg
    la = jax.nn.log_sigmoid(z) * (LOG2_E / GLA_GATE_NORMALIZER)
    tri = causal.astype(BF16)
    c3 = _dot(tri, jnp.concatenate(_split3(la), axis=1))
    nk = la.shape[1]
    cum = c3[:, :nk] + c3[:, nk:2 * nk] + c3[:, 2 * nk:]
    total = cum[n - 1:n] if d == 0 else cum[0:1]
    state = s_ref[...]
    out = _dot((q * jnp.exp2(cum)).astype(BF16), state.astype(BF16))
    qb = q.astype(BF16)
    kb = k.astype(BF16)
    lo, hi = slice(0, hn), slice(hn, n)
    diag = [jnp.where(lev_blk == -1, _dot_nt(qb[h], kb[h]).astype(BF16), jnp.zeros((), BF16)) for h in (lo, hi)]
    top = n.bit_length() - 2
    for l in range(top + 1):
        half = 1 << l
        ref = _block_ref_rows(cum, 2 * half, half - 1 if d == 0 else half)
        e = jnp.exp2(-jnp.abs(cum - ref)).astype(BF16)
        qe = qb * e
        ke = kb * e
        if l < top:
            diag = [jnp.where(lev_blk == l, _dot_nt(qe[h], ke[h]).astype(BF16), sc)
                    for h, sc in zip((lo, hi), diag)]
        else:
            cross = (_dot_nt(qe[hi], ke[lo]) if d == 0 else _dot_nt(qe[lo], ke[hi])).astype(BF16)
    zero = jnp.zeros((hn, hn), BF16)
    upper, lower = ([diag[0], zero], [cross, diag[1]]) if d == 0 else ([diag[0], cross], [zero, diag[1]])
    scores = jnp.concatenate([jnp.concatenate(upper, axis=1), jnp.concatenate(lower, axis=1)], axis=0)
    out = out + _dot(scores, v)
    k_out = (k * jnp.exp2(total - cum)).astype(BF16)
    dk = state.shape[0]
    et = jnp.broadcast_to(jnp.exp2(total), (dk, dk)).T
    scale = jnp.concatenate([et] * (state.shape[1] // dk), axis=1)
    s_ref[...] = scale * state + _dot_tn(k_out, v)
    return out


def _gla_kernel(qf, kf, vf, smf, qb, kb, vb, smb, wg_ref, bg_ref, of_ref, ob_ref, sf_ref, sb_ref):
    @pl.when(pl.program_id(1) == 0)
    def _():
        sf_ref[...] = jnp.zeros_like(sf_ref)
        sb_ref[...] = jnp.zeros_like(sb_ref)

    n = qf.shape[0]
    _, _, causal = _causal_masks(n)
    t_idx, s_idx, causal_blk = _causal_masks(n // 2)
    lev = 31 - lax.clz(t_idx ^ s_idx)
    dirs = ((qf, kf, vf, smf, of_ref, sf_ref), (qb, kb, vb, smb, ob_ref, sb_ref))
    for d, (q, k, v, sm, o_ref, s_ref) in enumerate(dirs):
        lr = sm[:, d * GLA_RANK:(d + 1) * GLA_RANK]
        lev_blk = jnp.where(causal_blk[d], lev, -2).astype(BF16)
        for hd in range(GLA_HEADS):
            ks = slice(hd * GLA_DK, (hd + 1) * GLA_DK)
            vs = slice(hd * GLA_DV, (hd + 1) * GLA_DV)
            o_ref[:, vs] = _gla_direction(d, q[:, ks], k[:, ks], v[:, vs], lr, wg_ref[d, :, ks], bg_ref[d, :, ks],
                                          s_ref.at[hd], causal[d], lev_blk)


def _soft_cap(z):
    return GATE_SOFT_CAP * jnp.tanh(z * (1.0 / GATE_SOFT_CAP))


def _mlstm_direction(d, head, q, k, v, g_col, g_row, c_ref, m_ref, causal):
    n = q.shape[0]
    tri = causal.astype(BF16)
    i_idx = 2 * MLSTM_HEADS * d + head
    f_idx = i_idx + MLSTM_HEADS
    sel_rows = lax.broadcasted_iota(I32, (3 * N_GATE_B, LANES), 0) % N_GATE_B
    rep = lambda a, idx: _dot(jnp.concatenate(_split3(a), axis=1), (sel_rows == idx).astype(BF16))
    i_rep = rep(g_col, i_idx)
    f_rep = rep(jax.nn.log_sigmoid(g_col), f_idx)
    c3 = _dot(tri, jnp.concatenate(_split3(f_rep), axis=1))
    cum = c3[:, :LANES] + c3[:, LANES:2 * LANES] + c3[:, 2 * LANES:]
    r3 = _dot_nt(jnp.concatenate(_split3(jax.nn.log_sigmoid(g_row)), axis=0), tri)
    cum_r_all = r3[:N_GATE_B] + r3[N_GATE_B:2 * N_GATE_B] + r3[2 * N_GATE_B:]
    sub = lax.broadcasted_iota(I32, (N_GATE_B, 1), 0)
    pick_r = lambda a, idx: jnp.sum(jnp.where(sub == idx, a, 0.0), axis=0, keepdims=True)
    i_r, cum_r = pick_r(g_row, i_idx), pick_r(cum_r_all, f_idx)
    wide = lambda a, reps: jnp.concatenate([a] * reps, axis=1)
    nt = n // LANES
    total = cum[n - 1:n] if d == 0 else cum[0:1]
    m_prev = m_ref[...]
    dlog = jnp.where(causal, wide(cum, nt) - cum_r + i_r, NEG_BIG)
    inter_log = cum + m_prev
    row_max = dlog[:, :LANES]
    for t in range(1, nt):
        row_max = jnp.maximum(row_max, dlog[:, t * LANES:(t + 1) * LANES])
    m_t = jnp.maximum(inter_log, jnp.max(row_max, axis=1, keepdims=True))
    w_inter = jnp.exp(inter_log - m_t)
    qb16 = q.astype(BF16)
    s = _dot_nt(qb16, k.astype(BF16)) * jnp.exp(dlog - wide(m_t, nt))
    dv = v.shape[1]
    v_ext = jnp.concatenate([v, jnp.ones((n, LANES), BF16)], axis=1)
    state = c_ref[...]
    acc = wide(w_inter, dv // LANES + 1) * _dot(qb16, state.astype(BF16)) + _dot(s.astype(BF16), v_ext)
    bound = jnp.maximum(jnp.abs(acc[:, dv:]), jnp.exp(-m_t))
    out = acc[:, :dv] / wide(bound, dv // LANES)
    g = total - cum + i_rep
    m_new = jnp.maximum(total + m_prev, jnp.max(g, axis=0, keepdims=True))
    w_c = jnp.exp(total + m_prev - m_new)
    w_k = jnp.exp(g - m_new)
    c_ref[...] = wide(w_c, dv // LANES + 1) * state + _dot_tn((k * w_k).astype(BF16), v_ext)
    m_ref[...] = m_new
    return out


def _mlstm_kernel(qf, kf, vf, gcf, grf, qb, kb, vb, gcb, grb, brow_ref, bcol_ref,
                  of_ref, ob_ref, cf_ref, cb_ref, mf_ref, mb_ref):
    @pl.when(pl.program_id(1) == 0)
    def _():
        cf_ref[...] = jnp.zeros_like(cf_ref)
        cb_ref[...] = jnp.zeros_like(cb_ref)
        mf_ref[...] = jnp.zeros_like(mf_ref)
        mb_ref[...] = jnp.zeros_like(mb_ref)

    n = qf.shape[0]
    _, _, causal = _causal_masks(n)
    g0 = 2 * GLA_RANK
    dirs = ((qf, kf, vf, gcf, grf, of_ref, cf_ref, mf_ref), (qb, kb, vb, gcb, grb, ob_ref, cb_ref, mb_ref))
    for d, (q, k, v, gc, gr, o_ref, c_ref, m_ref) in enumerate(dirs):
        g_col = _soft_cap(gc[:, g0:g0 + N_GATE_B] + brow_ref[...])
        g_row = _soft_cap(gr[...] + bcol_ref[...])
        for hd in range(MLSTM_HEADS):
            qs = slice(hd * MLSTM_DQK, (hd + 1) * MLSTM_DQK)
            vs = slice(hd * MLSTM_DV, (hd + 1) * MLSTM_DV)
            o_ref[:, vs] = _mlstm_direction(d, hd, q[:, qs], k[:, qs], v[:, vs], g_col, g_row,
                                            c_ref.at[hd], m_ref.at[hd], causal[d])


def _scan_row_maps(rows, chunk):
    lat_chunks = rows.seq // chunk
    ctx_chunks = rows.ctx_len // chunk
    ctx0 = rows.n_lat // chunk

    def fwd(b, s):
        return jnp.where(s < ctx_chunks, ctx0 + b * ctx_chunks + s, b * lat_chunks + (s - ctx_chunks))

    def bwd(b, s):
        return jnp.where(s < ctx_chunks, ctx0 + b * ctx_chunks + (ctx_chunks - 1 - s),
                         b * lat_chunks + (lat_chunks - 1 - (s - ctx_chunks)))

    return fwd, bwd, ctx_chunks + lat_chunks


def _gla_scan(rows, qk, v, small, gate_w, gate_b):
    m = qk.shape[0]
    n = SCAN_CHUNK
    fwd, bwd, steps = _scan_row_maps(rows, n)

    def specs(rmap):
        return [pl.BlockSpec((n, QA), lambda b, s: (rmap(b, s), 0)),
                pl.BlockSpec((n, QA), lambda b, s: (rmap(b, s), 1)),
                pl.BlockSpec((n, V_A), lambda b, s: (rmap(b, s), 0)),
                pl.BlockSpec((n, LANES), lambda b, s: (rmap(b, s), 0))]

    out_spec = lambda rmap: pl.BlockSpec((n, V_A), lambda b, s: (rmap(b, s), 0))
    out_sds = jax.ShapeDtypeStruct((m, V_A), F32)
    state = pltpu.VMEM((GLA_HEADS, GLA_DK, GLA_DV), F32)
    return pl.pallas_call(
        _gla_kernel,
        grid=(rows.batch, steps),
        in_specs=specs(fwd) + specs(bwd) + [
            pl.BlockSpec((2, GLA_RANK, QA), lambda b, s: (0, 0, 0)),
            pl.BlockSpec((2, 1, QA), lambda b, s: (0, 0, 0))],
        out_specs=[out_spec(fwd), out_spec(bwd)],
        out_shape=[out_sds, out_sds],
        scratch_shapes=[state, state],
        compiler_params=_cparams("parallel", "arbitrary"), name="gla_scan",
    )(qk, qk, v, small, qk, qk, v, small, gate_w, gate_b.reshape(2, 1, QA))


def _mlstm_scan(rows, qk, v, small, gates_t, gate_b):
    m = qk.shape[0]
    n = SCAN_CHUNK
    fwd, bwd, steps = _scan_row_maps(rows, n)
    q0 = 2 * QA // QB
    v0 = V_A // V_B

    def specs(rmap):
        return [pl.BlockSpec((n, QB), lambda b, s: (rmap(b, s), q0)),
                pl.BlockSpec((n, QB), lambda b, s: (rmap(b, s), q0 + 1)),
                pl.BlockSpec((n, V_B), lambda b, s: (rmap(b, s), v0)),
                pl.BlockSpec((n, LANES), lambda b, s: (rmap(b, s), 0)),
                pl.BlockSpec((N_GATE_B, n), lambda b, s: (0, rmap(b, s)))]

    out_spec = lambda rmap: pl.BlockSpec((n, V_B), lambda b, s: (rmap(b, s), 0))
    out_sds = jax.ShapeDtypeStruct((m, V_B), F32)
    state = pltpu.VMEM((MLSTM_HEADS, MLSTM_DQK, MLSTM_DV + LANES), F32)
    stab = pltpu.VMEM((MLSTM_HEADS, 1, LANES), F32)
    return pl.pallas_call(
        _mlstm_kernel,
        grid=(rows.batch, steps),
        in_specs=specs(fwd) + specs(bwd) + [
            pl.BlockSpec((1, N_GATE_B), lambda b, s: (0, 0)),
            pl.BlockSpec((N_GATE_B, 1), lambda b, s: (0, 0))],
        out_specs=[out_spec(fwd), out_spec(bwd)],
        out_shape=[out_sds, out_sds],
        scratch_shapes=[state, state, stab, stab],
        compiler_params=_cparams("parallel", "arbitrary"), name="mlstm_scan",
    )(qk, qk, v, small, gates_t, qk, qk, v, small, gates_t,
      gate_b.reshape(1, N_GATE_B), gate_b.reshape(N_GATE_B, 1))


def _mix_prep_kernel(oaf, oab, obf, obb, ro_ref, gna_ref, gnb_ref, ha_ref, hb_ref):
    branches = ((oaf, oab, gna_ref, ha_ref, 0, GLA_HEADS, GLA_DV, _silu),
                (obf, obb, gnb_ref, hb_ref, V_A, MLSTM_HEADS, MLSTM_DV, jax.nn.sigmoid))
    for of, ob, gn_ref, h_ref, off, heads, dv, gate_fn in branches:
        for h in range(heads):
            sl = slice(h * dv, (h + 1) * dv)
            o = of[:, sl] + ob[:, sl]
            y = o * lax.rsqrt(jnp.mean(o * o, axis=-1, keepdims=True) + EPS) * gn_ref[:, sl]
            gate = gate_fn(ro_ref[:, off + h * dv:off + (h + 1) * dv])
            h_ref[:, sl] = (y * gate).astype(h_ref.dtype)


def _mix_prep(oaf, oab, obf, obb, ro, gna, gnb):
    m = ro.shape[0]
    tm = _pick(m, (256, 128, 64, 32, 16, 8))
    row = lambda c: pl.BlockSpec((tm, c), lambda i: (i, 0))
    vec = lambda c: pl.BlockSpec((1, c), lambda i: (0, 0))
    return pl.pallas_call(
        _mix_prep_kernel, grid=(m // tm,),
        in_specs=[row(V_A), row(V_A), row(V_B), row(V_B), row(V_A + V_B), vec(V_A), vec(V_B)],
        out_specs=[row(V_A), row(V_B)],
        out_shape=[jax.ShapeDtypeStruct((m, V_A), BF16), jax.ShapeDtypeStruct((m, V_B), BF16)],
        compiler_params=_cparams("parallel"), name="mix_prep",
    )(oaf, oab, obf, obb, ro, gna.reshape(1, V_A), gnb.reshape(1, V_B))


def _expert_up_kernel(te_ref, na_ref, src0_ref, srcn_ref, v_hbm, w1_ref, w3_ref, o_ref, xbuf, abuf, sem,
                      *, tm, issue_steps):
    del te_ref
    i = pl.program_id(0)
    j = pl.program_id(1)
    slot = lax.rem(i, 2)
    per = tm // issue_steps

    def row_copy(src_ref, r, s):
        return pltpu.make_async_copy(v_hbm.at[pl.ds(src_ref[0, r], 1)], xbuf.at[s, pl.ds(r, 1)], sem.at[s])

    @pl.when(jnp.logical_and(i == 0, j == 0))
    def _():
        def start(r, carry):
            row_copy(src0_ref, r, 0).start()
            return carry

        lax.fori_loop(0, tm, start, 0)

    @pl.when(j == 0)
    def _():
        def wait(r, carry):
            row_copy(src0_ref, 0, slot).wait()
            return carry

        lax.fori_loop(0, tm, wait, 0, unroll=32)
        abuf[...] = xbuf[slot].astype(BF16)

    @pl.when(jnp.logical_and(i + 1 < pl.num_programs(0), j < issue_steps))
    def _():
        for t in range(per):
            row_copy(srcn_ref, j * per + t, 1 - slot).start()

    @pl.when(i < na_ref[0])
    def _():
        a = abuf[...]
        o_ref[...] = (_silu(_dot(a, w1_ref[...])) * _dot(a, w3_ref[...])).astype(o_ref.dtype)

    @pl.when(i >= na_ref[0])
    def _():
        o_ref[...] = jnp.zeros_like(o_ref)


def _expert_up(v, src, w1, w3, layer, tile_expert, n_active, tm):
    n_tiles = src.shape[0]
    d = v.shape[1]
    f = w1.shape[3]
    tn = _pick(f, (1408, 512, 256, 128))
    nj = f // tn
    issue_steps = max(s for s in (1, 2, 4, 8) if s <= nj)
    wmap = lambda i, j, te, na: (layer, te[i], 0, jnp.where(i < na[0], j, 0))
    return pl.pallas_call(
        functools.partial(_expert_up_kernel, tm=tm, issue_steps=issue_steps),
        grid_spec=pltpu.PrefetchScalarGridSpec(
            num_scalar_prefetch=2, grid=(n_tiles, nj),
            in_specs=[pl.BlockSpec((None, 1, tm), lambda i, j, te, na: (0, 0, 0), memory_space=pltpu.SMEM),
                      pl.BlockSpec((None, 1, tm), lambda i, j, te, na: (jnp.minimum(i + 1, n_tiles - 1), 0, 0),
                                   memory_space=pltpu.SMEM),
                      pl.BlockSpec(memory_space=pl.ANY),
                      pl.BlockSpec((None, None, d, tn), wmap),
                      pl.BlockSpec((None, None, d, tn), wmap)],
            out_specs=pl.BlockSpec((tm, tn), lambda i, j, te, na: (i, j)),
            scratch_shapes=[pltpu.VMEM((2, tm, d), F32), pltpu.VMEM((tm, d), BF16),
                            pltpu.SemaphoreType.DMA((2,))]),
        out_shape=jax.ShapeDtypeStruct((n_tiles * tm, f), BF16),
        compiler_params=_cparams("arbitrary", "arbitrary"), name="expert_up",
    )(tile_expert, n_active, src, src, v, w1, w3)


def _expert_down_kernel(te_ref, na_ref, a_ref, w_ref, o_ref):
    del te_ref
    i = pl.program_id(0)

    @pl.when(i < na_ref[0])
    def _():
        o_ref[...] = _dot(a_ref[...], w_ref[...])

    @pl.when(i >= na_ref[0])
    def _():
        o_ref[...] = jnp.zeros_like(o_ref)


def _expert_down(hs, w2, layer, tile_expert, n_active, tm):
    p, f = hs.shape
    d = w2.shape[3]
    tn = _pick(d, (1024, 512, 256, 128))
    return pl.pallas_call(
        _expert_down_kernel,
        grid_spec=pltpu.PrefetchScalarGridSpec(
            num_scalar_prefetch=2, grid=(p // tm, d // tn),
            in_specs=[pl.BlockSpec((tm, f), lambda i, j, te, na: (jnp.where(i < na[0], i, 0), 0)),
                      pl.BlockSpec((None, None, f, tn),
                                   lambda i, j, te, na: (layer, te[i], 0, jnp.where(i < na[0], j, 0)))],
            out_specs=pl.BlockSpec((tm, tn), lambda i, j, te, na: (i, j))),
        out_shape=jax.ShapeDtypeStruct((p, d), F32),
        compiler_params=_cparams("parallel", "parallel"), name="expert_down",
    )(tile_expert, n_active, hs, w2)


def _moe_combine_kernel(pos_ref, posn_ref, ys_hbm, x_ref, g_ref, w_ref, fn_ref, o_ref, buf, sem,
                        *, tile, final_norm):
    i = pl.program_id(0)
    slot = lax.rem(i, 2)

    def row_copy(p_ref, k, r, s):
        return pltpu.make_async_copy(ys_hbm.at[pl.ds(p_ref[0, k * tile + r], 1)],
                                     buf.at[s, k, pl.ds(r, 1)], sem.at[s])

    def start_tile(p_ref, s):
        for k in range(TOP_K):
            def start(r, carry):
                row_copy(p_ref, k, r, s).start()
                return carry

            lax.fori_loop(0, tile, start, 0, unroll=8)

    @pl.when(i == 0)
    def _():
        start_tile(pos_ref, 0)

    @pl.when(i + 1 < pl.num_programs(0))
    def _():
        start_tile(posn_ref, 1 - slot)

    def wait(r, carry):
        row_copy(pos_ref, 0, 0, slot).wait()
        return carry

    lax.fori_loop(0, TOP_K * tile, wait, 0, unroll=32)
    w = w_ref[...]
    y = w[:, 0:1] * buf[slot, 0] + w[:, 1:2] * buf[slot, 1]
    r = x_ref[...] + g_ref[...] * y
    if final_norm:
        r = r * lax.rsqrt(jnp.mean(r * r, axis=-1, keepdims=True) + EPS) * fn_ref[...]
    o_ref[...] = r


def _moe_combine(rows, ys, pos, x, mods, which, ew, m, final_gn):
    d = x.shape[1]
    tile = rows.row_tile((256, 128, 64, 32, 16, 8))
    n = m // tile
    pos_spec = lambda imap: pl.BlockSpec((None, 1, TOP_K * tile), imap, memory_space=pltpu.SMEM)
    final_norm = final_gn is not None
    fn = final_gn.reshape(1, d) if final_norm else jnp.ones((1, d), F32)
    return pl.pallas_call(
        functools.partial(_moe_combine_kernel, tile=tile, final_norm=final_norm),
        grid=(n,),
        in_specs=[pos_spec(lambda i: (i, 0, 0)),
                  pos_spec(lambda i: (jnp.minimum(i + 1, n - 1), 0, 0)),
                  pl.BlockSpec(memory_space=pl.ANY),
                  pl.BlockSpec((tile, d), lambda i: (i, 0)),
                  _mod_spec(rows, which, tile, d),
                  pl.BlockSpec((tile, LANES), lambda i: (i, 0)),
                  pl.BlockSpec((1, d), lambda i: (0, 0))],
        out_specs=pl.BlockSpec((tile, d), lambda i: (i, 0)),
        out_shape=jax.ShapeDtypeStruct((m, d), F32),
        scratch_shapes=[pltpu.VMEM((2, TOP_K, tile, d), F32), pltpu.SemaphoreType.DMA((2,))],
        compiler_params=_cparams("arbitrary"), name="moe_combine",
    )(pos, pos, ys, x, mods, ew, fn)


def _moe_dispatch(e_idx, n_experts, tm, combine_tile):
    m = e_idx.shape[0]
    ex = jnp.concatenate([e_idx[:, 0], e_idx[:, 1]])
    onehot = (ex[:, None] == jnp.arange(n_experts, dtype=I32)[None, :]).astype(I32)
    rank = jnp.sum((jnp.cumsum(onehot, axis=0) - onehot) * onehot, axis=1)
    counts = jnp.sum(onehot, axis=0)
    padded = (counts + tm - 1) // tm * tm
    ends = jnp.cumsum(padded)
    dest = (ends - padded)[ex] + rank
    n_tiles = (TOP_K * m + n_experts * (tm - 1)) // tm
    tok = jnp.concatenate([jnp.arange(m, dtype=I32)] * TOP_K)
    src = jnp.zeros((n_tiles * tm,), I32).at[dest].set(tok).reshape(n_tiles, 1, tm)
    tile_start = jnp.arange(n_tiles, dtype=I32) * tm
    tile_expert = jnp.minimum(jnp.sum((ends[None, :] <= tile_start[:, None]).astype(I32), axis=1), n_experts - 1)
    n_active = (ends[-1] // tm).astype(I32).reshape(1)
    pos = dest.reshape(TOP_K, m // combine_tile, combine_tile).transpose(1, 0, 2).reshape(
        m // combine_tile, 1, TOP_K * combine_tile).astype(I32)
    return src, tile_expert, n_active, pos


def _moe_ffn(rows, x, gn, mods, router_w, w1, w3, w2, layer, n_tok, final_gn):
    n_experts = w1.shape[1]
    v, e_idx, e_w = _norm_mod_call(rows, x, gn, mods, 3, 4, F32, router_w=router_w)
    tm = 512 if TOP_K * n_tok >= 8192 else 64
    combine_tile = rows.row_tile((256, 128, 64, 32, 16, 8))
    src, tile_expert, n_active, pos = _moe_dispatch(e_idx[:n_tok], n_experts, tm, combine_tile)
    hs = _expert_up(v, src, w1, w3, layer, tile_expert, n_active, tm)
    ys = _expert_down(hs, w2, layer, tile_expert, n_active, tm)
    return _moe_combine(rows, ys, pos, x, mods, 5, e_w, n_tok, final_gn)


def kernel(x, c, ctx, c_ctx, ada_w, ada_b, norm_mix, norm_ffn, w_in, conv_w, gla_gate_w, gla_gate_b,
           mlstm_gate_b, gla_out_norm, mlstm_out_norm, w_up_a, w_up_b, w_o, ffn_w1, ffn_w3, ffn_w2,
           router_w, moe_w1, moe_w3, moe_w2, norm_final):
    batch, seq, d = x.shape
    ctx_len = ctx.shape[1]
    depth = ada_w.shape[0]
    assert seq % GRID_W == 0 and seq % SCAN_CHUNK == 0 and ctx_len % SCAN_CHUNK == 0
    assert seq % (batch * ctx_len) == 0 and (batch * ctx_len) % GRID_W == 0
    rows = _Rows(batch, seq, ctx_len)

    h = jnp.concatenate([x.reshape(batch * seq, d), ctx.reshape(batch * ctx_len, d)], axis=0)

    cvec = jnp.zeros((SUBLANES * ((batch + 1 + SUBLANES - 1) // SUBLANES), d), F32)
    cvec = cvec.at[:batch].set(c).at[batch].set(c_ctx)
    mods_all = _ada_mods(cvec, ada_w, ada_b)
    mods_all = mods_all[:, :batch + 1].reshape(depth, batch + 1, 6, 1, d).transpose(0, 2, 1, 3, 4)

    qk_scale = jnp.ones((QK_COLS,), F32)
    qk_scale = qk_scale.at[:QA].set(GLA_DK ** -0.5).at[2 * QA + QB:].set(MLSTM_DQK ** -0.5).reshape(1, QK_COLS)
    c_v, c_ro = QK_COLS, QK_COLS + V_A + V_B
    c_sm = c_ro + V_A + V_B
    c_g = c_sm + 2 * GLA_RANK + N_GATE_B
    n_small = c_g - c_sm
    n_exp, d_ff = moe_w1.shape[1], moe_w1.shape[3]
    moe_w1_rows = moe_w1.reshape(-1, d_ff)
    moe_w3_rows = moe_w3.reshape(-1, d_ff)
    moe_w2_rows = moe_w2.reshape(-1, d)
    gate_tm, gate_tn = _mm_tiles(rows.m, d, 2 * d, None)
    gate_steps = (2 * d // gate_tn) * (rows.m // gate_tm)
    up_tm, up_tn = _swiglu_tiles(rows.m, ffn_w1.shape[2])
    up_steps = (ffn_w1.shape[2] // up_tn) * (rows.m // up_tm)
    w_main = w_in[:, :, :c_sm].astype(BF16)

    for l in range(depth):
        mods = mods_all[l]
        w_sm = jnp.zeros((1, d, LANES), F32).at[0, :, :n_small].set(w_in[l, :, c_sm:c_g])
        w_g = w_in[l, :, c_g:][None]

        u = _norm_mod_call(rows, h, norm_mix[l], mods, 0, 1, BF16)
        p_qk = _matmul(u, w_main, l, 0, QK_COLS, F32)
        v = _matmul(u, w_main, l, c_v, V_A + V_B, BF16)
        ro = _matmul(u, w_main, l, c_ro, V_A + V_B, F32)
        small = _matmul(u, w_sm, 0, 0, LANES, F32)
        if l % 2 == 0 and l + 1 < depth:
            cast = _SideCast(moe_w2_rows, (l // 2) * n_exp * d_ff, n_exp * d_ff, gate_steps)
            sg, w2_bf = _matmul(u, w_g, 0, 0, 2 * d, BF16, act="sigmoid", side=cast)
        elif l % 2 == 1:
            cast = _SideCast(moe_w3_rows, (l // 2) * n_exp * d, n_exp * d, gate_steps)
            sg, w3_bf = _matmul(u, w_g, 0, 0, 2 * d, BF16, act="sigmoid", side=cast)
        else:
            sg = _matmul(u, w_g, 0, 0, 2 * d, BF16, act="sigmoid")
        qk = _conv_silu(rows, p_qk, conv_w[l], qk_scale)
        gates_t = small[:, 2 * GLA_RANK:n_small].T
        oaf, oab = _gla_scan(rows, qk, v, small, gla_gate_w[l], gla_gate_b[l])
        obf, obb = _mlstm_scan(rows, qk, v, small, gates_t, mlstm_gate_b[l])
        ha, hb = _mix_prep(oaf, oab, obf, obb, ro, gla_out_norm[l], mlstm_out_norm[l])
        merged = _up_merge(ha, hb, w_up_a, w_up_b, l, sg)
        h = _matmul(merged, w_o, l, 0, d, F32, res=h, rows=rows, mods=mods, which=2)

        if l % 2 == 0:
            e = l // 2
            vv = _norm_mod_call(rows, h, norm_ffn[l], mods, 3, 4, BF16)
            if l + 1 < depth:
                cast = _SideCast(moe_w1_rows, e * n_exp * d, n_exp * d, up_steps)
                hid, w1_bf = _swiglu_up(vv, ffn_w1, ffn_w3, e, side=cast)
            else:
                hid = _swiglu_up(vv, ffn_w1, ffn_w3, e)
            h = _matmul(hid, ffn_w2, e, 0, d, F32, res=h, rows=rows, mods=mods, which=5)
        else:
            e = l // 2
            n_tok = rows.n_lat if l == depth - 1 else rows.m
            h = _moe_ffn(rows, h, norm_ffn[l], mods, router_w[e], w1_bf.reshape(1, n_exp, d, d_ff),
                         w3_bf.reshape(1, n_exp, d, d_ff), w2_bf.reshape(1, n_exp, d_ff, d), 0, n_tok,
                         final_gn=norm_final if l == depth - 1 else None)

    out = h if depth % 2 == 0 else _final_norm(rows, h, norm_final)
    return out.reshape(batch, seq, d)
```

```python
import functools
import math

import jax
import jax.numpy as jnp
from jax import lax
from jax.experimental import pallas as pl
from jax.experimental.pallas import tpu as pltpu

F32 = jnp.float32
BF16 = jnp.bfloat16
I32 = jnp.int32

GRID_W = 64
GLA_HEADS = 4
GLA_DK = 128
GLA_DV = 256
GLA_RANK = 16
GLA_GATE_NORMALIZER = 16.0
MLSTM_HEADS = 4
MLSTM_DQK = 128
MLSTM_DV = 256
GATE_SOFT_CAP = 15.0
N_GATE_B = 4 * MLSTM_HEADS
TOP_K = 2
EPS = 1e-6
QA = GLA_HEADS * GLA_DK
QB = MLSTM_HEADS * MLSTM_DQK
QK_COLS = 2 * QA + 2 * QB
V_A = GLA_HEADS * GLA_DV
V_B = MLSTM_HEADS * MLSTM_DV

LANES = 128
SUBLANES = 8
BF16_ROWS = 16
SCAN_CHUNK = 256
VMEM_LIMIT = 56 * 1024 * 1024
NEG_BIG = -1e30
LOG2_E = 1.4426950408889634
HIGHEST = lax.Precision.HIGHEST


def _cparams(*sem):
    return pltpu.CompilerParams(dimension_semantics=sem, vmem_limit_bytes=VMEM_LIMIT)


def _pick(dim, prefs):
    for p in prefs:
        if dim % p == 0:
            return p
    return dim


def _dot(a, b, precision=None):
    return jnp.dot(a, b, preferred_element_type=F32, precision=precision)


def _dot_nt(a, b, precision=None):
    return lax.dot_general(a, b, (((1,), (1,)), ((), ())), preferred_element_type=F32, precision=precision)


def _dot_tn(a, b):
    return lax.dot_general(a, b, (((0,), (0,)), ((), ())), preferred_element_type=F32)


def _silu(x):
    return x * jax.nn.sigmoid(x)


def _split3(x):
    hi = x.astype(BF16)
    rest = x - hi.astype(F32)
    mid = rest.astype(BF16)
    lo = (rest - mid.astype(F32)).astype(BF16)
    return hi, mid, lo


def _ada_kernel(c_ref, w_ref, b_ref, o_ref):
    o_ref[...] = _dot(_silu(c_ref[...]), w_ref[...], HIGHEST) + b_ref[...]


def _ada_mods(cvec, ada_w, ada_b):
    depth, d, n = ada_w.shape
    tn = _pick(n, (1024, 512, 256, 128))
    return pl.pallas_call(
        _ada_kernel,
        grid=(depth, n // tn),
        in_specs=[pl.BlockSpec(cvec.shape, lambda l, j: (0, 0)),
                  pl.BlockSpec((None, d, tn), lambda l, j: (l, 0, j)),
                  pl.BlockSpec((None, 1, tn), lambda l, j: (l, 0, j))],
        out_specs=pl.BlockSpec((None, cvec.shape[0], tn), lambda l, j: (l, 0, j)),
        out_shape=jax.ShapeDtypeStruct((depth, cvec.shape[0], n), F32),
        compiler_params=_cparams("parallel", "parallel"),
        name="ada_mods",
    )(cvec, ada_w, ada_b.reshape(depth, 1, n))


class _Rows:
    def __init__(self, batch, seq, ctx_len):
        self.batch, self.seq, self.ctx_len = batch, seq, ctx_len
        self.n_lat = batch * seq
        self.n_ctx = batch * ctx_len
        self.m = self.n_lat + self.n_ctx

    def row_tile(self, prefs):
        return _pick(math.gcd(self.seq, self.n_ctx), prefs)

    def gid(self, i, tm):
        r0 = i * tm
        return jnp.where(r0 < self.n_lat, r0 // self.seq, self.batch)


def _mod_spec(rows, which, tm, d):
    return pl.BlockSpec((None, None, 1, d), lambda i: (which, rows.gid(i, tm), 0, 0))


def _norm_mod(x, gn, sh, sc):
    y = x * lax.rsqrt(jnp.mean(x * x, axis=-1, keepdims=True) + EPS) * gn
    return y * (1.0 + sc) + sh


def _norm_mod_kernel(x_ref, gn_ref, sh_ref, sc_ref, o_ref):
    o_ref[...] = _norm_mod(x_ref[...], gn_ref[...], sh_ref[...], sc_ref[...]).astype(o_ref.dtype)


def _norm_mod_router_kernel(x_ref, gn_ref, sh_ref, sc_ref, rw_ref, o_ref, ei_ref, ew_ref, *, n_experts):
    u = _norm_mod(x_ref[...], gn_ref[...], sh_ref[...], sc_ref[...])
    o_ref[...] = u
    logits = _dot(u, rw_ref[...], HIGHEST)
    lane_i = lax.broadcasted_iota(I32, logits.shape, 1)
    lane = lane_i.astype(F32)
    lg = jnp.where(lane_i < n_experts, logits, -jnp.inf)
    m1 = jnp.max(lg, axis=-1, keepdims=True)
    i1 = jnp.min(jnp.where(lg == m1, lane, float(LANES)), axis=-1, keepdims=True)
    lg2 = jnp.where(lane == i1, -jnp.inf, lg)
    m2 = jnp.max(lg2, axis=-1, keepdims=True)
    i2 = jnp.min(jnp.where(lg2 == m2, lane, float(LANES)), axis=-1, keepdims=True)
    e = jnp.exp(m2 - m1)
    w1 = 1.0 / (1.0 + e)
    w2 = e / (1.0 + e)
    ei_ref[...] = jnp.where(lane_i == 0, i1, jnp.where(lane_i == 1, i2, 0.0)).astype(I32)
    ew_ref[...] = jnp.where(lane_i == 0, w1, jnp.where(lane_i == 1, w2, 0.0))


def _norm_mod_call(rows, x, gn, mods, which_sh, which_sc, out_dtype, router_w=None):
    m, d = x.shape
    tm = rows.row_tile((256, 128, 64, 32, 16, 8))
    in_specs = [pl.BlockSpec((tm, d), lambda i: (i, 0)),
                pl.BlockSpec((1, d), lambda i: (0, 0)),
                _mod_spec(rows, which_sh, tm, d),
                _mod_spec(rows, which_sc, tm, d)]
    row_spec = pl.BlockSpec((tm, d), lambda i: (i, 0))
    if router_w is None:
        return pl.pallas_call(
            _norm_mod_kernel, grid=(m // tm,), in_specs=in_specs, out_specs=row_spec,
            out_shape=jax.ShapeDtypeStruct((m, d), out_dtype),
            compiler_params=_cparams("parallel"), name="norm_mod",
        )(x, gn.reshape(1, d), mods, mods)
    n_experts = router_w.shape[1]
    rw = jnp.zeros((d, LANES), F32).at[:, :n_experts].set(router_w)
    lane_spec = pl.BlockSpec((tm, LANES), lambda i: (i, 0))
    return pl.pallas_call(
        functools.partial(_norm_mod_router_kernel, n_experts=n_experts),
        grid=(m // tm,),
        in_specs=in_specs + [pl.BlockSpec((d, LANES), lambda i: (0, 0))],
        out_specs=[row_spec, lane_spec, lane_spec],
        out_shape=[jax.ShapeDtypeStruct((m, d), F32), jax.ShapeDtypeStruct((m, LANES), I32),
                   jax.ShapeDtypeStruct((m, LANES), F32)],
        compiler_params=_cparams("parallel"), name="norm_mod_router",
    )(x, gn.reshape(1, d), mods, mods, rw)


def _final_norm_kernel(x_ref, gn_ref, o_ref):
    x = x_ref[...]
    o_ref[...] = x * lax.rsqrt(jnp.mean(x * x, axis=-1, keepdims=True) + EPS) * gn_ref[...]


def _final_norm(rows, x, gn):
    d = x.shape[1]
    tm = rows.row_tile((256, 128, 64, 32, 16, 8))
    return pl.pallas_call(
        _final_norm_kernel, grid=(rows.n_lat // tm,),
        in_specs=[pl.BlockSpec((tm, d), lambda i: (i, 0)), pl.BlockSpec((1, d), lambda i: (0, 0))],
        out_specs=pl.BlockSpec((tm, d), lambda i: (i, 0)),
        out_shape=jax.ShapeDtypeStruct((rows.n_lat, d), F32),
        compiler_params=_cparams("parallel"), name="final_norm",
    )(x, gn.reshape(1, d))


def _stash_weights(pairs):
    @pl.when(pl.program_id(1) == 0)
    def _():
        for w_ref, wbuf in pairs:
            wbuf[...] = w_ref[...].astype(BF16)


def _w_spec(kdim, tn, layer, col_blk0, w_rows=False):
    if w_rows:
        return pl.BlockSpec((None, tn, kdim), lambda j, i: (layer, col_blk0 + j, 0))
    return pl.BlockSpec((None, kdim, tn), lambda j, i: (layer, 0, col_blk0 + j))


class _SideCast:
    def __init__(self, src, row0, rows, steps):
        self.src = src
        self.cols = src.shape[1]
        self.n_blocks = next(nb for nb in range(min(steps, rows // BF16_ROWS), 0, -1)
                             if rows % nb == 0 and (rows // nb) % BF16_ROWS == 0)
        self.block_rows = rows // self.n_blocks
        assert row0 % self.block_rows == 0
        self.blk0 = row0 // self.block_rows
        self.out_shape = jax.ShapeDtypeStruct((rows, self.cols), BF16)

    def specs(self, n_inner):
        step = lambda j, i: jnp.minimum(j * n_inner + i, self.n_blocks - 1)
        blk = (self.block_rows, self.cols)
        return (pl.BlockSpec(blk, lambda j, i: (self.blk0 + step(j, i), 0)),
                pl.BlockSpec(blk, lambda j, i: (step(j, i), 0)))


def _mm_kernel(*refs, act, residual, side, w_rows):
    a_ref, w_ref = refs[:2]
    x_ref, g_ref = refs[2:4] if residual else (None, None)
    pos = 4 if residual else 2
    if side:
        side_in, o_ref, side_out, wbuf = refs[pos:pos + 4]
        side_out[...] = side_in[...].astype(BF16)
    else:
        o_ref, wbuf = refs[pos:pos + 2]
    _stash_weights([(w_ref, wbuf)])
    r = _dot_nt(a_ref[...], wbuf[...]) if w_rows else _dot(a_ref[...], wbuf[...])
    if act == "sigmoid":
        r = jax.nn.sigmoid(r)
    if residual:
        r = x_ref[...] + g_ref[...] * r
    o_ref[...] = r.astype(o_ref.dtype)


def _mm_tiles(m, kdim, n, rows):
    tn = _pick(n, (1024, 512, 256, 128) if kdim <= 2048 else (512, 256, 128))
    row_prefs = (1024, 512, 256, 128, 64, 32, 16, 8) if kdim <= 2048 else (512, 256, 128, 64, 32, 16, 8)
    tm = _pick(m, row_prefs) if rows is None else rows.row_tile(row_prefs)
    return tm, tn


def _matmul(a, w, layer, col0, n, out_dtype, *, act=None, res=None, rows=None, mods=None, which=None,
            side=None, w_rows=False):
    m, kdim = a.shape
    tm, tn = _mm_tiles(m, kdim, n, rows)
    assert col0 % tn == 0
    residual = res is not None
    in_specs = [pl.BlockSpec((tm, kdim), lambda j, i: (i, 0)), _w_spec(kdim, tn, layer, col0 // tn, w_rows)]
    args = [a, w]
    if residual:
        in_specs += [pl.BlockSpec((tm, tn), lambda j, i: (i, j)),
                     pl.BlockSpec((None, None, 1, tn), lambda j, i: (which, rows.gid(i, tm), 0, j))]
        args += [res, mods]
    out_specs = [pl.BlockSpec((tm, tn), lambda j, i: (i, j))]
    out_shape = [jax.ShapeDtypeStruct((m, n), out_dtype)]
    if side is not None:
        side_in, side_out = side.specs(m // tm)
        in_specs.append(side_in)
        args.append(side.src)
        out_specs.append(side_out)
        out_shape.append(side.out_shape)
    outs = pl.pallas_call(
        functools.partial(_mm_kernel, act=act, residual=residual, side=side is not None, w_rows=w_rows),
        grid=(n // tn, m // tm),
        in_specs=in_specs,
        out_specs=out_specs,
        out_shape=out_shape,
        scratch_shapes=[pltpu.VMEM((tn, kdim) if w_rows else (kdim, tn), BF16)],
        compiler_params=_cparams("parallel", "arbitrary"),
        name="matmul",
    )(*args)
    return outs[0] if side is None else outs


def _swiglu_kernel(*refs, side):
    if side:
        a_ref, w1_ref, w3_ref, side_in, o_ref, side_out, w1buf, w3buf = refs
        side_out[...] = side_in[...].astype(BF16)
    else:
        a_ref, w1_ref, w3_ref, o_ref, w1buf, w3buf = refs
    _stash_weights([(w1_ref, w1buf), (w3_ref, w3buf)])
    a = a_ref[...]
    o_ref[...] = (_silu(_dot(a, w1buf[...])) * _dot(a, w3buf[...])).astype(o_ref.dtype)


def _swiglu_tiles(m, f):
    return _pick(m, (1024, 512, 256, 128, 64, 32, 16, 8)), _pick(f, (512, 256, 128))


def _swiglu_up(a, w1, w3, layer, side=None):
    m, d = a.shape
    f = w1.shape[2]
    tm, tn = _swiglu_tiles(m, f)
    in_specs = [pl.BlockSpec((tm, d), lambda j, i: (i, 0)), _w_spec(d, tn, layer, 0), _w_spec(d, tn, layer, 0)]
    args = [a, w1, w3]
    out_specs = [pl.BlockSpec((tm, tn), lambda j, i: (i, j))]
    out_shape = [jax.ShapeDtypeStruct((m, f), BF16)]
    if side is not None:
        side_in, side_out = side.specs(m // tm)
        in_specs.append(side_in)
        args.append(side.src)
        out_specs.append(side_out)
        out_shape.append(side.out_shape)
    outs = pl.pallas_call(
        functools.partial(_swiglu_kernel, side=side is not None), grid=(f // tn, m // tm),
        in_specs=in_specs, out_specs=out_specs, out_shape=out_shape,
        scratch_shapes=[pltpu.VMEM((d, tn), BF16), pltpu.VMEM((d, tn), BF16)],
        compiler_params=_cparams("parallel", "arbitrary"), name="swiglu_up",
    )(*args)
    return outs[0] if side is None else outs


def _up_merge_kernel(ha_ref, hb_ref, wa_ref, wb_ref, sga_ref, sgb_ref, o_ref, wabuf, wbbuf):
    _stash_weights([(wa_ref, wabuf), (wb_ref, wbbuf)])
    up_a = _dot(ha_ref[...], wabuf[...])
    up_b = _dot(hb_ref[...], wbbuf[...])
    o_ref[...] = (sga_ref[...] * up_a + sgb_ref[...] * up_b).astype(o_ref.dtype)


def _up_merge(ha, hb, wa, wb, layer, sg):
    m, va = ha.shape
    vb = hb.shape[1]
    d = wa.shape[2]
    tm = _pick(m, (512, 256, 128, 64, 32, 16, 8))
    tn = _pick(d, (1024, 512, 256, 128))
    nj = d // tn
    return pl.pallas_call(
        _up_merge_kernel, grid=(nj, m // tm),
        in_specs=[pl.BlockSpec((tm, va), lambda j, i: (i, 0)),
                  pl.BlockSpec((tm, vb), lambda j, i: (i, 0)),
                  _w_spec(va, tn, layer, 0), _w_spec(vb, tn, layer, 0),
                  pl.BlockSpec((tm, tn), lambda j, i: (i, j)),
                  pl.BlockSpec((tm, tn), lambda j, i: (i, j + nj))],
        out_specs=pl.BlockSpec((tm, tn), lambda j, i: (i, j)),
        out_shape=jax.ShapeDtypeStruct((m, d), BF16),
        scratch_shapes=[pltpu.VMEM((va, tn), BF16), pltpu.VMEM((vb, tn), BF16)],
        compiler_params=_cparams("parallel", "arbitrary"), name="up_merge",
    )(ha, hb, wa, wb, sg, sg)


def _conv_taps(xs, w, dy, not_first, not_last):
    n = xs.shape[0]
    left = jnp.where(not_first, pltpu.roll(xs, 1, 0), 0.0)
    right = jnp.where(not_last, pltpu.roll(xs, n - 1, 0), 0.0)
    return w[3 * dy:3 * dy + 1] * left + w[3 * dy + 1:3 * dy + 2] * xs + w[3 * dy + 2:3 * dy + 3] * right


def _conv_kernel(up_ref, x_ref, dn_ref, w_ref, s_ref, o_ref, pad_ref, *, rows, rb, strip):
    r0 = pl.program_id(0) * rb
    w = w_ref[...]

    @pl.when(r0 < rows.n_lat)
    def _():
        tc = x_ref.shape[-1]
        at_start = lax.rem(r0, rows.seq) == 0
        at_end = lax.rem(r0 + rb, rows.seq) == 0
        pad_ref[pl.ds(0, GRID_W), :] = jnp.where(at_start, 0.0, up_ref[...])
        pad_ref[pl.ds(GRID_W + rb, GRID_W), :] = jnp.where(at_end, 0.0, dn_ref[...])
        pad_ref[pl.ds(GRID_W, rb), :] = x_ref[...]
        col = lax.broadcasted_iota(I32, (strip, tc), 0) % GRID_W
        not_first = col != 0
        not_last = col != GRID_W - 1
        for s in range(rb // strip):
            acc = jnp.zeros((strip, tc), F32)
            for dy in range(3):
                xs = pad_ref[pl.ds(s * strip + dy * GRID_W, strip), :]
                acc = acc + _conv_taps(xs, w, dy, not_first, not_last)
            o_ref[pl.ds(s * strip, strip), :] = _silu(acc) * s_ref[...]

    @pl.when(r0 >= rows.n_lat)
    def _():
        x = x_ref[...]
        pos = lax.broadcasted_iota(I32, x.shape, 0) % rows.ctx_len
        acc = _conv_taps(x, w, 1, pos != 0, pos != rows.ctx_len - 1)
        o_ref[...] = _silu(acc) * s_ref[...]


def _conv_silu(rows, p_qk, conv_w, scale):
    m, c = p_qk.shape
    tc = _pick(c, (512, 256, 128))
    rb = rows.n_ctx
    strip = _pick(rb, (512, 256, 128, 64))
    per = rb // GRID_W
    last = m // GRID_W - 1
    return pl.pallas_call(
        functools.partial(_conv_kernel, rows=rows, rb=rb, strip=strip),
        grid=(m // rb, c // tc),
        in_specs=[pl.BlockSpec((GRID_W, tc), lambda i, j: (jnp.maximum(i * per - 1, 0), j)),
                  pl.BlockSpec((rb, tc), lambda i, j: (i, j)),
                  pl.BlockSpec((GRID_W, tc), lambda i, j: (jnp.minimum((i + 1) * per, last), j)),
                  pl.BlockSpec((9, tc), lambda i, j: (0, j)),
                  pl.BlockSpec((1, tc), lambda i, j: (0, j))],
        out_specs=pl.BlockSpec((rb, tc), lambda i, j: (i, j)),
        out_shape=jax.ShapeDtypeStruct((m, c), F32),
        scratch_shapes=[pltpu.VMEM((rb + 2 * GRID_W, tc), F32)],
        compiler_params=_cparams("parallel", "parallel"), name="conv_silu",
    )(p_qk, p_qk, p_qk, conv_w.reshape(9, c), scale)


def _causal_masks(n):
    t = lax.broadcasted_iota(I32, (n, n), 0)
    s = lax.broadcasted_iota(I32, (n, n), 1)
    return t, s, (s <= t, s >= t)


def _block_ref_rows(x, blk, row):
    n, c = x.shape
    if blk >= SUBLANES:
        x3 = x.reshape(n // blk, blk, c)
        return jnp.broadcast_to(x3[:, row:row + 1, :], x3.shape).reshape(n, c)
    x3 = x.reshape(n // SUBLANES, SUBLANES, c)
    sub = lax.broadcasted_iota(I32, x3.shape, 1)
    out = jnp.zeros_like(x3)
    for g in range(SUBLANES // blk):
        r = g * blk + row
        out = jnp.where(sub // blk == g, jnp.broadcast_to(x3[:, r:r + 1, :], x3.shape), out)
    return out.reshape(n, c)


def _gla_direction(d, q, k, v, lr, wg, bg, s_ref, causal, lev_blk):
    n = q.shape[0]
    hn = n // 2
    lh, lm, _ = _split3(lr)
    wh, wm, _ = _split3(wg)
    z = _dot(jnp.concatenate([lh, lh, lm], axis=1), jnp.concatenate([wh, wm, wh], axis=0)) + bg
    la = jax.nn.log_sigmoid(z) * (LOG2_E / GLA_GATE_NORMALIZER)
    tri = causal.astype(BF16)
    c3 = _dot(tri, jnp.concatenate(_split3(la), axis=1))
    nk = la.shape[1]
    cum = c3[:, :nk] + c3[:, nk:2 * nk] + c3[:, 2 * nk:]
    total = cum[n - 1:n] if d == 0 else cum[0:1]
    state = s_ref[...]
    out = _dot((q * jnp.exp2(cum)).astype(BF16), state.astype(BF16))
    qb = q.astype(BF16)
    kb = k.astype(BF16)
    lo, hi = slice(0, hn), slice(hn, n)
    diag = [jnp.where(lev_blk == -1, _dot_nt(qb[h], kb[h]).astype(BF16), jnp.zeros((), BF16)) for h in (lo, hi)]
    top = n.bit_length() - 2
    for l in range(top + 1):
        half = 1 << l
        ref = _block_ref_rows(cum, 2 * half, half - 1 if d == 0 else half)
        e = jnp.exp2(-jnp.abs(cum - ref)).astype(BF16)
        qe = qb * e
        ke = kb * e
        if l < top:
            diag = [jnp.where(lev_blk == l, _dot_nt(qe[h], ke[h]).astype(BF16), sc)
                    for h, sc in zip((lo, hi), diag)]
        else:
            cross = (_dot_nt(qe[hi], ke[lo]) if d == 0 else _dot_nt(qe[lo], ke[hi])).astype(BF16)
    zero = jnp.zeros((hn, hn), BF16)
    upper, lower = ([diag[0], zero], [cross, diag[1]]) if d == 0 else ([diag[0], cross], [zero, diag[1]])
    scores = jnp.concatenate([jnp.concatenate(upper, axis=1), jnp.concatenate(lower, axis=1)], axis=0)
    out = out + _dot(scores, v)
    k_out = (k * jnp.exp2(total - cum)).astype(BF16)
    dk = state.shape[0]
    et = jnp.broadcast_to(jnp.exp2(total), (dk, dk)).T
    scale = jnp.concatenate([et] * (state.shape[1] // dk), axis=1)
    s_ref[...] = scale * state + _dot_tn(k_out, v)
    return out


def _gla_kernel(qf, kf, vf, smf, qb, kb, vb, smb, wg_ref, bg_ref, of_ref, ob_ref, sf_ref, sb_ref):
    @pl.when(pl.program_id(1) == 0)
    def _():
        sf_ref[...] = jnp.zeros_like(sf_ref)
        sb_ref[...] = jnp.zeros_like(sb_ref)

    n = qf.shape[0]
    _, _, causal = _causal_masks(n)
    t_idx, s_idx, causal_blk = _causal_masks(n // 2)
    lev = 31 - lax.clz(t_idx ^ s_idx)
    dirs = ((qf, kf, vf, smf, of_ref, sf_ref), (qb, kb, vb, smb, ob_ref, sb_ref))
    for d, (q, k, v, sm, o_ref, s_ref) in enumerate(dirs):
        lr = sm[:, d * GLA_RANK:(d + 1) * GLA_RANK]
        lev_blk = jnp.where(causal_blk[d], lev, -2).astype(BF16)
        for hd in range(GLA_HEADS):
            ks = slice(hd * GLA_DK, (hd + 1) * GLA_DK)
            vs = slice(hd * GLA_DV, (hd + 1) * GLA_DV)
            o_ref[:, vs] = _gla_direction(d, q[:, ks], k[:, ks], v[:, vs], lr, wg_ref[d, :, ks], bg_ref[d, :, ks],
                                          s_ref.at[hd], causal[d], lev_blk)


def _soft_cap(z):
    return GATE_SOFT_CAP * jnp.tanh(z * (1.0 / GATE_SOFT_CAP))


def _mlstm_direction(d, head, q, k, v, g_col, g_row, c_ref, m_ref, causal):
    n = q.shape[0]
    tri = causal.astype(BF16)
    i_idx = 2 * MLSTM_HEADS * d + head
    f_idx = i_idx + MLSTM_HEADS
    sel_rows = lax.broadcasted_iota(I32, (3 * N_GATE_B, LANES), 0) % N_GATE_B
    rep = lambda a, idx: _dot(jnp.concatenate(_split3(a), axis=1), (sel_rows == idx).astype(BF16))
    i_rep = rep(g_col, i_idx)
    f_rep = rep(jax.nn.log_sigmoid(g_col), f_idx)
    c3 = _dot(tri, jnp.concatenate(_split3(f_rep), axis=1))
    cum = c3[:, :LANES] + c3[:, LANES:2 * LANES] + c3[:, 2 * LANES:]
    r3 = _dot_nt(jnp.concatenate(_split3(jax.nn.log_sigmoid(g_row)), axis=0), tri)
    cum_r_all = r3[:N_GATE_B] + r3[N_GATE_B:2 * N_GATE_B] + r3[2 * N_GATE_B:]
    sub = lax.broadcasted_iota(I32, (N_GATE_B, 1), 0)
    pick_r = lambda a, idx: jnp.sum(jnp.where(sub == idx, a, 0.0), axis=0, keepdims=True)
    i_r, cum_r = pick_r(g_row, i_idx), pick_r(cum_r_all, f_idx)
    wide = lambda a, reps: jnp.concatenate([a] * reps, axis=1)
    nt = n // LANES
    total = cum[n - 1:n] if d == 0 else cum[0:1]
    m_prev = m_ref[...]
    dlog = jnp.where(causal, wide(cum, nt) - cum_r + i_r, NEG_BIG)
    inter_log = cum + m_prev
    row_max = dlog[:, :LANES]
    for t in range(1, nt):
        row_max = jnp.maximum(row_max, dlog[:, t * LANES:(t + 1) * LANES])
    m_t = jnp.maximum(inter_log, jnp.max(row_max, axis=1, keepdims=True))
    w_inter = jnp.exp(inter_log - m_t)
    qb16 = q.astype(BF16)
    s = _dot_nt(qb16, k.astype(BF16)) * jnp.exp(dlog - wide(m_t, nt))
    dv = v.shape[1]
    v_ext = jnp.concatenate([v, jnp.ones((n, LANES), BF16)], axis=1)
    state = c_ref[...]
    acc = wide(w_inter, dv // LANES + 1) * _dot(qb16, state.astype(BF16)) + _dot(s.astype(BF16), v_ext)
    bound = jnp.maximum(jnp.abs(acc[:, dv:]), jnp.exp(-m_t))
    out = acc[:, :dv] / wide(bound, dv // LANES)
    g = total - cum + i_rep
    m_new = jnp.maximum(total + m_prev, jnp.max(g, axis=0, keepdims=True))
    w_c = jnp.exp(total + m_prev - m_new)
    w_k = jnp.exp(g - m_new)
    c_ref[...] = wide(w_c, dv // LANES + 1) * state + _dot_tn((k * w_k).astype(BF16), v_ext)
    m_ref[...] = m_new
    return out


def _mlstm_kernel(qf, kf, vf, gcf, grf, qb, kb, vb, gcb, grb, brow_ref, bcol_ref,
                  of_ref, ob_ref, cf_ref, cb_ref, mf_ref, mb_ref):
    @pl.when(pl.program_id(1) == 0)
    def _():
        cf_ref[...] = jnp.zeros_like(cf_ref)
        cb_ref[...] = jnp.zeros_like(cb_ref)
        mf_ref[...] = jnp.zeros_like(mf_ref)
        mb_ref[...] = jnp.zeros_like(mb_ref)

    n = qf.shape[0]
    _, _, causal = _causal_masks(n)
    g0 = 2 * GLA_RANK
    dirs = ((qf, kf, vf, gcf, grf, of_ref, cf_ref, mf_ref), (qb, kb, vb, gcb, grb, ob_ref, cb_ref, mb_ref))
    for d, (q, k, v, gc, gr, o_ref, c_ref, m_ref) in enumerate(dirs):
        g_col = _soft_cap(gc[:, g0:g0 + N_GATE_B] + brow_ref[...])
        g_row = _soft_cap(gr[...] + bcol_ref[...])
        for hd in range(MLSTM_HEADS):
            qs = slice(hd * MLSTM_DQK, (hd + 1) * MLSTM_DQK)
            vs = slice(hd * MLSTM_DV, (hd + 1) * MLSTM_DV)
            o_ref[:, vs] = _mlstm_direction(d, hd, q[:, qs], k[:, qs], v[:, vs], g_col, g_row,
                                            c_ref.at[hd], m_ref.at[hd], causal[d])


def _scan_row_maps(rows, chunk):
    lat_chunks = rows.seq // chunk
    ctx_chunks = rows.ctx_len // chunk
    ctx0 = rows.n_lat // chunk

    def fwd(b, s):
        return jnp.where(s < ctx_chunks, ctx0 + b * ctx_chunks + s, b * lat_chunks + (s - ctx_chunks))

    def bwd(b, s):
        return jnp.where(s < ctx_chunks, ctx0 + b * ctx_chunks + (ctx_chunks - 1 - s),
                         b * lat_chunks + (lat_chunks - 1 - (s - ctx_chunks)))

    return fwd, bwd, ctx_chunks + lat_chunks


def _gla_scan(rows, qk, v, small, gate_w, gate_b):
    m = qk.shape[0]
    n = SCAN_CHUNK
    fwd, bwd, steps = _scan_row_maps(rows, n)

    def specs(rmap):
        return [pl.BlockSpec((n, QA), lambda b, s: (rmap(b, s), 0)),
                pl.BlockSpec((n, QA), lambda b, s: (rmap(b, s), 1)),
                pl.BlockSpec((n, V_A), lambda b, s: (rmap(b, s), 0)),
                pl.BlockSpec((n, LANES), lambda b, s: (rmap(b, s), 0))]

    out_spec = lambda rmap: pl.BlockSpec((n, V_A), lambda b, s: (rmap(b, s), 0))
    out_sds = jax.ShapeDtypeStruct((m, V_A), F32)
    state = pltpu.VMEM((GLA_HEADS, GLA_DK, GLA_DV), F32)
    return pl.pallas_call(
        _gla_kernel,
        grid=(rows.batch, steps),
        in_specs=specs(fwd) + specs(bwd) + [
            pl.BlockSpec((2, GLA_RANK, QA), lambda b, s: (0, 0, 0)),
            pl.BlockSpec((2, 1, QA), lambda b, s: (0, 0, 0))],
        out_specs=[out_spec(fwd), out_spec(bwd)],
        out_shape=[out_sds, out_sds],
        scratch_shapes=[state, state],
        compiler_params=_cparams("parallel", "arbitrary"), name="gla_scan",
    )(qk, qk, v, small, qk, qk, v, small, gate_w, gate_b.reshape(2, 1, QA))


def _mlstm_scan(rows, qk, v, small, gates_t, gate_b):
    m = qk.shape[0]
    n = SCAN_CHUNK
    fwd, bwd, steps = _scan_row_maps(rows, n)
    q0 = 2 * QA // QB
    v0 = V_A // V_B

    def specs(rmap):
        return [pl.BlockSpec((n, QB), lambda b, s: (rmap(b, s), q0)),
                pl.BlockSpec((n, QB), lambda b, s: (rmap(b, s), q0 + 1)),
                pl.BlockSpec((n, V_B), lambda b, s: (rmap(b, s), v0)),
                pl.BlockSpec((n, LANES), lambda b, s: (rmap(b, s), 0)),
                pl.BlockSpec((N_GATE_B, n), lambda b, s: (0, rmap(b, s)))]

    out_spec = lambda rmap: pl.BlockSpec((n, V_B), lambda b, s: (rmap(b, s), 0))
    out_sds = jax.ShapeDtypeStruct((m, V_B), F32)
    state = pltpu.VMEM((MLSTM_HEADS, MLSTM_DQK, MLSTM_DV + LANES), F32)
    stab = pltpu.VMEM((MLSTM_HEADS, 1, LANES), F32)
    return pl.pallas_call(
        _mlstm_kernel,
        grid=(rows.batch, steps),
        in_specs=specs(fwd) + specs(bwd) + [
            pl.BlockSpec((1, N_GATE_B), lambda b, s: (0, 0)),
            pl.BlockSpec((N_GATE_B, 1), lambda b, s: (0, 0))],
        out_specs=[out_spec(fwd), out_spec(bwd)],
        out_shape=[out_sds, out_sds],
        scratch_shapes=[state, state, stab, stab],
        compiler_params=_cparams("parallel", "arbitrary"), name="mlstm_scan",
    )(qk, qk, v, small, gates_t, qk, qk, v, small, gates_t,
      gate_b.reshape(1, N_GATE_B), gate_b.reshape(N_GATE_B, 1))


def _mix_prep_kernel(oaf, oab, obf, obb, ro_ref, gna_ref, gnb_ref, ha_ref, hb_ref):
    branches = ((oaf, oab, gna_ref, ha_ref, 0, GLA_HEADS, GLA_DV, _silu),
                (obf, obb, gnb_ref, hb_ref, V_A, MLSTM_HEADS, MLSTM_DV, jax.nn.sigmoid))
    for of, ob, gn_ref, h_ref, off, heads, dv, gate_fn in branches:
        for h in range(heads):
            sl = slice(h * dv, (h + 1) * dv)
            o = of[:, sl] + ob[:, sl]
            y = o * lax.rsqrt(jnp.mean(o * o, axis=-1, keepdims=True) + EPS) * gn_ref[:, sl]
            gate = gate_fn(ro_ref[:, off + h * dv:off + (h + 1) * dv])
            h_ref[:, sl] = (y * gate).astype(h_ref.dtype)


def _mix_prep(oaf, oab, obf, obb, ro, gna, gnb):
    m = ro.shape[0]
    tm = _pick(m, (256, 128, 64, 32, 16, 8))
    row = lambda c: pl.BlockSpec((tm, c), lambda i: (i, 0))
    vec = lambda c: pl.BlockSpec((1, c), lambda i: (0, 0))
    return pl.pallas_call(
        _mix_prep_kernel, grid=(m // tm,),
        in_specs=[row(V_A), row(V_A), row(V_B), row(V_B), row(V_A + V_B), vec(V_A), vec(V_B)],
        out_specs=[row(V_A), row(V_B)],
        out_shape=[jax.ShapeDtypeStruct((m, V_A), BF16), jax.ShapeDtypeStruct((m, V_B), BF16)],
        compiler_params=_cparams("parallel"), name="mix_prep",
    )(oaf, oab, obf, obb, ro, gna.reshape(1, V_A), gnb.reshape(1, V_B))


def _expert_up_kernel(te_ref, na_ref, src0_ref, srcn_ref, v_hbm, w1_ref, w3_ref, o_ref, xbuf, abuf, sem,
                      *, tm, issue_steps):
    del te_ref
    i = pl.program_id(0)
    j = pl.program_id(1)
    slot = lax.rem(i, 2)
    per = tm // issue_steps

    def row_copy(src_ref, grp, t, s):
        return pltpu.make_async_copy(v_hbm.at[pl.ds(src_ref[0, grp * per + t], 1)],
                                     xbuf.at[s, grp, pl.ds(t, 1)], sem.at[s])

    @pl.when(jnp.logical_and(i == 0, j == 0))
    def _():
        def start(grp, carry):
            for t in range(per):
                row_copy(src0_ref, grp, t, 0).start()
            return carry

        lax.fori_loop(0, issue_steps, start, 0)

    @pl.when(j == 0)
    def _():
        def wait(r, carry):
            row_copy(src0_ref, 0, 0, slot).wait()
            return carry

        lax.fori_loop(0, tm, wait, 0, unroll=32)
        abuf[...] = xbuf[slot].reshape(tm, abuf.shape[1]).astype(BF16)

    @pl.when(jnp.logical_and(i + 1 < pl.num_programs(0), j < issue_steps))
    def _():
        for t in range(per):
            row_copy(srcn_ref, j, t, 1 - slot).start()

    @pl.when(i < na_ref[0])
    def _():
        a = abuf[...]
        o_ref[...] = (_silu(_dot(a, w1_ref[...])) * _dot(a, w3_ref[...])).astype(o_ref.dtype)

    @pl.when(i >= na_ref[0])
    def _():
        o_ref[...] = jnp.zeros_like(o_ref)


def _expert_up(v, src, w1, w3, layer, tile_expert, n_active, tm):
    n_tiles = src.shape[0]
    d = v.shape[1]
    f = w1.shape[3]
    tn = _pick(f, (1408, 512, 256, 128))
    nj = f // tn
    issue_steps = max(s for s in (1, 2, 4, 8) if s <= nj)
    wmap = lambda i, j, te, na: (layer, te[i], 0, jnp.where(i < na[0], j, 0))
    return pl.pallas_call(
        functools.partial(_expert_up_kernel, tm=tm, issue_steps=issue_steps),
        grid_spec=pltpu.PrefetchScalarGridSpec(
            num_scalar_prefetch=2, grid=(n_tiles, nj),
            in_specs=[pl.BlockSpec((None, 1, tm), lambda i, j, te, na: (0, 0, 0), memory_space=pltpu.SMEM),
                      pl.BlockSpec((None, 1, tm), lambda i, j, te, na: (jnp.minimum(i + 1, n_tiles - 1), 0, 0),
                                   memory_space=pltpu.SMEM),
                      pl.BlockSpec(memory_space=pl.ANY),
                      pl.BlockSpec((None, None, d, tn), wmap),
                      pl.BlockSpec((None, None, d, tn), wmap)],
            out_specs=pl.BlockSpec((tm, tn), lambda i, j, te, na: (i, j)),
            scratch_shapes=[pltpu.VMEM((2, issue_steps, tm // issue_steps, d), F32), pltpu.VMEM((tm, d), BF16),
                            pltpu.SemaphoreType.DMA((2,))]),
        out_shape=jax.ShapeDtypeStruct((n_tiles * tm, f), BF16),
        compiler_params=_cparams("arbitrary", "arbitrary"), name="expert_up",
    )(tile_expert, n_active, src, src, v, w1, w3)


def _expert_down_kernel(te_ref, na_ref, a_ref, w_ref, o_ref):
    del te_ref
    i = pl.program_id(0)

    @pl.when(i < na_ref[0])
    def _():
        o_ref[...] = _dot(a_ref[...], w_ref[...])

    @pl.when(i >= na_ref[0])
    def _():
        o_ref[...] = jnp.zeros_like(o_ref)


def _expert_down(hs, w2, layer, tile_expert, n_active, tm):
    p, f = hs.shape
    d = w2.shape[3]
    tn = _pick(d, (1024, 512, 256, 128))
    return pl.pallas_call(
        _expert_down_kernel,
        grid_spec=pltpu.PrefetchScalarGridSpec(
            num_scalar_prefetch=2, grid=(p // tm, d // tn),
            in_specs=[pl.BlockSpec((tm, f), lambda i, j, te, na: (jnp.where(i < na[0], i, 0), 0)),
                      pl.BlockSpec((None, None, f, tn),
                                   lambda i, j, te, na: (layer, te[i], 0, jnp.where(i < na[0], j, 0)))],
            out_specs=pl.BlockSpec((tm, tn), lambda i, j, te, na: (i, j))),
        out_shape=jax.ShapeDtypeStruct((p, d), F32),
        compiler_params=_cparams("parallel", "parallel"), name="expert_down",
    )(tile_expert, n_active, hs, w2)


def _moe_combine_kernel(pos_ref, posn_ref, ys_hbm, x_ref, g_ref, w_ref, fn_ref, o_ref, buf, sem,
                        *, tile, final_norm):
    i = pl.program_id(0)
    slot = lax.rem(i, 2)

    def row_copy(p_ref, k, r, s):
        return pltpu.make_async_copy(ys_hbm.at[pl.ds(p_ref[0, k * tile + r], 1)],
                                     buf.at[s, k, pl.ds(r, 1)], sem.at[s])

    def start_tile(p_ref, s):
        for k in range(TOP_K):
            def start(r, carry):
                row_copy(p_ref, k, r, s).start()
                return carry

            lax.fori_loop(0, tile, start, 0, unroll=8)

    @pl.when(i == 0)
    def _():
        start_tile(pos_ref, 0)

    @pl.when(i + 1 < pl.num_programs(0))
    def _():
        start_tile(posn_ref, 1 - slot)

    def wait(r, carry):
        row_copy(pos_ref, 0, 0, slot).wait()
        return carry

    lax.fori_loop(0, TOP_K * tile, wait, 0, unroll=32)
    w = w_ref[...]
    y = w[:, 0:1] * buf[slot, 0] + w[:, 1:2] * buf[slot, 1]
    r = x_ref[...] + g_ref[...] * y
    if final_norm:
        r = r * lax.rsqrt(jnp.mean(r * r, axis=-1, keepdims=True) + EPS) * fn_ref[...]
    o_ref[...] = r


def _moe_combine(rows, ys, pos, x, mods, which, ew, m, final_gn):
    d = x.shape[1]
    tile = rows.row_tile((256, 128, 64, 32, 16, 8))
    n = m // tile
    pos_spec = lambda imap: pl.BlockSpec((None, 1, TOP_K * tile), imap, memory_space=pltpu.SMEM)
    final_norm = final_gn is not None
    fn = final_gn.reshape(1, d) if final_norm else jnp.ones((1, d), F32)
    return pl.pallas_call(
        functools.partial(_moe_combine_kernel, tile=tile, final_norm=final_norm),
        grid=(n,),
        in_specs=[pos_spec(lambda i: (i, 0, 0)),
                  pos_spec(lambda i: (jnp.minimum(i + 1, n - 1), 0, 0)),
                  pl.BlockSpec(memory_space=pl.ANY),
                  pl.BlockSpec((tile, d), lambda i: (i, 0)),
                  _mod_spec(rows, which, tile, d),
                  pl.BlockSpec((tile, LANES), lambda i: (i, 0)),
                  pl.BlockSpec((1, d), lambda i: (0, 0))],
        out_specs=pl.BlockSpec((tile, d), lambda i: (i, 0)),
        out_shape=jax.ShapeDtypeStruct((m, d), F32),
        scratch_shapes=[pltpu.VMEM((2, TOP_K, tile, d), F32), pltpu.SemaphoreType.DMA((2,))],
        compiler_params=_cparams("arbitrary"), name="moe_combine",
    )(pos, pos, ys, x, mods, ew, fn)


def _moe_dispatch(e_idx, n_experts, tm, combine_tile):
    m = e_idx.shape[0]
    ex = jnp.concatenate([e_idx[:, 0], e_idx[:, 1]])
    onehot = (ex[:, None] == jnp.arange(n_experts, dtype=I32)[None, :]).astype(I32)
    rank = jnp.sum((jnp.cumsum(onehot, axis=0) - onehot) * onehot, axis=1)
    counts = jnp.sum(onehot, axis=0)
    padded = (counts + tm - 1) // tm * tm
    ends = jnp.cumsum(padded)
    dest = (ends - padded)[ex] + rank
    n_tiles = (TOP_K * m + n_experts * (tm - 1)) // tm
    tok = jnp.concatenate([jnp.arange(m, dtype=I32)] * TOP_K)
    src = jnp.zeros((n_tiles * tm,), I32).at[dest].set(tok).reshape(n_tiles, 1, tm)
    tile_start = jnp.arange(n_tiles, dtype=I32) * tm
    tile_expert = jnp.minimum(jnp.sum((ends[None, :] <= tile_start[:, None]).astype(I32), axis=1), n_experts - 1)
    n_active = (ends[-1] // tm).astype(I32).reshape(1)
    pos = dest.reshape(TOP_K, m // combine_tile, combine_tile).transpose(1, 0, 2).reshape(
        m // combine_tile, 1, TOP_K * combine_tile).astype(I32)
    return src, tile_expert, n_active, pos


def _moe_ffn(rows, x, gn, mods, router_w, w1, w3, w2, layer, n_tok, final_gn):
    n_experts = w1.shape[1]
    v, e_idx, e_w = _norm_mod_call(rows, x, gn, mods, 3, 4, F32, router_w=router_w)
    tm = 512 if TOP_K * n_tok >= 8192 else 64
    combine_tile = rows.row_tile((256, 128, 64, 32, 16, 8))
    src, tile_expert, n_active, pos = _moe_dispatch(e_idx[:n_tok], n_experts, tm, combine_tile)
    hs = _expert_up(v, src, w1, w3, layer, tile_expert, n_active, tm)
    ys = _expert_down(hs, w2, layer, tile_expert, n_active, tm)
    return _moe_combine(rows, ys, pos, x, mods, 5, e_w, n_tok, final_gn)


def kernel(x, c, ctx, c_ctx, ada_w, ada_b, norm_mix, norm_ffn, w_in, conv_w, gla_gate_w, gla_gate_b,
           mlstm_gate_b, gla_out_norm, mlstm_out_norm, w_up_a, w_up_b, w_o, ffn_w1, ffn_w3, ffn_w2,
           router_w, moe_w1, moe_w3, moe_w2, norm_final):
    batch, seq, d = x.shape
    ctx_len = ctx.shape[1]
    depth = ada_w.shape[0]
    assert seq % GRID_W == 0 and seq % SCAN_CHUNK == 0 and ctx_len % SCAN_CHUNK == 0
    assert seq % (batch * ctx_len) == 0 and (batch * ctx_len) % GRID_W == 0
    rows = _Rows(batch, seq, ctx_len)

    h = jnp.concatenate([x.reshape(batch * seq, d), ctx.reshape(batch * ctx_len, d)], axis=0)

    cvec = jnp.zeros((SUBLANES * ((batch + 1 + SUBLANES - 1) // SUBLANES), d), F32)
    cvec = cvec.at[:batch].set(c).at[batch].set(c_ctx)
    mods_all = _ada_mods(cvec, ada_w, ada_b)
    mods_all = mods_all[:, :batch + 1].reshape(depth, batch + 1, 6, 1, d).transpose(0, 2, 1, 3, 4)

    qk_scale = jnp.ones((QK_COLS,), F32)
    qk_scale = qk_scale.at[:QA].set(GLA_DK ** -0.5).at[2 * QA + QB:].set(MLSTM_DQK ** -0.5).reshape(1, QK_COLS)
    c_v, c_ro = QK_COLS, QK_COLS + V_A + V_B
    c_sm = c_ro + V_A + V_B
    c_g = c_sm + 2 * GLA_RANK + N_GATE_B
    n_small = c_g - c_sm
    n_exp, d_ff = moe_w1.shape[1], moe_w1.shape[3]
    moe_w1_rows = moe_w1.reshape(-1, d_ff)
    moe_w3_rows = moe_w3.reshape(-1, d_ff)
    moe_w2_rows = moe_w2.reshape(-1, d)
    gate_tm, gate_tn = _mm_tiles(rows.m, d, 2 * d, None)
    gate_steps = (2 * d // gate_tn) * (rows.m // gate_tm)
    up_tm, up_tn = _swiglu_tiles(rows.m, ffn_w1.shape[2])
    up_steps = (ffn_w1.shape[2] // up_tn) * (rows.m // up_tm)
    w_in_t = jnp.swapaxes(w_in, 1, 2)

    for l in range(depth):
        mods = mods_all[l]
        w_sm = jnp.zeros((1, LANES, d), F32).at[0, :n_small].set(w_in_t[l, c_sm:c_g])
        w_g = w_in_t[l, c_g:][None]

        u = _norm_mod_call(rows, h, norm_mix[l], mods, 0, 1, BF16)
        p_qk = _matmul(u, w_in_t, l, 0, QK_COLS, F32, w_rows=True)
        v = _matmul(u, w_in_t, l, c_v, V_A + V_B, BF16, w_rows=True)
        ro = _matmul(u, w_in_t, l, c_ro, V_A + V_B, F32, w_rows=True)
        small = _matmul(u, w_sm, 0, 0, LANES, F32, w_rows=True)
        if l % 2 == 0 and l + 1 < depth:
            cast = _SideCast(moe_w2_rows, (l // 2) * n_exp * d_ff, n_exp * d_ff, gate_steps)
            sg, w2_bf = _matmul(u, w_g, 0, 0, 2 * d, BF16, act="sigmoid", side=cast, w_rows=True)
        elif l % 2 == 1:
            cast = _SideCast(moe_w3_rows, (l // 2) * n_exp * d, n_exp * d, gate_steps)
            sg, w3_bf = _matmul(u, w_g, 0, 0, 2 * d, BF16, act="sigmoid", side=cast, w_rows=True)
        else:
            sg = _matmul(u, w_g, 0, 0, 2 * d, BF16, act="sigmoid", w_rows=True)
        qk = _conv_silu(rows, p_qk, conv_w[l], qk_scale)
        gates_t = small[:, 2 * GLA_RANK:n_small].T
        oaf, oab = _gla_scan(rows, qk, v, small, gla_gate_w[l], gla_gate_b[l])
        obf, obb = _mlstm_scan(rows, qk, v, small, gates_t, mlstm_gate_b[l])
        ha, hb = _mix_prep(oaf, oab, obf, obb, ro, gla_out_norm[l], mlstm_out_norm[l])
        merged = _up_merge(ha, hb, w_up_a, w_up_b, l, sg)
        h = _matmul(merged, w_o, l, 0, d, F32, res=h, rows=rows, mods=mods, which=2)

        if l % 2 == 0:
            e = l // 2
            vv = _norm_mod_call(rows, h, norm_ffn[l], mods, 3, 4, BF16)
            if l + 1 < depth:
                cast = _SideCast(moe_w1_rows, e * n_exp * d, n_exp * d, up_steps)
                hid, w1_bf = _swiglu_up(vv, ffn_w1, ffn_w3, e, side=cast)
            else:
                hid = _swiglu_up(vv, ffn_w1, ffn_w3, e)
            h = _matmul(hid, ffn_w2, e, 0, d, F32, res=h, rows=rows, mods=mods, which=5)
        else:
            e = l // 2
            n_tok = rows.n_lat if l == depth - 1 else rows.m
            h = _moe_ffn(rows, h, norm_ffn[l], mods, router_w[e], w1_bf.reshape(1, n_exp, d, d_ff),
                         w3_bf.reshape(1, n_exp, d, d_ff), w2_bf.reshape(1, n_exp, d_ff, d), 0, n_tok,
                         final_gn=norm_final if l == depth - 1 else None)

    out = h if depth % 2 == 0 else _final_norm(rows, h, norm_final)
    return out.reshape(batch, seq, d)
```

```python
import functools
import math

import jax
import jax.numpy as jnp
from jax import lax
from jax.experimental import pallas as pl
from jax.experimental.pallas import tpu as pltpu

F32 = jnp.float32
BF16 = jnp.bfloat16
I32 = jnp.int32

GRID_W = 64
GLA_HEADS = 4
GLA_DK = 128
GLA_DV = 256
GLA_RANK = 16
GLA_GATE_NORMALIZER = 16.0
MLSTM_HEADS = 4
MLSTM_DQK = 128
MLSTM_DV = 256
GATE_SOFT_CAP = 15.0
N_GATE_B = 4 * MLSTM_HEADS
TOP_K = 2
EPS = 1e-6
QA = GLA_HEADS * GLA_DK
QB = MLSTM_HEADS * MLSTM_DQK
QK_COLS = 2 * QA + 2 * QB
V_A = GLA_HEADS * GLA_DV
V_B = MLSTM_HEADS * MLSTM_DV

LANES = 128
SUBLANES = 8
BF16_ROWS = 16
COPY_GROUP = 8
SCAN_CHUNK = 256
VMEM_LIMIT = 56 * 1024 * 1024
NEG_BIG = -1e30
LOG2_E = 1.4426950408889634
HIGHEST = lax.Precision.HIGHEST


def _cparams(*sem):
    return pltpu.CompilerParams(dimension_semantics=sem, vmem_limit_bytes=VMEM_LIMIT)


def _pick(dim, prefs):
    for p in prefs:
        if dim % p == 0:
            return p
    return dim


def _dot(a, b, precision=None):
    return jnp.dot(a, b, preferred_element_type=F32, precision=precision)


def _dot_nt(a, b, precision=None):
    return lax.dot_general(a, b, (((1,), (1,)), ((), ())), preferred_element_type=F32, precision=precision)


def _dot_tn(a, b):
    return lax.dot_general(a, b, (((0,), (0,)), ((), ())), preferred_element_type=F32)


def _silu(x):
    return x * jax.nn.sigmoid(x)


def _split3(x):
    hi = x.astype(BF16)
    rest = x - hi.astype(F32)
    mid = rest.astype(BF16)
    lo = (rest - mid.astype(F32)).astype(BF16)
    return hi, mid, lo


def _ada_kernel(c_ref, w_ref, b_ref, o_ref):
    o_ref[...] = _dot(_silu(c_ref[...]), w_ref[...], HIGHEST) + b_ref[...]


def _ada_mods(cvec, ada_w, ada_b):
    depth, d, n = ada_w.shape
    tn = _pick(n, (1024, 512, 256, 128))
    return pl.pallas_call(
        _ada_kernel,
        grid=(depth, n // tn),
        in_specs=[pl.BlockSpec(cvec.shape, lambda l, j: (0, 0)),
                  pl.BlockSpec((None, d, tn), lambda l, j: (l, 0, j)),
                  pl.BlockSpec((None, 1, tn), lambda l, j: (l, 0, j))],
        out_specs=pl.BlockSpec((None, cvec.shape[0], tn), lambda l, j: (l, 0, j)),
        out_shape=jax.ShapeDtypeStruct((depth, cvec.shape[0], n), F32),
        compiler_params=_cparams("parallel", "parallel"),
        name="ada_mods",
    )(cvec, ada_w, ada_b.reshape(depth, 1, n))


class _Rows:
    def __init__(self, batch, seq, ctx_len):
        self.batch, self.seq, self.ctx_len = batch, seq, ctx_len
        self.n_lat = batch * seq
        self.n_ctx = batch * ctx_len
        self.m = self.n_lat + self.n_ctx

    def row_tile(self, prefs):
        return _pick(math.gcd(self.seq, self.n_ctx), prefs)

    def gid(self, i, tm):
        r0 = i * tm
        return jnp.where(r0 < self.n_lat, r0 // self.seq, self.batch)


def _mod_spec(rows, which, tm, d):
    return pl.BlockSpec((None, None, 1, d), lambda i: (which, rows.gid(i, tm), 0, 0))


def _norm_mod(x, gn, sh, sc):
    y = x * lax.rsqrt(jnp.mean(x * x, axis=-1, keepdims=True) + EPS) * gn
    return y * (1.0 + sc) + sh


def _norm_mod_kernel(x_ref, gn_ref, sh_ref, sc_ref, o_ref):
    o_ref[...] = _norm_mod(x_ref[...], gn_ref[...], sh_ref[...], sc_ref[...]).astype(o_ref.dtype)


def _norm_mod_router_kernel(x_ref, gn_ref, sh_ref, sc_ref, rw_ref, o_ref, ei_ref, ew_ref, *, n_experts):
    u = _norm_mod(x_ref[...], gn_ref[...], sh_ref[...], sc_ref[...])
    o_ref[...] = u
    logits = _dot(u, rw_ref[...], HIGHEST)
    lane_i = lax.broadcasted_iota(I32, logits.shape, 1)
    lane = lane_i.astype(F32)
    lg = jnp.where(lane_i < n_experts, logits, -jnp.inf)
    m1 = jnp.max(lg, axis=-1, keepdims=True)
    i1 = jnp.min(jnp.where(lg == m1, lane, float(LANES)), axis=-1, keepdims=True)
    lg2 = jnp.where(lane == i1, -jnp.inf, lg)
    m2 = jnp.max(lg2, axis=-1, keepdims=True)
    i2 = jnp.min(jnp.where(lg2 == m2, lane, float(LANES)), axis=-1, keepdims=True)
    e = jnp.exp(m2 - m1)
    w1 = 1.0 / (1.0 + e)
    w2 = e / (1.0 + e)
    ei_ref[...] = jnp.where(lane_i == 0, i1, jnp.where(lane_i == 1, i2, 0.0)).astype(I32)
    ew_ref[...] = jnp.where(lane_i == 0, w1, jnp.where(lane_i == 1, w2, 0.0))


def _norm_mod_call(rows, x, gn, mods, which_sh, which_sc, out_dtype, router_w=None):
    m, d = x.shape
    tm = rows.row_tile((256, 128, 64, 32, 16, 8))
    in_specs = [pl.BlockSpec((tm, d), lambda i: (i, 0)),
                pl.BlockSpec((1, d), lambda i: (0, 0)),
                _mod_spec(rows, which_sh, tm, d),
                _mod_spec(rows, which_sc, tm, d)]
    row_spec = pl.BlockSpec((tm, d), lambda i: (i, 0))
    if router_w is None:
        return pl.pallas_call(
            _norm_mod_kernel, grid=(m // tm,), in_specs=in_specs, out_specs=row_spec,
            out_shape=jax.ShapeDtypeStruct((m, d), out_dtype),
            compiler_params=_cparams("parallel"), name="norm_mod",
        )(x, gn.reshape(1, d), mods, mods)
    n_experts = router_w.shape[1]
    rw = jnp.zeros((d, LANES), F32).at[:, :n_experts].set(router_w)
    lane_spec = pl.BlockSpec((tm, LANES), lambda i: (i, 0))
    return pl.pallas_call(
        functools.partial(_norm_mod_router_kernel, n_experts=n_experts),
        grid=(m // tm,),
        in_specs=in_specs + [pl.BlockSpec((d, LANES), lambda i: (0, 0))],
        out_specs=[row_spec, lane_spec, lane_spec],
        out_shape=[jax.ShapeDtypeStruct((m, d), F32), jax.ShapeDtypeStruct((m, LANES), I32),
                   jax.ShapeDtypeStruct((m, LANES), F32)],
        compiler_params=_cparams("parallel"), name="norm_mod_router",
    )(x, gn.reshape(1, d), mods, mods, rw)


def _final_norm_kernel(x_ref, gn_ref, o_ref):
    x = x_ref[...]
    o_ref[...] = x * lax.rsqrt(jnp.mean(x * x, axis=-1, keepdims=True) + EPS) * gn_ref[...]


def _final_norm(rows, x, gn):
    d = x.shape[1]
    tm = rows.row_tile((256, 128, 64, 32, 16, 8))
    return pl.pallas_call(
        _final_norm_kernel, grid=(rows.n_lat // tm,),
        in_specs=[pl.BlockSpec((tm, d), lambda i: (i, 0)), pl.BlockSpec((1, d), lambda i: (0, 0))],
        out_specs=pl.BlockSpec((tm, d), lambda i: (i, 0)),
        out_shape=jax.ShapeDtypeStruct((rows.n_lat, d), F32),
        compiler_params=_cparams("parallel"), name="final_norm",
    )(x, gn.reshape(1, d))


def _stash_weights(pairs):
    @pl.when(pl.program_id(1) == 0)
    def _():
        for w_ref, wbuf in pairs:
            wbuf[...] = w_ref[...].astype(BF16)


def _w_spec(kdim, tn, layer, col_blk0, w_rows=False):
    if w_rows:
        return pl.BlockSpec((None, tn, kdim), lambda j, i: (layer, col_blk0 + j, 0))
    return pl.BlockSpec((None, kdim, tn), lambda j, i: (layer, 0, col_blk0 + j))


class _SideCast:
    def __init__(self, src, row0, rows, steps):
        self.src = src
        self.cols = src.shape[1]
        self.n_blocks = next(nb for nb in range(min(steps, rows // BF16_ROWS), 0, -1)
                             if rows % nb == 0 and (rows // nb) % BF16_ROWS == 0)
        self.block_rows = rows // self.n_blocks
        assert row0 % self.block_rows == 0
        self.blk0 = row0 // self.block_rows
        self.out_shape = jax.ShapeDtypeStruct((rows, self.cols), BF16)

    def specs(self, n_inner):
        step = lambda j, i: jnp.minimum(j * n_inner + i, self.n_blocks - 1)
        blk = (self.block_rows, self.cols)
        return (pl.BlockSpec(blk, lambda j, i: (self.blk0 + step(j, i), 0)),
                pl.BlockSpec(blk, lambda j, i: (step(j, i), 0)))


def _mm_kernel(*refs, act, residual, side, w_rows):
    a_ref, w_ref = refs[:2]
    x_ref, g_ref = refs[2:4] if residual else (None, None)
    pos = 4 if residual else 2
    if side:
        side_in, o_ref, side_out, wbuf = refs[pos:pos + 4]
        side_out[...] = side_in[...].astype(BF16)
    else:
        o_ref, wbuf = refs[pos:pos + 2]
    _stash_weights([(w_ref, wbuf)])
    r = _dot_nt(a_ref[...], wbuf[...]) if w_rows else _dot(a_ref[...], wbuf[...])
    if act == "sigmoid":
        r = jax.nn.sigmoid(r)
    if residual:
        r = x_ref[...] + g_ref[...] * r
    o_ref[...] = r.astype(o_ref.dtype)


def _mm_tiles(m, kdim, n, rows):
    tn = _pick(n, (1024, 512, 256, 128) if kdim <= 2048 else (512, 256, 128))
    row_prefs = (1024, 512, 256, 128, 64, 32, 16, 8) if kdim <= 2048 else (512, 256, 128, 64, 32, 16, 8)
    tm = _pick(m, row_prefs) if rows is None else rows.row_tile(row_prefs)
    return tm, tn


def _matmul(a, w, layer, col0, n, out_dtype, *, act=None, res=None, rows=None, mods=None, which=None,
            side=None, w_rows=False):
    m, kdim = a.shape
    tm, tn = _mm_tiles(m, kdim, n, rows)
    assert col0 % tn == 0
    residual = res is not None
    in_specs = [pl.BlockSpec((tm, kdim), lambda j, i: (i, 0)), _w_spec(kdim, tn, layer, col0 // tn, w_rows)]
    args = [a, w]
    if residual:
        in_specs += [pl.BlockSpec((tm, tn), lambda j, i: (i, j)),
                     pl.BlockSpec((None, None, 1, tn), lambda j, i: (which, rows.gid(i, tm), 0, j))]
        args += [res, mods]
    out_specs = [pl.BlockSpec((tm, tn), lambda j, i: (i, j))]
    out_shape = [jax.ShapeDtypeStruct((m, n), out_dtype)]
    if side is not None:
        side_in, side_out = side.specs(m // tm)
        in_specs.append(side_in)
        args.append(side.src)
        out_specs.append(side_out)
        out_shape.append(side.out_shape)
    outs = pl.pallas_call(
        functools.partial(_mm_kernel, act=act, residual=residual, side=side is not None, w_rows=w_rows),
        grid=(n // tn, m // tm),
        in_specs=in_specs,
        out_specs=out_specs,
        out_shape=out_shape,
        scratch_shapes=[pltpu.VMEM((tn, kdim) if w_rows else (kdim, tn), BF16)],
        compiler_params=_cparams("parallel", "arbitrary"),
        name="matmul",
    )(*args)
    return outs[0] if side is None else outs


def _swiglu_kernel(*refs, side):
    if side:
        a_ref, w1_ref, w3_ref, side_in, o_ref, side_out, w1buf, w3buf = refs
        side_out[...] = side_in[...].astype(BF16)
    else:
        a_ref, w1_ref, w3_ref, o_ref, w1buf, w3buf = refs
    _stash_weights([(w1_ref, w1buf), (w3_ref, w3buf)])
    a = a_ref[...]
    o_ref[...] = (_silu(_dot(a, w1buf[...])) * _dot(a, w3buf[...])).astype(o_ref.dtype)


def _swiglu_tiles(m, f):
    return _pick(m, (1024, 512, 256, 128, 64, 32, 16, 8)), _pick(f, (512, 256, 128))


def _swiglu_up(a, w1, w3, layer, side=None):
    m, d = a.shape
    f = w1.shape[2]
    tm, tn = _swiglu_tiles(m, f)
    in_specs = [pl.BlockSpec((tm, d), lambda j, i: (i, 0)), _w_spec(d, tn, layer, 0), _w_spec(d, tn, layer, 0)]
    args = [a, w1, w3]
    out_specs = [pl.BlockSpec((tm, tn), lambda j, i: (i, j))]
    out_shape = [jax.ShapeDtypeStruct((m, f), BF16)]
    if side is not None:
        side_in, side_out = side.specs(m // tm)
        in_specs.append(side_in)
        args.append(side.src)
        out_specs.append(side_out)
        out_shape.append(side.out_shape)
    outs = pl.pallas_call(
        functools.partial(_swiglu_kernel, side=side is not None), grid=(f // tn, m // tm),
        in_specs=in_specs, out_specs=out_specs, out_shape=out_shape,
        scratch_shapes=[pltpu.VMEM((d, tn), BF16), pltpu.VMEM((d, tn), BF16)],
        compiler_params=_cparams("parallel", "arbitrary"), name="swiglu_up",
    )(*args)
    return outs[0] if side is None else outs


def _up_merge_kernel(ha_ref, hb_ref, wa_ref, wb_ref, sga_ref, sgb_ref, o_ref, wabuf, wbbuf):
    _stash_weights([(wa_ref, wabuf), (wb_ref, wbbuf)])
    up_a = _dot(ha_ref[...], wabuf[...])
    up_b = _dot(hb_ref[...], wbbuf[...])
    o_ref[...] = (sga_ref[...] * up_a + sgb_ref[...] * up_b).astype(o_ref.dtype)


def _up_merge(ha, hb, wa, wb, layer, sg):
    m, va = ha.shape
    vb = hb.shape[1]
    d = wa.shape[2]
    tm = _pick(m, (512, 256, 128, 64, 32, 16, 8))
    tn = _pick(d, (1024, 512, 256, 128))
    nj = d // tn
    return pl.pallas_call(
        _up_merge_kernel, grid=(nj, m // tm),
        in_specs=[pl.BlockSpec((tm, va), lambda j, i: (i, 0)),
                  pl.BlockSpec((tm, vb), lambda j, i: (i, 0)),
                  _w_spec(va, tn, layer, 0), _w_spec(vb, tn, layer, 0),
                  pl.BlockSpec((tm, tn), lambda j, i: (i, j)),
                  pl.BlockSpec((tm, tn), lambda j, i: (i, j + nj))],
        out_specs=pl.BlockSpec((tm, tn), lambda j, i: (i, j)),
        out_shape=jax.ShapeDtypeStruct((m, d), BF16),
        scratch_shapes=[pltpu.VMEM((va, tn), BF16), pltpu.VMEM((vb, tn), BF16)],
        compiler_params=_cparams("parallel", "arbitrary"), name="up_merge",
    )(ha, hb, wa, wb, sg, sg)


def _conv_taps(xs, w, dy, not_first, not_last):
    n = xs.shape[0]
    left = jnp.where(not_first, pltpu.roll(xs, 1, 0), 0.0)
    right = jnp.where(not_last, pltpu.roll(xs, n - 1, 0), 0.0)
    return w[3 * dy:3 * dy + 1] * left + w[3 * dy + 1:3 * dy + 2] * xs + w[3 * dy + 2:3 * dy + 3] * right


def _conv_kernel(up_ref, x_ref, dn_ref, w_ref, s_ref, o_ref, pad_ref, *, rows, rb, strip):
    r0 = pl.program_id(0) * rb
    w = w_ref[...]

    @pl.when(r0 < rows.n_lat)
    def _():
        tc = x_ref.shape[-1]
        at_start = lax.rem(r0, rows.seq) == 0
        at_end = lax.rem(r0 + rb, rows.seq) == 0
        pad_ref[pl.ds(0, GRID_W), :] = jnp.where(at_start, 0.0, up_ref[...])
        pad_ref[pl.ds(GRID_W + rb, GRID_W), :] = jnp.where(at_end, 0.0, dn_ref[...])
        pad_ref[pl.ds(GRID_W, rb), :] = x_ref[...]
        col = lax.broadcasted_iota(I32, (strip, tc), 0) % GRID_W
        not_first = col != 0
        not_last = col != GRID_W - 1
        for s in range(rb // strip):
            acc = jnp.zeros((strip, tc), F32)
            for dy in range(3):
                xs = pad_ref[pl.ds(s * strip + dy * GRID_W, strip), :]
                acc = acc + _conv_taps(xs, w, dy, not_first, not_last)
            o_ref[pl.ds(s * strip, strip), :] = _silu(acc) * s_ref[...]

    @pl.when(r0 >= rows.n_lat)
    def _():
        x = x_ref[...]
        pos = lax.broadcasted_iota(I32, x.shape, 0) % rows.ctx_len
        acc = _conv_taps(x, w, 1, pos != 0, pos != rows.ctx_len - 1)
        o_ref[...] = _silu(acc) * s_ref[...]


def _conv_silu(rows, p_qk, conv_w, scale):
    m, c = p_qk.shape
    tc = _pick(c, (512, 256, 128))
    rb = rows.n_ctx
    strip = _pick(rb, (512, 256, 128, 64))
    per = rb // GRID_W
    last = m // GRID_W - 1
    return pl.pallas_call(
        functools.partial(_conv_kernel, rows=rows, rb=rb, strip=strip),
        grid=(m // rb, c // tc),
        in_specs=[pl.BlockSpec((GRID_W, tc), lambda i, j: (jnp.maximum(i * per - 1, 0), j)),
                  pl.BlockSpec((rb, tc), lambda i, j: (i, j)),
                  pl.BlockSpec((GRID_W, tc), lambda i, j: (jnp.minimum((i + 1) * per, last), j)),
                  pl.BlockSpec((9, tc), lambda i, j: (0, j)),
                  pl.BlockSpec((1, tc), lambda i, j: (0, j))],
        out_specs=pl.BlockSpec((rb, tc), lambda i, j: (i, j)),
        out_shape=jax.ShapeDtypeStruct((m, c), F32),
        scratch_shapes=[pltpu.VMEM((rb + 2 * GRID_W, tc), F32)],
        compiler_params=_cparams("parallel", "parallel"), name="conv_silu",
    )(p_qk, p_qk, p_qk, conv_w.reshape(9, c), scale)


def _causal_masks(n):
    t = lax.broadcasted_iota(I32, (n, n), 0)
    s = lax.broadcasted_iota(I32, (n, n), 1)
    return t, s, (s <= t, s >= t)


def _block_ref_rows(x, blk, row):
    n, c = x.shape
    if blk >= SUBLANES:
        x3 = x.reshape(n // blk, blk, c)
        return jnp.broadcast_to(x3[:, row:row + 1, :], x3.shape).reshape(n, c)
    x3 = x.reshape(n // SUBLANES, SUBLANES, c)
    sub = lax.broadcasted_iota(I32, x3.shape, 1)
    out = jnp.zeros_like(x3)
    for g in range(SUBLANES // blk):
        r = g * blk + row
        out = jnp.where(sub // blk == g, jnp.broadcast_to(x3[:, r:r + 1, :], x3.shape), out)
    return out.reshape(n, c)


def _gla_direction(d, q, k, v, lr, wg, bg, s_ref, causal, lev_blk):
    n = q.shape[0]
    hn = n // 2
    lh, lm, _ = _split3(lr)
    wh, wm, _ = _split3(wg)
    z = _dot(jnp.concatenate([lh, lh, lm], axis=1), jnp.concatenate([wh, wm, wh], axis=0)) + bg
    la = jax.nn.log_sigmoid(z) * (LOG2_E / GLA_GATE_NORMALIZER)
    tri = causal.astype(BF16)
    c3 = _dot(tri, jnp.concatenate(_split3(la), axis=1))
    nk = la.shape[1]
    cum = c3[:, :nk] + c3[:, nk:2 * nk] + c3[:, 2 * nk:]
    total = cum[n - 1:n] if d == 0 else cum[0:1]
    state = s_ref[...]
    out = _dot((q * jnp.exp2(cum)).astype(BF16), state.astype(BF16))
    qb = q.astype(BF16)
    kb = k.astype(BF16)
    lo, hi = slice(0, hn), slice(hn, n)
    diag = [jnp.where(lev_blk == -1, _dot_nt(qb[h], kb[h]).astype(BF16), jnp.zeros((), BF16)) for h in (lo, hi)]
    top = n.bit_length() - 2
    for l in range(top + 1):
        half = 1 << l
        ref = _block_ref_rows(cum, 2 * half, half - 1 if d == 0 else half)
        e = jnp.exp2(-jnp.abs(cum - ref)).astype(BF16)
        qe = qb * e
        ke = kb * e
        if l < top:
            diag = [jnp.where(lev_blk == l, _dot_nt(qe[h], ke[h]).astype(BF16), sc)
                    for h, sc in zip((lo, hi), diag)]
        else:
            cross = (_dot_nt(qe[hi], ke[lo]) if d == 0 else _dot_nt(qe[lo], ke[hi])).astype(BF16)
    zero = jnp.zeros((hn, hn), BF16)
    upper, lower = ([diag[0], zero], [cross, diag[1]]) if d == 0 else ([diag[0], cross], [zero, diag[1]])
    scores = jnp.concatenate([jnp.concatenate(upper, axis=1), jnp.concatenate(lower, axis=1)], axis=0)
    out = out + _dot(scores, v)
    k_out = (k * jnp.exp2(total - cum)).astype(BF16)
    dk = state.shape[0]
    et = jnp.broadcast_to(jnp.exp2(total), (dk, dk)).T
    scale = jnp.concatenate([et] * (state.shape[1] // dk), axis=1)
    s_ref[...] = scale * state + _dot_tn(k_out, v)
    return out


def _gla_kernel(qf, kf, vf, smf, qb, kb, vb, smb, wg_ref, bg_ref, of_ref, ob_ref, sf_ref, sb_ref):
    @pl.when(pl.program_id(1) == 0)
    def _():
        sf_ref[...] = jnp.zeros_like(sf_ref)
        sb_ref[...] = jnp.zeros_like(sb_ref)

    n = qf.shape[0]
    _, _, causal = _causal_masks(n)
    t_idx, s_idx, causal_blk = _causal_masks(n // 2)
    lev = 31 - lax.clz(t_idx ^ s_idx)
    dirs = ((qf, kf, vf, smf, of_ref, sf_ref), (qb, kb, vb, smb, ob_ref, sb_ref))
    for d, (q, k, v, sm, o_ref, s_ref) in enumerate(dirs):
        lr = sm[:, d * GLA_RANK:(d + 1) * GLA_RANK]
        lev_blk = jnp.where(causal_blk[d], lev, -2).astype(BF16)
        for hd in range(GLA_HEADS):
            ks = slice(hd * GLA_DK, (hd + 1) * GLA_DK)
            vs = slice(hd * GLA_DV, (hd + 1) * GLA_DV)
            o_ref[:, vs] = _gla_direction(d, q[:, ks], k[:, ks], v[:, vs], lr, wg_ref[d, :, ks], bg_ref[d, :, ks],
                                          s_ref.at[hd], causal[d], lev_blk)


def _soft_cap(z):
    return GATE_SOFT_CAP * jnp.tanh(z * (1.0 / GATE_SOFT_CAP))


def _mlstm_direction(d, head, q, k, v, g_col, g_row, c_ref, m_ref, causal):
    n = q.shape[0]
    tri = causal.astype(BF16)
    i_idx = 2 * MLSTM_HEADS * d + head
    f_idx = i_idx + MLSTM_HEADS
    sel_rows = lax.broadcasted_iota(I32, (3 * N_GATE_B, LANES), 0) % N_GATE_B
    rep = lambda a, idx: _dot(jnp.concatenate(_split3(a), axis=1), (sel_rows == idx).astype(BF16))
    i_rep = rep(g_col, i_idx)
    f_rep = rep(jax.nn.log_sigmoid(g_col), f_idx)
    c3 = _dot(tri, jnp.concatenate(_split3(f_rep), axis=1))
    cum = c3[:, :LANES] + c3[:, LANES:2 * LANES] + c3[:, 2 * LANES:]
    r3 = _dot_nt(jnp.concatenate(_split3(jax.nn.log_sigmoid(g_row)), axis=0), tri)
    cum_r_all = r3[:N_GATE_B] + r3[N_GATE_B:2 * N_GATE_B] + r3[2 * N_GATE_B:]
    sub = lax.broadcasted_iota(I32, (N_GATE_B, 1), 0)
    pick_r = lambda a, idx: jnp.sum(jnp.where(sub == idx, a, 0.0), axis=0, keepdims=True)
    i_r, cum_r = pick_r(g_row, i_idx), pick_r(cum_r_all, f_idx)
    wide = lambda a, reps: jnp.concatenate([a] * reps, axis=1)
    nt = n // LANES
    total = cum[n - 1:n] if d == 0 else cum[0:1]
    m_prev = m_ref[...]
    dlog = jnp.where(causal, wide(cum, nt) - cum_r + i_r, NEG_BIG)
    inter_log = cum + m_prev
    row_max = dlog[:, :LANES]
    for t in range(1, nt):
        row_max = jnp.maximum(row_max, dlog[:, t * LANES:(t + 1) * LANES])
    m_t = jnp.maximum(inter_log, jnp.max(row_max, axis=1, keepdims=True))
    w_inter = jnp.exp(inter_log - m_t)
    qb16 = q.astype(BF16)
    s = _dot_nt(qb16, k.astype(BF16)) * jnp.exp(dlog - wide(m_t, nt))
    dv = v.shape[1]
    v_ext = jnp.concatenate([v, jnp.ones((n, LANES), BF16)], axis=1)
    state = c_ref[...]
    acc = wide(w_inter, dv // LANES + 1) * _dot(qb16, state.astype(BF16)) + _dot(s.astype(BF16), v_ext)
    bound = jnp.maximum(jnp.abs(acc[:, dv:]), jnp.exp(-m_t))
    out = acc[:, :dv] / wide(bound, dv // LANES)
    g = total - cum + i_rep
    m_new = jnp.maximum(total + m_prev, jnp.max(g, axis=0, keepdims=True))
    w_c = jnp.exp(total + m_prev - m_new)
    w_k = jnp.exp(g - m_new)
    c_ref[...] = wide(w_c, dv // LANES + 1) * state + _dot_tn((k * w_k).astype(BF16), v_ext)
    m_ref[...] = m_new
    return out


def _mlstm_kernel(qf, kf, vf, gcf, grf, qb, kb, vb, gcb, grb, brow_ref, bcol_ref,
                  of_ref, ob_ref, cf_ref, cb_ref, mf_ref, mb_ref):
    @pl.when(pl.program_id(1) == 0)
    def _():
        cf_ref[...] = jnp.zeros_like(cf_ref)
        cb_ref[...] = jnp.zeros_like(cb_ref)
        mf_ref[...] = jnp.zeros_like(mf_ref)
        mb_ref[...] = jnp.zeros_like(mb_ref)

    n = qf.shape[0]
    _, _, causal = _causal_masks(n)
    g0 = 2 * GLA_RANK
    dirs = ((qf, kf, vf, gcf, grf, of_ref, cf_ref, mf_ref), (qb, kb, vb, gcb, grb, ob_ref, cb_ref, mb_ref))
    for d, (q, k, v, gc, gr, o_ref, c_ref, m_ref) in enumerate(dirs):
        g_col = _soft_cap(gc[:, g0:g0 + N_GATE_B] + brow_ref[...])
        g_row = _soft_cap(gr[...] + bcol_ref[...])
        for hd in range(MLSTM_HEADS):
            qs = slice(hd * MLSTM_DQK, (hd + 1) * MLSTM_DQK)
            vs = slice(hd * MLSTM_DV, (hd + 1) * MLSTM_DV)
            o_ref[:, vs] = _mlstm_direction(d, hd, q[:, qs], k[:, qs], v[:, vs], g_col, g_row,
                                            c_ref.at[hd], m_ref.at[hd], causal[d])


def _scan_row_maps(rows, chunk):
    lat_chunks = rows.seq // chunk
    ctx_chunks = rows.ctx_len // chunk
    ctx0 = rows.n_lat // chunk

    def fwd(b, s):
        return jnp.where(s < ctx_chunks, ctx0 + b * ctx_chunks + s, b * lat_chunks + (s - ctx_chunks))

    def bwd(b, s):
        return jnp.where(s < ctx_chunks, ctx0 + b * ctx_chunks + (ctx_chunks - 1 - s),
                         b * lat_chunks + (lat_chunks - 1 - (s - ctx_chunks)))

    return fwd, bwd, ctx_chunks + lat_chunks


def _gla_scan(rows, qk, v, small, gate_w, gate_b):
    m = qk.shape[0]
    n = SCAN_CHUNK
    fwd, bwd, steps = _scan_row_maps(rows, n)

    def specs(rmap):
        return [pl.BlockSpec((n, QA), lambda b, s: (rmap(b, s), 0)),
                pl.BlockSpec((n, QA), lambda b, s: (rmap(b, s), 1)),
                pl.BlockSpec((n, V_A), lambda b, s: (rmap(b, s), 0)),
                pl.BlockSpec((n, LANES), lambda b, s: (rmap(b, s), 0))]

    out_spec = lambda rmap: pl.BlockSpec((n, V_A), lambda b, s: (rmap(b, s), 0))
    out_sds = jax.ShapeDtypeStruct((m, V_A), F32)
    state = pltpu.VMEM((GLA_HEADS, GLA_DK, GLA_DV), F32)
    return pl.pallas_call(
        _gla_kernel,
        grid=(rows.batch, steps),
        in_specs=specs(fwd) + specs(bwd) + [
            pl.BlockSpec((2, GLA_RANK, QA), lambda b, s: (0, 0, 0)),
            pl.BlockSpec((2, 1, QA), lambda b, s: (0, 0, 0))],
        out_specs=[out_spec(fwd), out_spec(bwd)],
        out_shape=[out_sds, out_sds],
        scratch_shapes=[state, state],
        compiler_params=_cparams("parallel", "arbitrary"), name="gla_scan",
    )(qk, qk, v, small, qk, qk, v, small, gate_w, gate_b.reshape(2, 1, QA))


def _mlstm_scan(rows, qk, v, small, gates_t, gate_b):
    m = qk.shape[0]
    n = SCAN_CHUNK
    fwd, bwd, steps = _scan_row_maps(rows, n)
    q0 = 2 * QA // QB
    v0 = V_A // V_B

    def specs(rmap):
        return [pl.BlockSpec((n, QB), lambda b, s: (rmap(b, s), q0)),
                pl.BlockSpec((n, QB), lambda b, s: (rmap(b, s), q0 + 1)),
                pl.BlockSpec((n, V_B), lambda b, s: (rmap(b, s), v0)),
                pl.BlockSpec((n, LANES), lambda b, s: (rmap(b, s), 0)),
                pl.BlockSpec((N_GATE_B, n), lambda b, s: (0, rmap(b, s)))]

    out_spec = lambda rmap: pl.BlockSpec((n, V_B), lambda b, s: (rmap(b, s), 0))
    out_sds = jax.ShapeDtypeStruct((m, V_B), F32)
    state = pltpu.VMEM((MLSTM_HEADS, MLSTM_DQK, MLSTM_DV + LANES), F32)
    stab = pltpu.VMEM((MLSTM_HEADS, 1, LANES), F32)
    return pl.pallas_call(
        _mlstm_kernel,
        grid=(rows.batch, steps),
        in_specs=specs(fwd) + specs(bwd) + [
            pl.BlockSpec((1, N_GATE_B), lambda b, s: (0, 0)),
            pl.BlockSpec((N_GATE_B, 1), lambda b, s: (0, 0))],
        out_specs=[out_spec(fwd), out_spec(bwd)],
        out_shape=[out_sds, out_sds],
        scratch_shapes=[state, state, stab, stab],
        compiler_params=_cparams("parallel", "arbitrary"), name="mlstm_scan",
    )(qk, qk, v, small, gates_t, qk, qk, v, small, gates_t,
      gate_b.reshape(1, N_GATE_B), gate_b.reshape(N_GATE_B, 1))


def _mix_prep_kernel(oaf, oab, obf, obb, ro_ref, gna_ref, gnb_ref, ha_ref, hb_ref):
    branches = ((oaf, oab, gna_ref, ha_ref, 0, GLA_HEADS, GLA_DV, _silu),
                (obf, obb, gnb_ref, hb_ref, V_A, MLSTM_HEADS, MLSTM_DV, jax.nn.sigmoid))
    for of, ob, gn_ref, h_ref, off, heads, dv, gate_fn in branches:
        for h in range(heads):
            sl = slice(h * dv, (h + 1) * dv)
            o = of[:, sl] + ob[:, sl]
            y = o * lax.rsqrt(jnp.mean(o * o, axis=-1, keepdims=True) + EPS) * gn_ref[:, sl]
            gate = gate_fn(ro_ref[:, off + h * dv:off + (h + 1) * dv])
            h_ref[:, sl] = (y * gate).astype(h_ref.dtype)


def _mix_prep(oaf, oab, obf, obb, ro, gna, gnb):
    m = ro.shape[0]
    tm = _pick(m, (256, 128, 64, 32, 16, 8))
    row = lambda c: pl.BlockSpec((tm, c), lambda i: (i, 0))
    vec = lambda c: pl.BlockSpec((1, c), lambda i: (0, 0))
    return pl.pallas_call(
        _mix_prep_kernel, grid=(m // tm,),
        in_specs=[row(V_A), row(V_A), row(V_B), row(V_B), row(V_A + V_B), vec(V_A), vec(V_B)],
        out_specs=[row(V_A), row(V_B)],
        out_shape=[jax.ShapeDtypeStruct((m, V_A), BF16), jax.ShapeDtypeStruct((m, V_B), BF16)],
        compiler_params=_cparams("parallel"), name="mix_prep",
    )(oaf, oab, obf, obb, ro, gna.reshape(1, V_A), gnb.reshape(1, V_B))


def _expert_up_kernel(te_ref, na_ref, src0_ref, srcn_ref, v_hbm, w1_ref, w3_ref, o_ref, xbuf, abuf, sem,
                      *, tm, issue_steps):
    del te_ref
    i = pl.program_id(0)
    j = pl.program_id(1)
    slot = lax.rem(i, 2)
    per = tm // issue_steps

    def row_copy(src_ref, grp, t, s):
        return pltpu.make_async_copy(v_hbm.at[pl.ds(src_ref[0, grp * per + t], 1)],
                                     xbuf.at[s, grp, pl.ds(t, 1)], sem.at[s])

    @pl.when(jnp.logical_and(i == 0, j == 0))
    def _():
        def start(grp, carry):
            for t in range(per):
                row_copy(src0_ref, grp, t, 0).start()
            return carry

        lax.fori_loop(0, issue_steps, start, 0)

    @pl.when(j == 0)
    def _():
        def wait(r, carry):
            row_copy(src0_ref, 0, 0, slot).wait()
            return carry

        lax.fori_loop(0, tm, wait, 0, unroll=32)
        abuf[...] = xbuf[slot].reshape(tm, abuf.shape[1]).astype(BF16)

    @pl.when(jnp.logical_and(i + 1 < pl.num_programs(0), j < issue_steps))
    def _():
        for t in range(per):
            row_copy(srcn_ref, j, t, 1 - slot).start()

    @pl.when(i < na_ref[0])
    def _():
        a = abuf[...]
        o_ref[...] = (_silu(_dot(a, w1_ref[...])) * _dot(a, w3_ref[...])).astype(o_ref.dtype)

    @pl.when(i >= na_ref[0])
    def _():
        o_ref[...] = jnp.zeros_like(o_ref)


def _expert_up(v, src, w1, w3, layer, tile_expert, n_active, tm):
    n_tiles = src.shape[0]
    d = v.shape[1]
    f = w1.shape[3]
    tn = _pick(f, (1408, 512, 256, 128))
    nj = f // tn
    issue_steps = max(s for s in (1, 2, 4, 8) if s <= nj)
    wmap = lambda i, j, te, na: (layer, te[i], 0, jnp.where(i < na[0], j, 0))
    return pl.pallas_call(
        functools.partial(_expert_up_kernel, tm=tm, issue_steps=issue_steps),
        grid_spec=pltpu.PrefetchScalarGridSpec(
            num_scalar_prefetch=2, grid=(n_tiles, nj),
            in_specs=[pl.BlockSpec((None, 1, tm), lambda i, j, te, na: (0, 0, 0), memory_space=pltpu.SMEM),
                      pl.BlockSpec((None, 1, tm), lambda i, j, te, na: (jnp.minimum(i + 1, n_tiles - 1), 0, 0),
                                   memory_space=pltpu.SMEM),
                      pl.BlockSpec(memory_space=pl.ANY),
                      pl.BlockSpec((None, None, d, tn), wmap),
                      pl.BlockSpec((None, None, d, tn), wmap)],
            out_specs=pl.BlockSpec((tm, tn), lambda i, j, te, na: (i, j)),
            scratch_shapes=[pltpu.VMEM((2, issue_steps, tm // issue_steps, d), F32), pltpu.VMEM((tm, d), BF16),
                            pltpu.SemaphoreType.DMA((2,))]),
        out_shape=jax.ShapeDtypeStruct((n_tiles * tm, f), BF16),
        compiler_params=_cparams("arbitrary", "arbitrary"), name="expert_up",
    )(tile_expert, n_active, src, src, v, w1, w3)


def _expert_down_kernel(te_ref, na_ref, a_ref, w_ref, o_ref):
    del te_ref
    i = pl.program_id(0)

    @pl.when(i < na_ref[0])
    def _():
        o_ref[...] = _dot(a_ref[...], w_ref[...])

    @pl.when(i >= na_ref[0])
    def _():
        o_ref[...] = jnp.zeros_like(o_ref)


def _expert_down(hs, w2, layer, tile_expert, n_active, tm):
    p, f = hs.shape
    d = w2.shape[3]
    tn = _pick(d, (1024, 512, 256, 128))
    return pl.pallas_call(
        _expert_down_kernel,
        grid_spec=pltpu.PrefetchScalarGridSpec(
            num_scalar_prefetch=2, grid=(p // tm, d // tn),
            in_specs=[pl.BlockSpec((tm, f), lambda i, j, te, na: (jnp.where(i < na[0], i, 0), 0)),
                      pl.BlockSpec((None, None, f, tn),
                                   lambda i, j, te, na: (layer, te[i], 0, jnp.where(i < na[0], j, 0)))],
            out_specs=pl.BlockSpec((tm, tn), lambda i, j, te, na: (i, j))),
        out_shape=jax.ShapeDtypeStruct((p, d), F32),
        compiler_params=_cparams("parallel", "parallel"), name="expert_down",
    )(tile_expert, n_active, hs, w2)


def _moe_combine_kernel(pos_ref, posn_ref, ys_hbm, x_ref, g_ref, w_ref, fn_ref, o_ref, buf, sem,
                        *, tile, final_norm):
    i = pl.program_id(0)
    slot = lax.rem(i, 2)

    def row_copy(p_ref, k, grp, t, s):
        return pltpu.make_async_copy(ys_hbm.at[pl.ds(p_ref[0, k * tile + grp * COPY_GROUP + t], 1)],
                                     buf.at[s, k, grp, pl.ds(t, 1)], sem.at[s])

    def start_tile(p_ref, s):
        for k in range(TOP_K):
            def start(grp, carry):
                for t in range(COPY_GROUP):
                    row_copy(p_ref, k, grp, t, s).start()
                return carry

            lax.fori_loop(0, tile // COPY_GROUP, start, 0)

    @pl.when(i == 0)
    def _():
        start_tile(pos_ref, 0)

    @pl.when(i + 1 < pl.num_programs(0))
    def _():
        start_tile(posn_ref, 1 - slot)

    def wait(r, carry):
        row_copy(pos_ref, 0, 0, 0, slot).wait()
        return carry

    lax.fori_loop(0, TOP_K * tile, wait, 0, unroll=32)
    w = w_ref[...]
    d = x_ref.shape[1]
    y = w[:, 0:1] * buf[slot, 0].reshape(tile, d) + w[:, 1:2] * buf[slot, 1].reshape(tile, d)
    r = x_ref[...] + g_ref[...] * y
    if final_norm:
        r = r * lax.rsqrt(jnp.mean(r * r, axis=-1, keepdims=True) + EPS) * fn_ref[...]
    o_ref[...] = r


def _moe_combine(rows, ys, pos, x, mods, which, ew, m, final_gn):
    d = x.shape[1]
    tile = rows.row_tile((256, 128, 64, 32, 16, 8))
    n = m // tile
    pos_spec = lambda imap: pl.BlockSpec((None, 1, TOP_K * tile), imap, memory_space=pltpu.SMEM)
    final_norm = final_gn is not None
    fn = final_gn.reshape(1, d) if final_norm else jnp.ones((1, d), F32)
    return pl.pallas_call(
        functools.partial(_moe_combine_kernel, tile=tile, final_norm=final_norm),
        grid=(n,),
        in_specs=[pos_spec(lambda i: (i, 0, 0)),
                  pos_spec(lambda i: (jnp.minimum(i + 1, n - 1), 0, 0)),
                  pl.BlockSpec(memory_space=pl.ANY),
                  pl.BlockSpec((tile, d), lambda i: (i, 0)),
                  _mod_spec(rows, which, tile, d),
                  pl.BlockSpec((tile, LANES), lambda i: (i, 0)),
                  pl.BlockSpec((1, d), lambda i: (0, 0))],
        out_specs=pl.BlockSpec((tile, d), lambda i: (i, 0)),
        out_shape=jax.ShapeDtypeStruct((m, d), F32),
        scratch_shapes=[pltpu.VMEM((2, TOP_K, tile // COPY_GROUP, COPY_GROUP, d), F32),
                        pltpu.SemaphoreType.DMA((2,))],
        compiler_params=_cparams("arbitrary"), name="moe_combine",
    )(pos, pos, ys, x, mods, ew, fn)


def _moe_dispatch(e_idx, n_experts, tm, combine_tile):
    m = e_idx.shape[0]
    ex = jnp.concatenate([e_idx[:, 0], e_idx[:, 1]])
    onehot = (ex[:, None] == jnp.arange(n_experts, dtype=I32)[None, :]).astype(I32)
    rank = jnp.sum((jnp.cumsum(onehot, axis=0) - onehot) * onehot, axis=1)
    counts = jnp.sum(onehot, axis=0)
    padded = (counts + tm - 1) // tm * tm
    ends = jnp.cumsum(padded)
    dest = (ends - padded)[ex] + rank
    n_tiles = (TOP_K * m + n_experts * (tm - 1)) // tm
    tok = jnp.concatenate([jnp.arange(m, dtype=I32)] * TOP_K)
    src = jnp.zeros((n_tiles * tm,), I32).at[dest].set(tok).reshape(n_tiles, 1, tm)
    tile_start = jnp.arange(n_tiles, dtype=I32) * tm
    tile_expert = jnp.minimum(jnp.sum((ends[None, :] <= tile_start[:, None]).astype(I32), axis=1), n_experts - 1)
    n_active = (ends[-1] // tm).astype(I32).reshape(1)
    pos = dest.reshape(TOP_K, m // combine_tile, combine_tile).transpose(1, 0, 2).reshape(
        m // combine_tile, 1, TOP_K * combine_tile).astype(I32)
    return src, tile_expert, n_active, pos


def _moe_ffn(rows, x, gn, mods, router_w, w1, w3, w2, layer, n_tok, final_gn):
    n_experts = w1.shape[1]
    v, e_idx, e_w = _norm_mod_call(rows, x, gn, mods, 3, 4, F32, router_w=router_w)
    tm = 512 if TOP_K * n_tok >= 8192 else 64
    combine_tile = rows.row_tile((256, 128, 64, 32, 16, 8))
    src, tile_expert, n_active, pos = _moe_dispatch(e_idx[:n_tok], n_experts, tm, combine_tile)
    hs = _expert_up(v, src, w1, w3, layer, tile_expert, n_active, tm)
    ys = _expert_down(hs, w2, layer, tile_expert, n_active, tm)
    return _moe_combine(rows, ys, pos, x, mods, 5, e_w, n_tok, final_gn)


def kernel(x, c, ctx, c_ctx, ada_w, ada_b, norm_mix, norm_ffn, w_in, conv_w, gla_gate_w, gla_gate_b,
           mlstm_gate_b, gla_out_norm, mlstm_out_norm, w_up_a, w_up_b, w_o, ffn_w1, ffn_w3, ffn_w2,
           router_w, moe_w1, moe_w3, moe_w2, norm_final):
    batch, seq, d = x.shape
    ctx_len = ctx.shape[1]
    depth = ada_w.shape[0]
    assert seq % GRID_W == 0 and seq % SCAN_CHUNK == 0 and ctx_len % SCAN_CHUNK == 0
    assert seq % (batch * ctx_len) == 0 and (batch * ctx_len) % GRID_W == 0
    rows = _Rows(batch, seq, ctx_len)

    h = jnp.concatenate([x.reshape(batch * seq, d), ctx.reshape(batch * ctx_len, d)], axis=0)

    cvec = jnp.zeros((SUBLANES * ((batch + 1 + SUBLANES - 1) // SUBLANES), d), F32)
    cvec = cvec.at[:batch].set(c).at[batch].set(c_ctx)
    mods_all = _ada_mods(cvec, ada_w, ada_b)
    mods_all = mods_all[:, :batch + 1].reshape(depth, batch + 1, 6, 1, d).transpose(0, 2, 1, 3, 4)

    qk_scale = jnp.ones((QK_COLS,), F32)
    qk_scale = qk_scale.at[:QA].set(GLA_DK ** -0.5).at[2 * QA + QB:].set(MLSTM_DQK ** -0.5).reshape(1, QK_COLS)
    c_v, c_ro = QK_COLS, QK_COLS + V_A + V_B
    c_sm = c_ro + V_A + V_B
    c_g = c_sm + 2 * GLA_RANK + N_GATE_B
    n_small = c_g - c_sm
    n_exp, d_ff = moe_w1.shape[1], moe_w1.shape[3]
    moe_w1_rows = moe_w1.reshape(-1, d_ff)
    moe_w3_rows = moe_w3.reshape(-1, d_ff)
    moe_w2_rows = moe_w2.reshape(-1, d)
    gate_tm, gate_tn = _mm_tiles(rows.m, d, 2 * d, None)
    gate_steps = (2 * d // gate_tn) * (rows.m // gate_tm)
    up_tm, up_tn = _swiglu_tiles(rows.m, ffn_w1.shape[2])
    up_steps = (ffn_w1.shape[2] // up_tn) * (rows.m // up_tm)
    w_in_t = jnp.swapaxes(w_in, 1, 2)

    for l in range(depth):
        mods = mods_all[l]
        w_sm = jnp.zeros((1, LANES, d), F32).at[0, :n_small].set(w_in_t[l, c_sm:c_g])
        w_g = w_in_t[l, c_g:][None]

        u = _norm_mod_call(rows, h, norm_mix[l], mods, 0, 1, BF16)
        p_qk = _matmul(u, w_in_t, l, 0, QK_COLS, F32, w_rows=True)
        v = _matmul(u, w_in_t, l, c_v, V_A + V_B, BF16, w_rows=True)
        ro = _matmul(u, w_in_t, l, c_ro, V_A + V_B, F32, w_rows=True)
        small = _matmul(u, w_sm, 0, 0, LANES, F32, w_rows=True)
        if l % 2 == 0 and l + 1 < depth:
            cast = _SideCast(moe_w2_rows, (l // 2) * n_exp * d_ff, n_exp * d_ff, gate_steps)
            sg, w2_bf = _matmul(u, w_g, 0, 0, 2 * d, BF16, act="sigmoid", side=cast, w_rows=True)
        elif l % 2 == 1:
            cast = _SideCast(moe_w3_rows, (l // 2) * n_exp * d, n_exp * d, gate_steps)
            sg, w3_bf = _matmul(u, w_g, 0, 0, 2 * d, BF16, act="sigmoid", side=cast, w_rows=True)
        else:
            sg = _matmul(u, w_g, 0, 0, 2 * d, BF16, act="sigmoid", w_rows=True)
        qk = _conv_silu(rows, p_qk, conv_w[l], qk_scale)
        gates_t = small[:, 2 * GLA_RANK:n_small].T
        oaf, oab = _gla_scan(rows, qk, v, small, gla_gate_w[l], gla_gate_b[l])
        obf, obb = _mlstm_scan(rows, qk, v, small, gates_t, mlstm_gate_b[l])
        ha, hb = _mix_prep(oaf, oab, obf, obb, ro, gla_out_norm[l], mlstm_out_norm[l])
        merged = _up_merge(ha, hb, w_up_a, w_up_b, l, sg)
        h = _matmul(merged, w_o, l, 0, d, F32, res=h, rows=rows, mods=mods, which=2)

        if l % 2 == 0:
            e = l // 2
            vv = _norm_mod_call(rows, h, norm_ffn[l], mods, 3, 4, BF16)
            if l + 1 < depth:
                cast = _SideCast(moe_w1_rows, e * n_exp * d, n_exp * d, up_steps)
                hid, w1_bf = _swiglu_up(vv, ffn_w1, ffn_w3, e, side=cast)
            else:
                hid = _swiglu_up(vv, ffn_w1, ffn_w3, e)
            h = _matmul(hid, ffn_w2, e, 0, d, F32, res=h, rows=rows, mods=mods, which=5)
        else:
            e = l // 2
            n_tok = rows.n_lat if l == depth - 1 else rows.m
            h = _moe_ffn(rows, h, norm_ffn[l], mods, router_w[e], w1_bf.reshape(1, n_exp, d, d_ff),
                         w3_bf.reshape(1, n_exp, d, d_ff), w2_bf.reshape(1, n_exp, d_ff, d), 0, n_tok,
                         final_gn=norm_final if l == depth - 1 else None)

    out = h if depth % 2 == 0 else _final_norm(rows, h, norm_final)
    return out.reshape(batch, seq, d)
```

```python
import functools
import math

import jax
import jax.numpy as jnp
from jax import lax
from jax.experimental import pallas as pl
from jax.experimental.pallas import tpu as pltpu

F32 = jnp.float32
BF16 = jnp.bfloat16
I32 = jnp.int32

GRID_W = 64
GLA_HEADS = 4
GLA_DK = 128
GLA_DV = 256
GLA_RANK = 16
GLA_GATE_NORMALIZER = 16.0
MLSTM_HEADS = 4
MLSTM_DQK = 128
MLSTM_DV = 256
GATE_SOFT_CAP = 15.0
N_GATE_B = 4 * MLSTM_HEADS
TOP_K = 2
EPS = 1e-6
QA = GLA_HEADS * GLA_DK
QB = MLSTM_HEADS * MLSTM_DQK
QK_COLS = 2 * QA + 2 * QB
V_A = GLA_HEADS * GLA_DV
V_B = MLSTM_HEADS * MLSTM_DV

LANES = 128
SUBLANES = 8
BF16_ROWS = 16
COPY_GROUP = 8
SCAN_CHUNK = 256
VMEM_LIMIT = 56 * 1024 * 1024
NEG_BIG = -1e30
LOG2_E = 1.4426950408889634
HIGHEST = lax.Precision.HIGHEST


def _cparams(*sem):
    return pltpu.CompilerParams(dimension_semantics=sem, vmem_limit_bytes=VMEM_LIMIT)


def _pick(dim, prefs):
    for p in prefs:
        if dim % p == 0:
            return p
    return dim


def _dot(a, b, precision=None):
    return jnp.dot(a, b, preferred_element_type=F32, precision=precision)


def _dot_nt(a, b, precision=None):
    return lax.dot_general(a, b, (((1,), (1,)), ((), ())), preferred_element_type=F32, precision=precision)


def _dot_tn(a, b):
    return lax.dot_general(a, b, (((0,), (0,)), ((), ())), preferred_element_type=F32)


def _silu(x):
    return x * jax.nn.sigmoid(x)


def _split3(x):
    hi = x.astype(BF16)
    rest = x - hi.astype(F32)
    mid = rest.astype(BF16)
    lo = (rest - mid.astype(F32)).astype(BF16)
    return hi, mid, lo


def _ada_kernel(c_ref, w_ref, b_ref, o_ref):
    o_ref[...] = _dot(_silu(c_ref[...]), w_ref[...], HIGHEST) + b_ref[...]


def _ada_mods(cvec, ada_w, ada_b):
    depth, d, n = ada_w.shape
    tn = _pick(n, (1024, 512, 256, 128))
    return pl.pallas_call(
        _ada_kernel,
        grid=(depth, n // tn),
        in_specs=[pl.BlockSpec(cvec.shape, lambda l, j: (0, 0)),
                  pl.BlockSpec((None, d, tn), lambda l, j: (l, 0, j)),
                  pl.BlockSpec((None, 1, tn), lambda l, j: (l, 0, j))],
        out_specs=pl.BlockSpec((None, cvec.shape[0], tn), lambda l, j: (l, 0, j)),
        out_shape=jax.ShapeDtypeStruct((depth, cvec.shape[0], n), F32),
        compiler_params=_cparams("parallel", "parallel"),
        name="ada_mods",
    )(cvec, ada_w, ada_b.reshape(depth, 1, n))


class _Rows:
    def __init__(self, batch, seq, ctx_len):
        self.batch, self.seq, self.ctx_len = batch, seq, ctx_len
        self.n_lat = batch * seq
        self.n_ctx = batch * ctx_len
        self.m = self.n_lat + self.n_ctx

    def row_tile(self, prefs):
        return _pick(math.gcd(self.seq, self.n_ctx), prefs)

    def gid(self, i, tm):
        r0 = i * tm
        return jnp.where(r0 < self.n_lat, r0 // self.seq, self.batch)


def _mod_spec(rows, which, tm, d):
    return pl.BlockSpec((None, None, 1, d), lambda i: (which, rows.gid(i, tm), 0, 0))


def _norm_mod(x, gn, sh, sc):
    y = x * lax.rsqrt(jnp.mean(x * x, axis=-1, keepdims=True) + EPS) * gn
    return y * (1.0 + sc) + sh


def _norm_mod_kernel(x_ref, gn_ref, sh_ref, sc_ref, o_ref):
    o_ref[...] = _norm_mod(x_ref[...], gn_ref[...], sh_ref[...], sc_ref[...]).astype(o_ref.dtype)


def _norm_mod_router_kernel(x_ref, gn_ref, sh_ref, sc_ref, rw_ref, o_ref, ei_ref, ew_ref, *, n_experts):
    u = _norm_mod(x_ref[...], gn_ref[...], sh_ref[...], sc_ref[...])
    o_ref[...] = u
    logits = _dot(u, rw_ref[...], HIGHEST)
    lane_i = lax.broadcasted_iota(I32, logits.shape, 1)
    lane = lane_i.astype(F32)
    lg = jnp.where(lane_i < n_experts, logits, -jnp.inf)
    m1 = jnp.max(lg, axis=-1, keepdims=True)
    i1 = jnp.min(jnp.where(lg == m1, lane, float(LANES)), axis=-1, keepdims=True)
    lg2 = jnp.where(lane == i1, -jnp.inf, lg)
    m2 = jnp.max(lg2, axis=-1, keepdims=True)
    i2 = jnp.min(jnp.where(lg2 == m2, lane, float(LANES)), axis=-1, keepdims=True)
    e = jnp.exp(m2 - m1)
    w1 = 1.0 / (1.0 + e)
    w2 = e / (1.0 + e)
    ei_ref[...] = jnp.where(lane_i == 0, i1, jnp.where(lane_i == 1, i2, 0.0)).astype(I32)
    ew_ref[...] = jnp.where(lane_i == 0, w1, jnp.where(lane_i == 1, w2, 0.0))


def _norm_mod_call(rows, x, gn, mods, which_sh, which_sc, out_dtype, router_w=None):
    assert router_w is None or out_dtype == F32
    m, d = x.shape
    tm = rows.row_tile((256, 128, 64, 32, 16, 8))
    in_specs = [pl.BlockSpec((tm, d), lambda i: (i, 0)),
                pl.BlockSpec((1, d), lambda i: (0, 0)),
                _mod_spec(rows, which_sh, tm, d),
                _mod_spec(rows, which_sc, tm, d)]
    row_spec = pl.BlockSpec((tm, d), lambda i: (i, 0))
    if router_w is None:
        return pl.pallas_call(
            _norm_mod_kernel, grid=(m // tm,), in_specs=in_specs, out_specs=row_spec,
            out_shape=jax.ShapeDtypeStruct((m, d), out_dtype),
            compiler_params=_cparams("parallel"), name="norm_mod",
        )(x, gn.reshape(1, d), mods, mods)
    n_experts = router_w.shape[1]
    rw = jnp.zeros((d, LANES), F32).at[:, :n_experts].set(router_w)
    lane_spec = pl.BlockSpec((tm, LANES), lambda i: (i, 0))
    return pl.pallas_call(
        functools.partial(_norm_mod_router_kernel, n_experts=n_experts),
        grid=(m // tm,),
        in_specs=in_specs + [pl.BlockSpec((d, LANES), lambda i: (0, 0))],
        out_specs=[row_spec, lane_spec, lane_spec],
        out_shape=[jax.ShapeDtypeStruct((m, d), F32), jax.ShapeDtypeStruct((m, LANES), I32),
                   jax.ShapeDtypeStruct((m, LANES), F32)],
        compiler_params=_cparams("parallel"), name="norm_mod_router",
    )(x, gn.reshape(1, d), mods, mods, rw)


def _final_norm_kernel(x_ref, gn_ref, o_ref):
    x = x_ref[...]
    o_ref[...] = x * lax.rsqrt(jnp.mean(x * x, axis=-1, keepdims=True) + EPS) * gn_ref[...]


def _final_norm(rows, x, gn):
    d = x.shape[1]
    tm = rows.row_tile((256, 128, 64, 32, 16, 8))
    return pl.pallas_call(
        _final_norm_kernel, grid=(rows.n_lat // tm,),
        in_specs=[pl.BlockSpec((tm, d), lambda i: (i, 0)), pl.BlockSpec((1, d), lambda i: (0, 0))],
        out_specs=pl.BlockSpec((tm, d), lambda i: (i, 0)),
        out_shape=jax.ShapeDtypeStruct((rows.n_lat, d), F32),
        compiler_params=_cparams("parallel"), name="final_norm",
    )(x, gn.reshape(1, d))


def _stash_weights(pairs):
    @pl.when(pl.program_id(1) == 0)
    def _():
        for w_ref, wbuf in pairs:
            wbuf[...] = w_ref[...].astype(BF16)


def _w_spec(kdim, tn, layer, col_blk0, w_rows=False):
    if w_rows:
        return pl.BlockSpec((None, tn, kdim), lambda j, i: (layer, col_blk0 + j, 0))
    return pl.BlockSpec((None, kdim, tn), lambda j, i: (layer, 0, col_blk0 + j))


class _SideCast:
    def __init__(self, src, row0, rows, steps):
        self.src = src
        self.cols = src.shape[1]
        self.n_blocks = next(nb for nb in range(min(steps, rows // BF16_ROWS), 0, -1)
                             if rows % nb == 0 and (rows // nb) % BF16_ROWS == 0)
        self.block_rows = rows // self.n_blocks
        assert row0 % self.block_rows == 0
        self.blk0 = row0 // self.block_rows
        self.out_shape = jax.ShapeDtypeStruct((rows, self.cols), BF16)

    def specs(self, n_inner):
        step = lambda j, i: jnp.minimum(j * n_inner + i, self.n_blocks - 1)
        blk = (self.block_rows, self.cols)
        return (pl.BlockSpec(blk, lambda j, i: (self.blk0 + step(j, i), 0)),
                pl.BlockSpec(blk, lambda j, i: (step(j, i), 0)))


def _mm_kernel(*refs, act, residual, side, w_rows):
    a_ref, w_ref = refs[:2]
    x_ref, g_ref = refs[2:4] if residual else (None, None)
    pos = 4 if residual else 2
    if side:
        side_in, o_ref, side_out, wbuf = refs[pos:pos + 4]
        side_out[...] = side_in[...].astype(BF16)
    else:
        o_ref, wbuf = refs[pos:pos + 2]
    _stash_weights([(w_ref, wbuf)])
    r = _dot_nt(a_ref[...], wbuf[...]) if w_rows else _dot(a_ref[...], wbuf[...])
    if act == "sigmoid":
        r = jax.nn.sigmoid(r)
    if residual:
        r = x_ref[...] + g_ref[...] * r
    o_ref[...] = r.astype(o_ref.dtype)


def _mm_tiles(m, kdim, n, rows):
    tn = _pick(n, (1024, 512, 256, 128) if kdim <= 2048 else (512, 256, 128))
    row_prefs = (1024, 512, 256, 128, 64, 32, 16, 8) if kdim <= 2048 else (512, 256, 128, 64, 32, 16, 8)
    tm = _pick(m, row_prefs) if rows is None else rows.row_tile(row_prefs)
    return tm, tn


def _matmul(a, w, layer, col0, n, out_dtype, *, act=None, res=None, rows=None, mods=None, which=None,
            side=None, w_rows=False):
    m, kdim = a.shape
    tm, tn = _mm_tiles(m, kdim, n, rows)
    assert col0 % tn == 0
    residual = res is not None
    in_specs = [pl.BlockSpec((tm, kdim), lambda j, i: (i, 0)), _w_spec(kdim, tn, layer, col0 // tn, w_rows)]
    args = [a, w]
    if residual:
        in_specs += [pl.BlockSpec((tm, tn), lambda j, i: (i, j)),
                     pl.BlockSpec((None, None, 1, tn), lambda j, i: (which, rows.gid(i, tm), 0, j))]
        args += [res, mods]
    out_specs = [pl.BlockSpec((tm, tn), lambda j, i: (i, j))]
    out_shape = [jax.ShapeDtypeStruct((m, n), out_dtype)]
    if side is not None:
        side_in, side_out = side.specs(m // tm)
        in_specs.append(side_in)
        args.append(side.src)
        out_specs.append(side_out)
        out_shape.append(side.out_shape)
    outs = pl.pallas_call(
        functools.partial(_mm_kernel, act=act, residual=residual, side=side is not None, w_rows=w_rows),
        grid=(n // tn, m // tm),
        in_specs=in_specs,
        out_specs=out_specs,
        out_shape=out_shape,
        scratch_shapes=[pltpu.VMEM((tn, kdim) if w_rows else (kdim, tn), BF16)],
        compiler_params=_cparams("parallel", "arbitrary"),
        name="matmul",
    )(*args)
    return outs[0] if side is None else outs


def _swiglu_kernel(*refs, side):
    if side:
        a_ref, w1_ref, w3_ref, side_in, o_ref, side_out, w1buf, w3buf = refs
        side_out[...] = side_in[...].astype(BF16)
    else:
        a_ref, w1_ref, w3_ref, o_ref, w1buf, w3buf = refs
    _stash_weights([(w1_ref, w1buf), (w3_ref, w3buf)])
    a = a_ref[...]
    o_ref[...] = (_silu(_dot(a, w1buf[...])) * _dot(a, w3buf[...])).astype(o_ref.dtype)


def _swiglu_tiles(m, f):
    return _pick(m, (1024, 512, 256, 128, 64, 32, 16, 8)), _pick(f, (512, 256, 128))


def _swiglu_up(a, w1, w3, layer, side=None):
    m, d = a.shape
    f = w1.shape[2]
    tm, tn = _swiglu_tiles(m, f)
    in_specs = [pl.BlockSpec((tm, d), lambda j, i: (i, 0)), _w_spec(d, tn, layer, 0), _w_spec(d, tn, layer, 0)]
    args = [a, w1, w3]
    out_specs = [pl.BlockSpec((tm, tn), lambda j, i: (i, j))]
    out_shape = [jax.ShapeDtypeStruct((m, f), BF16)]
    if side is not None:
        side_in, side_out = side.specs(m // tm)
        in_specs.append(side_in)
        args.append(side.src)
        out_specs.append(side_out)
        out_shape.append(side.out_shape)
    outs = pl.pallas_call(
        functools.partial(_swiglu_kernel, side=side is not None), grid=(f // tn, m // tm),
        in_specs=in_specs, out_specs=out_specs, out_shape=out_shape,
        scratch_shapes=[pltpu.VMEM((d, tn), BF16), pltpu.VMEM((d, tn), BF16)],
        compiler_params=_cparams("parallel", "arbitrary"), name="swiglu_up",
    )(*args)
    return outs[0] if side is None else outs


def _up_merge_kernel(ha_ref, hb_ref, wa_ref, wb_ref, sga_ref, sgb_ref, o_ref, wabuf, wbbuf):
    _stash_weights([(wa_ref, wabuf), (wb_ref, wbbuf)])
    up_a = _dot(ha_ref[...], wabuf[...])
    up_b = _dot(hb_ref[...], wbbuf[...])
    o_ref[...] = (sga_ref[...] * up_a + sgb_ref[...] * up_b).astype(o_ref.dtype)


def _up_merge(ha, hb, wa, wb, layer, sg):
    m, va = ha.shape
    vb = hb.shape[1]
    d = wa.shape[2]
    tm = _pick(m, (512, 256, 128, 64, 32, 16, 8))
    tn = _pick(d, (1024, 512, 256, 128))
    nj = d // tn
    return pl.pallas_call(
        _up_merge_kernel, grid=(nj, m // tm),
        in_specs=[pl.BlockSpec((tm, va), lambda j, i: (i, 0)),
                  pl.BlockSpec((tm, vb), lambda j, i: (i, 0)),
                  _w_spec(va, tn, layer, 0), _w_spec(vb, tn, layer, 0),
                  pl.BlockSpec((tm, tn), lambda j, i: (i, j)),
                  pl.BlockSpec((tm, tn), lambda j, i: (i, j + nj))],
        out_specs=pl.BlockSpec((tm, tn), lambda j, i: (i, j)),
        out_shape=jax.ShapeDtypeStruct((m, d), BF16),
        scratch_shapes=[pltpu.VMEM((va, tn), BF16), pltpu.VMEM((vb, tn), BF16)],
        compiler_params=_cparams("parallel", "arbitrary"), name="up_merge",
    )(ha, hb, wa, wb, sg, sg)


def _conv_taps(xs, w, dy, not_first, not_last):
    n = xs.shape[0]
    left = jnp.where(not_first, pltpu.roll(xs, 1, 0), 0.0)
    right = jnp.where(not_last, pltpu.roll(xs, n - 1, 0), 0.0)
    return w[3 * dy:3 * dy + 1] * left + w[3 * dy + 1:3 * dy + 2] * xs + w[3 * dy + 2:3 * dy + 3] * right


def _conv_kernel(up_ref, x_ref, dn_ref, w_ref, s_ref, o_ref, pad_ref, *, rows, rb, strip):
    r0 = pl.program_id(0) * rb
    w = w_ref[...]

    @pl.when(r0 < rows.n_lat)
    def _():
        tc = x_ref.shape[-1]
        at_start = lax.rem(r0, rows.seq) == 0
        at_end = lax.rem(r0 + rb, rows.seq) == 0
        pad_ref[pl.ds(0, GRID_W), :] = jnp.where(at_start, 0.0, up_ref[...])
        pad_ref[pl.ds(GRID_W + rb, GRID_W), :] = jnp.where(at_end, 0.0, dn_ref[...])
        pad_ref[pl.ds(GRID_W, rb), :] = x_ref[...]
        col = lax.broadcasted_iota(I32, (strip, tc), 0) % GRID_W
        not_first = col != 0
        not_last = col != GRID_W - 1
        for s in range(rb // strip):
            acc = jnp.zeros((strip, tc), F32)
            for dy in range(3):
                xs = pad_ref[pl.ds(s * strip + dy * GRID_W, strip), :]
                acc = acc + _conv_taps(xs, w, dy, not_first, not_last)
            o_ref[pl.ds(s * strip, strip), :] = _silu(acc) * s_ref[...]

    @pl.when(r0 >= rows.n_lat)
    def _():
        x = x_ref[...]
        pos = lax.broadcasted_iota(I32, x.shape, 0) % rows.ctx_len
        acc = _conv_taps(x, w, 1, pos != 0, pos != rows.ctx_len - 1)
        o_ref[...] = _silu(acc) * s_ref[...]


def _conv_silu(rows, p_qk, conv_w, scale):
    m, c = p_qk.shape
    tc = _pick(c, (512, 256, 128))
    rb = rows.n_ctx
    strip = _pick(rb, (512, 256, 128, 64))
    per = rb // GRID_W
    last = m // GRID_W - 1
    return pl.pallas_call(
        functools.partial(_conv_kernel, rows=rows, rb=rb, strip=strip),
        grid=(m // rb, c // tc),
        in_specs=[pl.BlockSpec((GRID_W, tc), lambda i, j: (jnp.maximum(i * per - 1, 0), j)),
                  pl.BlockSpec((rb, tc), lambda i, j: (i, j)),
                  pl.BlockSpec((GRID_W, tc), lambda i, j: (jnp.minimum((i + 1) * per, last), j)),
                  pl.BlockSpec((9, tc), lambda i, j: (0, j)),
                  pl.BlockSpec((1, tc), lambda i, j: (0, j))],
        out_specs=pl.BlockSpec((rb, tc), lambda i, j: (i, j)),
        out_shape=jax.ShapeDtypeStruct((m, c), F32),
        scratch_shapes=[pltpu.VMEM((rb + 2 * GRID_W, tc), F32)],
        compiler_params=_cparams("parallel", "parallel"), name="conv_silu",
    )(p_qk, p_qk, p_qk, conv_w.reshape(9, c), scale)


def _causal_masks(n):
    t = lax.broadcasted_iota(I32, (n, n), 0)
    s = lax.broadcasted_iota(I32, (n, n), 1)
    return t, s, (s <= t, s >= t)


def _block_ref_rows(x, blk, row):
    n, c = x.shape
    if blk >= SUBLANES:
        x3 = x.reshape(n // blk, blk, c)
        return jnp.broadcast_to(x3[:, row:row + 1, :], x3.shape).reshape(n, c)
    x3 = x.reshape(n // SUBLANES, SUBLANES, c)
    sub = lax.broadcasted_iota(I32, x3.shape, 1)
    out = jnp.zeros_like(x3)
    for g in range(SUBLANES // blk):
        r = g * blk + row
        out = jnp.where(sub // blk == g, jnp.broadcast_to(x3[:, r:r + 1, :], x3.shape), out)
    return out.reshape(n, c)


def _gla_direction(d, q, k, v, lr, wg, bg, s_ref, causal, lev_blk):
    n = q.shape[0]
    hn = n // 2
    lh, lm, _ = _split3(lr)
    wh, wm, _ = _split3(wg)
    z = _dot(jnp.concatenate([lh, lh, lm], axis=1), jnp.concatenate([wh, wm, wh], axis=0)) + bg
    la = jax.nn.log_sigmoid(z) * (LOG2_E / GLA_GATE_NORMALIZER)
    tri = causal.astype(BF16)
    c3 = _dot(tri, jnp.concatenate(_split3(la), axis=1))
    nk = la.shape[1]
    cum = c3[:, :nk] + c3[:, nk:2 * nk] + c3[:, 2 * nk:]
    total = cum[n - 1:n] if d == 0 else cum[0:1]
    state = s_ref[...]
    out = _dot((q * jnp.exp2(cum)).astype(BF16), state.astype(BF16))
    qb = q.astype(BF16)
    kb = k.astype(BF16)
    lo, hi = slice(0, hn), slice(hn, n)
    diag = [jnp.where(lev_blk == -1, _dot_nt(qb[h], kb[h]).astype(BF16), jnp.zeros((), BF16)) for h in (lo, hi)]
    top = n.bit_length() - 2
    for l in range(top + 1):
        half = 1 << l
        ref = _block_ref_rows(cum, 2 * half, half - 1 if d == 0 else half)
        e = jnp.exp2(-jnp.abs(cum - ref)).astype(BF16)
        qe = qb * e
        ke = kb * e
        if l < top:
            diag = [jnp.where(lev_blk == l, _dot_nt(qe[h], ke[h]).astype(BF16), sc)
                    for h, sc in zip((lo, hi), diag)]
        else:
            cross = (_dot_nt(qe[hi], ke[lo]) if d == 0 else _dot_nt(qe[lo], ke[hi])).astype(BF16)
    zero = jnp.zeros((hn, hn), BF16)
    upper, lower = ([diag[0], zero], [cross, diag[1]]) if d == 0 else ([diag[0], cross], [zero, diag[1]])
    scores = jnp.concatenate([jnp.concatenate(upper, axis=1), jnp.concatenate(lower, axis=1)], axis=0)
    out = out + _dot(scores, v)
    k_out = (k * jnp.exp2(total - cum)).astype(BF16)
    dk = state.shape[0]
    et = jnp.broadcast_to(jnp.exp2(total), (dk, dk)).T
    scale = jnp.concatenate([et] * (state.shape[1] // dk), axis=1)
    s_ref[...] = scale * state + _dot_tn(k_out, v)
    return out


def _gla_kernel(qf, kf, vf, smf, qb, kb, vb, smb, wg_ref, bg_ref, of_ref, ob_ref, sf_ref, sb_ref):
    @pl.when(pl.program_id(1) == 0)
    def _():
        sf_ref[...] = jnp.zeros_like(sf_ref)
        sb_ref[...] = jnp.zeros_like(sb_ref)

    n = qf.shape[0]
    _, _, causal = _causal_masks(n)
    t_idx, s_idx, causal_blk = _causal_masks(n // 2)
    lev = 31 - lax.clz(t_idx ^ s_idx)
    dirs = ((qf, kf, vf, smf, of_ref, sf_ref), (qb, kb, vb, smb, ob_ref, sb_ref))
    for d, (q, k, v, sm, o_ref, s_ref) in enumerate(dirs):
        lr = sm[:, d * GLA_RANK:(d + 1) * GLA_RANK]
        lev_blk = jnp.where(causal_blk[d], lev, -2).astype(BF16)
        for hd in range(GLA_HEADS):
            ks = slice(hd * GLA_DK, (hd + 1) * GLA_DK)
            vs = slice(hd * GLA_DV, (hd + 1) * GLA_DV)
            o_ref[:, vs] = _gla_direction(d, q[:, ks], k[:, ks], v[:, vs], lr, wg_ref[d, :, ks], bg_ref[d, :, ks],
                                          s_ref.at[hd], causal[d], lev_blk)


def _soft_cap(z):
    return GATE_SOFT_CAP * jnp.tanh(z * (1.0 / GATE_SOFT_CAP))


def _mlstm_direction(d, head, q, k, v, g_col, g_row, c_ref, m_ref, causal):
    n = q.shape[0]
    tri = causal.astype(BF16)
    i_idx = 2 * MLSTM_HEADS * d + head
    f_idx = i_idx + MLSTM_HEADS
    sel_rows = lax.broadcasted_iota(I32, (3 * N_GATE_B, LANES), 0) % N_GATE_B
    rep = lambda a, idx: _dot(jnp.concatenate(_split3(a), axis=1), (sel_rows == idx).astype(BF16))
    i_rep = rep(g_col, i_idx)
    f_rep = rep(jax.nn.log_sigmoid(g_col), f_idx)
    c3 = _dot(tri, jnp.concatenate(_split3(f_rep), axis=1))
    cum = c3[:, :LANES] + c3[:, LANES:2 * LANES] + c3[:, 2 * LANES:]
    r3 = _dot_nt(jnp.concatenate(_split3(jax.nn.log_sigmoid(g_row)), axis=0), tri)
    cum_r_all = r3[:N_GATE_B] + r3[N_GATE_B:2 * N_GATE_B] + r3[2 * N_GATE_B:]
    sub = lax.broadcasted_iota(I32, (N_GATE_B, 1), 0)
    pick_r = lambda a, idx: jnp.sum(jnp.where(sub == idx, a, 0.0), axis=0, keepdims=True)
    i_r, cum_r = pick_r(g_row, i_idx), pick_r(cum_r_all, f_idx)
    wide = lambda a, reps: jnp.concatenate([a] * reps, axis=1)
    nt = n // LANES
    total = cum[n - 1:n] if d == 0 else cum[0:1]
    m_prev = m_ref[...]
    dlog = jnp.where(causal, wide(cum, nt) - cum_r + i_r, NEG_BIG)
    inter_log = cum + m_prev
    row_max = dlog[:, :LANES]
    for t in range(1, nt):
        row_max = jnp.maximum(row_max, dlog[:, t * LANES:(t + 1) * LANES])
    m_t = jnp.maximum(inter_log, jnp.max(row_max, axis=1, keepdims=True))
    w_inter = jnp.exp(inter_log - m_t)
    qb16 = q.astype(BF16)
    s = _dot_nt(qb16, k.astype(BF16)) * jnp.exp(dlog - wide(m_t, nt))
    dv = v.shape[1]
    v_ext = jnp.concatenate([v, jnp.ones((n, LANES), BF16)], axis=1)
    state = c_ref[...]
    acc = wide(w_inter, dv // LANES + 1) * _dot(qb16, state.astype(BF16)) + _dot(s.astype(BF16), v_ext)
    bound = jnp.maximum(jnp.abs(acc[:, dv:]), jnp.exp(-m_t))
    out = acc[:, :dv] / wide(bound, dv // LANES)
    g = total - cum + i_rep
    m_new = jnp.maximum(total + m_prev, jnp.max(g, axis=0, keepdims=True))
    w_c = jnp.exp(total + m_prev - m_new)
    w_k = jnp.exp(g - m_new)
    c_ref[...] = wide(w_c, dv // LANES + 1) * state + _dot_tn((k * w_k).astype(BF16), v_ext)
    m_ref[...] = m_new
    return out


def _mlstm_kernel(qf, kf, vf, gcf, grf, qb, kb, vb, gcb, grb, brow_ref, bcol_ref,
                  of_ref, ob_ref, cf_ref, cb_ref, mf_ref, mb_ref):
    @pl.when(pl.program_id(1) == 0)
    def _():
        cf_ref[...] = jnp.zeros_like(cf_ref)
        cb_ref[...] = jnp.zeros_like(cb_ref)
        mf_ref[...] = jnp.zeros_like(mf_ref)
        mb_ref[...] = jnp.zeros_like(mb_ref)

    n = qf.shape[0]
    _, _, causal = _causal_masks(n)
    g0 = 2 * GLA_RANK
    dirs = ((qf, kf, vf, gcf, grf, of_ref, cf_ref, mf_ref), (qb, kb, vb, gcb, grb, ob_ref, cb_ref, mb_ref))
    for d, (q, k, v, gc, gr, o_ref, c_ref, m_ref) in enumerate(dirs):
        g_col = _soft_cap(gc[:, g0:g0 + N_GATE_B] + brow_ref[...])
        g_row = _soft_cap(gr[...] + bcol_ref[...])
        for hd in range(MLSTM_HEADS):
            qs = slice(hd * MLSTM_DQK, (hd + 1) * MLSTM_DQK)
            vs = slice(hd * MLSTM_DV, (hd + 1) * MLSTM_DV)
            o_ref[:, vs] = _mlstm_direction(d, hd, q[:, qs], k[:, qs], v[:, vs], g_col, g_row,
                                            c_ref.at[hd], m_ref.at[hd], causal[d])


def _scan_row_maps(rows, chunk):
    lat_chunks = rows.seq // chunk
    ctx_chunks = rows.ctx_len // chunk
    ctx0 = rows.n_lat // chunk

    def fwd(b, s):
        return jnp.where(s < ctx_chunks, ctx0 + b * ctx_chunks + s, b * lat_chunks + (s - ctx_chunks))

    def bwd(b, s):
        return jnp.where(s < ctx_chunks, ctx0 + b * ctx_chunks + (ctx_chunks - 1 - s),
                         b * lat_chunks + (lat_chunks - 1 - (s - ctx_chunks)))

    return fwd, bwd, ctx_chunks + lat_chunks


def _gla_scan(rows, qk, v, small, gate_w, gate_b):
    m = qk.shape[0]
    n = SCAN_CHUNK
    fwd, bwd, steps = _scan_row_maps(rows, n)

    def specs(rmap):
        return [pl.BlockSpec((n, QA), lambda b, s: (rmap(b, s), 0)),
                pl.BlockSpec((n, QA), lambda b, s: (rmap(b, s), 1)),
                pl.BlockSpec((n, V_A), lambda b, s: (rmap(b, s), 0)),
                pl.BlockSpec((n, LANES), lambda b, s: (rmap(b, s), 0))]

    out_spec = lambda rmap: pl.BlockSpec((n, V_A), lambda b, s: (rmap(b, s), 0))
    out_sds = jax.ShapeDtypeStruct((m, V_A), F32)
    state = pltpu.VMEM((GLA_HEADS, GLA_DK, GLA_DV), F32)
    return pl.pallas_call(
        _gla_kernel,
        grid=(rows.batch, steps),
        in_specs=specs(fwd) + specs(bwd) + [
            pl.BlockSpec((2, GLA_RANK, QA), lambda b, s: (0, 0, 0)),
            pl.BlockSpec((2, 1, QA), lambda b, s: (0, 0, 0))],
        out_specs=[out_spec(fwd), out_spec(bwd)],
        out_shape=[out_sds, out_sds],
        scratch_shapes=[state, state],
        compiler_params=_cparams("parallel", "arbitrary"), name="gla_scan",
    )(qk, qk, v, small, qk, qk, v, small, gate_w, gate_b.reshape(2, 1, QA))


def _mlstm_scan(rows, qk, v, small, gates_t, gate_b):
    m = qk.shape[0]
    n = SCAN_CHUNK
    fwd, bwd, steps = _scan_row_maps(rows, n)
    q0 = 2 * QA // QB
    v0 = V_A // V_B

    def specs(rmap):
        return [pl.BlockSpec((n, QB), lambda b, s: (rmap(b, s), q0)),
                pl.BlockSpec((n, QB), lambda b, s: (rmap(b, s), q0 + 1)),
                pl.BlockSpec((n, V_B), lambda b, s: (rmap(b, s), v0)),
                pl.BlockSpec((n, LANES), lambda b, s: (rmap(b, s), 0)),
                pl.BlockSpec((N_GATE_B, n), lambda b, s: (0, rmap(b, s)))]

    out_spec = lambda rmap: pl.BlockSpec((n, V_B), lambda b, s: (rmap(b, s), 0))
    out_sds = jax.ShapeDtypeStruct((m, V_B), F32)
    state = pltpu.VMEM((MLSTM_HEADS, MLSTM_DQK, MLSTM_DV + LANES), F32)
    stab = pltpu.VMEM((MLSTM_HEADS, 1, LANES), F32)
    return pl.pallas_call(
        _mlstm_kernel,
        grid=(rows.batch, steps),
        in_specs=specs(fwd) + specs(bwd) + [
            pl.BlockSpec((1, N_GATE_B), lambda b, s: (0, 0)),
            pl.BlockSpec((N_GATE_B, 1), lambda b, s: (0, 0))],
        out_specs=[out_spec(fwd), out_spec(bwd)],
        out_shape=[out_sds, out_sds],
        scratch_shapes=[state, state, stab, stab],
        compiler_params=_cparams("parallel", "arbitrary"), name="mlstm_scan",
    )(qk, qk, v, small, gates_t, qk, qk, v, small, gates_t,
      gate_b.reshape(1, N_GATE_B), gate_b.reshape(N_GATE_B, 1))


def _mix_prep_kernel(oaf, oab, obf, obb, ro_ref, gna_ref, gnb_ref, ha_ref, hb_ref):
    branches = ((oaf, oab, gna_ref, ha_ref, 0, GLA_HEADS, GLA_DV, _silu),
                (obf, obb, gnb_ref, hb_ref, V_A, MLSTM_HEADS, MLSTM_DV, jax.nn.sigmoid))
    for of, ob, gn_ref, h_ref, off, heads, dv, gate_fn in branches:
        for h in range(heads):
            sl = slice(h * dv, (h + 1) * dv)
            o = of[:, sl] + ob[:, sl]
            y = o * lax.rsqrt(jnp.mean(o * o, axis=-1, keepdims=True) + EPS) * gn_ref[:, sl]
            gate = gate_fn(ro_ref[:, off + h * dv:off + (h + 1) * dv])
            h_ref[:, sl] = (y * gate).astype(h_ref.dtype)


def _mix_prep(oaf, oab, obf, obb, ro, gna, gnb):
    m = ro.shape[0]
    tm = _pick(m, (256, 128, 64, 32, 16, 8))
    row = lambda c: pl.BlockSpec((tm, c), lambda i: (i, 0))
    vec = lambda c: pl.BlockSpec((1, c), lambda i: (0, 0))
    return pl.pallas_call(
        _mix_prep_kernel, grid=(m // tm,),
        in_specs=[row(V_A), row(V_A), row(V_B), row(V_B), row(V_A + V_B), vec(V_A), vec(V_B)],
        out_specs=[row(V_A), row(V_B)],
        out_shape=[jax.ShapeDtypeStruct((m, V_A), BF16), jax.ShapeDtypeStruct((m, V_B), BF16)],
        compiler_params=_cparams("parallel"), name="mix_prep",
    )(oaf, oab, obf, obb, ro, gna.reshape(1, V_A), gnb.reshape(1, V_B))


def _expert_up_kernel(te_ref, na_ref, src0_ref, srcn_ref, v_hbm, w1_ref, w3_ref, o_ref, xbuf, abuf, sem,
                      *, tm, issue_steps):
    del te_ref
    i = pl.program_id(0)
    j = pl.program_id(1)
    slot = lax.rem(i, 2)
    per = tm // issue_steps

    def row_copy(src_ref, grp, t, s):
        return pltpu.make_async_copy(v_hbm.at[pl.ds(src_ref[0, grp * per + t], 1)],
                                     xbuf.at[s, grp, pl.ds(t, 1)], sem.at[s])

    @pl.when(jnp.logical_and(i == 0, j == 0))
    def _():
        def start(grp, carry):
            for t in range(per):
                row_copy(src0_ref, grp, t, 0).start()
            return carry

        lax.fori_loop(0, issue_steps, start, 0)

    @pl.when(j == 0)
    def _():
        def wait(r, carry):
            row_copy(src0_ref, 0, 0, slot).wait()
            return carry

        lax.fori_loop(0, tm, wait, 0, unroll=32)
        abuf[...] = xbuf[slot].reshape(tm, abuf.shape[1]).astype(BF16)

    @pl.when(jnp.logical_and(i + 1 < pl.num_programs(0), j < issue_steps))
    def _():
        for t in range(per):
            row_copy(srcn_ref, j, t, 1 - slot).start()

    @pl.when(i < na_ref[0])
    def _():
        a = abuf[...]
        o_ref[...] = (_silu(_dot(a, w1_ref[...])) * _dot(a, w3_ref[...])).astype(o_ref.dtype)

    @pl.when(i >= na_ref[0])
    def _():
        o_ref[...] = jnp.zeros_like(o_ref)


def _expert_up(v, src, w1, w3, layer, tile_expert, n_active, tm):
    n_tiles = src.shape[0]
    d = v.shape[1]
    f = w1.shape[3]
    tn = _pick(f, (1408, 512, 256, 128))
    nj = f // tn
    issue_steps = max(s for s in (1, 2, 4, 8) if s <= nj)
    wmap = lambda i, j, te, na: (layer, te[i], 0, jnp.where(i < na[0], j, 0))
    return pl.pallas_call(
        functools.partial(_expert_up_kernel, tm=tm, issue_steps=issue_steps),
        grid_spec=pltpu.PrefetchScalarGridSpec(
            num_scalar_prefetch=2, grid=(n_tiles, nj),
            in_specs=[pl.BlockSpec((None, 1, tm), lambda i, j, te, na: (0, 0, 0), memory_space=pltpu.SMEM),
                      pl.BlockSpec((None, 1, tm), lambda i, j, te, na: (jnp.minimum(i + 1, n_tiles - 1), 0, 0),
                                   memory_space=pltpu.SMEM),
                      pl.BlockSpec(memory_space=pl.ANY),
                      pl.BlockSpec((None, None, d, tn), wmap),
                      pl.BlockSpec((None, None, d, tn), wmap)],
            out_specs=pl.BlockSpec((tm, tn), lambda i, j, te, na: (i, j)),
            scratch_shapes=[pltpu.VMEM((2, issue_steps, tm // issue_steps, d), F32), pltpu.VMEM((tm, d), BF16),
                            pltpu.SemaphoreType.DMA((2,))]),
        out_shape=jax.ShapeDtypeStruct((n_tiles * tm, f), BF16),
        compiler_params=_cparams("arbitrary", "arbitrary"), name="expert_up",
    )(tile_expert, n_active, src, src, v, w1, w3)


def _expert_down_kernel(te_ref, na_ref, a_ref, w_ref, o_ref):
    del te_ref
    i = pl.program_id(0)

    @pl.when(i < na_ref[0])
    def _():
        o_ref[...] = _dot(a_ref[...], w_ref[...])

    @pl.when(i >= na_ref[0])
    def _():
        o_ref[...] = jnp.zeros_like(o_ref)


def _expert_down(hs, w2, layer, tile_expert, n_active, tm):
    p, f = hs.shape
    d = w2.shape[3]
    tn = _pick(d, (1024, 512, 256, 128))
    return pl.pallas_call(
        _expert_down_kernel,
        grid_spec=pltpu.PrefetchScalarGridSpec(
            num_scalar_prefetch=2, grid=(p // tm, d // tn),
            in_specs=[pl.BlockSpec((tm, f), lambda i, j, te, na: (jnp.where(i < na[0], i, 0), 0)),
                      pl.BlockSpec((None, None, f, tn),
                                   lambda i, j, te, na: (layer, te[i], 0, jnp.where(i < na[0], j, 0)))],
            out_specs=pl.BlockSpec((tm, tn), lambda i, j, te, na: (i, j))),
        out_shape=jax.ShapeDtypeStruct((p, d), F32),
        compiler_params=_cparams("parallel", "parallel"), name="expert_down",
    )(tile_expert, n_active, hs, w2)


def _moe_combine_kernel(pos_ref, posn_ref, ys_hbm, x_ref, g_ref, w_ref, fn_ref, o_ref, buf, sem,
                        *, tile, final_norm):
    i = pl.program_id(0)
    slot = lax.rem(i, 2)

    def row_copy(p_ref, k, grp, t, s):
        return pltpu.make_async_copy(ys_hbm.at[pl.ds(p_ref[0, k * tile + grp * COPY_GROUP + t], 1)],
                                     buf.at[s, k, grp, pl.ds(t, 1)], sem.at[s])

    def start_tile(p_ref, s):
        for k in range(TOP_K):
            def start(grp, carry):
                for t in range(COPY_GROUP):
                    row_copy(p_ref, k, grp, t, s).start()
                return carry

            lax.fori_loop(0, tile // COPY_GROUP, start, 0)

    @pl.when(i == 0)
    def _():
        start_tile(pos_ref, 0)

    @pl.when(i + 1 < pl.num_programs(0))
    def _():
        start_tile(posn_ref, 1 - slot)

    def wait(r, carry):
        row_copy(pos_ref, 0, 0, 0, slot).wait()
        return carry

    lax.fori_loop(0, TOP_K * tile, wait, 0, unroll=32)
    w = w_ref[...]
    d = x_ref.shape[1]
    y = w[:, 0:1] * buf[slot, 0].reshape(tile, d) + w[:, 1:2] * buf[slot, 1].reshape(tile, d)
    r = x_ref[...] + g_ref[...] * y
    if final_norm:
        r = r * lax.rsqrt(jnp.mean(r * r, axis=-1, keepdims=True) + EPS) * fn_ref[...]
    o_ref[...] = r


def _moe_combine(rows, ys, pos, x, mods, which, ew, m, final_gn):
    d = x.shape[1]
    tile = rows.row_tile((256, 128, 64, 32, 16, 8))
    n = m // tile
    pos_spec = lambda imap: pl.BlockSpec((None, 1, TOP_K * tile), imap, memory_space=pltpu.SMEM)
    final_norm = final_gn is not None
    fn = final_gn.reshape(1, d) if final_norm else jnp.ones((1, d), F32)
    return pl.pallas_call(
        functools.partial(_moe_combine_kernel, tile=tile, final_norm=final_norm),
        grid=(n,),
        in_specs=[pos_spec(lambda i: (i, 0, 0)),
                  pos_spec(lambda i: (jnp.minimum(i + 1, n - 1), 0, 0)),
                  pl.BlockSpec(memory_space=pl.ANY),
                  pl.BlockSpec((tile, d), lambda i: (i, 0)),
                  _mod_spec(rows, which, tile, d),
                  pl.BlockSpec((tile, LANES), lambda i: (i, 0)),
                  pl.BlockSpec((1, d), lambda i: (0, 0))],
        out_specs=pl.BlockSpec((tile, d), lambda i: (i, 0)),
        out_shape=jax.ShapeDtypeStruct((m, d), F32),
        scratch_shapes=[pltpu.VMEM((2, TOP_K, tile // COPY_GROUP, COPY_GROUP, d), F32),
                        pltpu.SemaphoreType.DMA((2,))],
        compiler_params=_cparams("arbitrary"), name="moe_combine",
    )(pos, pos, ys, x, mods, ew, fn)


def _moe_dispatch(e_idx, n_experts, tm, combine_tile):
    m = e_idx.shape[0]
    ex = jnp.concatenate([e_idx[:, 0], e_idx[:, 1]])
    onehot = (ex[:, None] == jnp.arange(n_experts, dtype=I32)[None, :]).astype(I32)
    rank = jnp.sum((jnp.cumsum(onehot, axis=0) - onehot) * onehot, axis=1)
    counts = jnp.sum(onehot, axis=0)
    padded = (counts + tm - 1) // tm * tm
    ends = jnp.cumsum(padded)
    dest = (ends - padded)[ex] + rank
    n_tiles = (TOP_K * m + n_experts * (tm - 1)) // tm
    tok = jnp.concatenate([jnp.arange(m, dtype=I32)] * TOP_K)
    src = jnp.zeros((n_tiles * tm,), I32).at[dest].set(tok).reshape(n_tiles, 1, tm)
    tile_start = jnp.arange(n_tiles, dtype=I32) * tm
    tile_expert = jnp.minimum(jnp.sum((ends[None, :] <= tile_start[:, None]).astype(I32), axis=1), n_experts - 1)
    n_active = (ends[-1] // tm).astype(I32).reshape(1)
    pos = dest.reshape(TOP_K, m // combine_tile, combine_tile).transpose(1, 0, 2).reshape(
        m // combine_tile, 1, TOP_K * combine_tile).astype(I32)
    return src, tile_expert, n_active, pos


def _moe_ffn(rows, x, gn, mods, router_w, w1, w3, w2, layer, n_tok, final_gn):
    n_experts = w1.shape[1]
    v, e_idx, e_w = _norm_mod_call(rows, x, gn, mods, 3, 4, F32, router_w=router_w)
    tm = 512 if TOP_K * n_tok >= 8192 else 64
    combine_tile = rows.row_tile((256, 128, 64, 32, 16, 8))
    src, tile_expert, n_active, pos = _moe_dispatch(e_idx[:n_tok], n_experts, tm, combine_tile)
    hs = _expert_up(v, src, w1, w3, layer, tile_expert, n_active, tm)
    ys = _expert_down(hs, w2, layer, tile_expert, n_active, tm)
    return _moe_combine(rows, ys, pos, x, mods, 5, e_w, n_tok, final_gn)


def kernel(x, c, ctx, c_ctx, ada_w, ada_b, norm_mix, norm_ffn, w_in, conv_w, gla_gate_w, gla_gate_b,
           mlstm_gate_b, gla_out_norm, mlstm_out_norm, w_up_a, w_up_b, w_o, ffn_w1, ffn_w3, ffn_w2,
           router_w, moe_w1, moe_w3, moe_w2, norm_final):
    batch, seq, d = x.shape
    ctx_len = ctx.shape[1]
    depth = ada_w.shape[0]
    assert seq % GRID_W == 0 and seq % SCAN_CHUNK == 0 and ctx_len % SCAN_CHUNK == 0
    assert seq % (batch * ctx_len) == 0 and (batch * ctx_len) % GRID_W == 0
    rows = _Rows(batch, seq, ctx_len)

    h = jnp.concatenate([x.reshape(batch * seq, d), ctx.reshape(batch * ctx_len, d)], axis=0)

    cvec = jnp.zeros((SUBLANES * ((batch + 1 + SUBLANES - 1) // SUBLANES), d), F32)
    cvec = cvec.at[:batch].set(c).at[batch].set(c_ctx)
    mods_all = _ada_mods(cvec, ada_w, ada_b)
    mods_all = mods_all[:, :batch + 1].reshape(depth, batch + 1, 6, 1, d).transpose(0, 2, 1, 3, 4)

    qk_scale = jnp.ones((QK_COLS,), F32)
    qk_scale = qk_scale.at[:QA].set(GLA_DK ** -0.5).at[2 * QA + QB:].set(MLSTM_DQK ** -0.5).reshape(1, QK_COLS)
    c_v, c_ro = QK_COLS, QK_COLS + V_A + V_B
    c_sm = c_ro + V_A + V_B
    c_g = c_sm + 2 * GLA_RANK + N_GATE_B
    n_small = c_g - c_sm
    n_exp, d_ff = moe_w1.shape[1], moe_w1.shape[3]
    moe_w1_rows = moe_w1.reshape(-1, d_ff)
    moe_w3_rows = moe_w3.reshape(-1, d_ff)
    moe_w2_rows = moe_w2.reshape(-1, d)
    gate_tm, gate_tn = _mm_tiles(rows.m, d, 2 * d, None)
    gate_steps = (2 * d // gate_tn) * (rows.m // gate_tm)
    up_tm, up_tn = _swiglu_tiles(rows.m, ffn_w1.shape[2])
    up_steps = (ffn_w1.shape[2] // up_tn) * (rows.m // up_tm)
    w_in_t = jnp.swapaxes(w_in, 1, 2)

    for l in range(depth):
        mods = mods_all[l]
        w_sm = jnp.zeros((1, LANES, d), F32).at[0, :n_small].set(w_in_t[l, c_sm:c_g])
        w_g = w_in_t[l, c_g:][None]

        u = _norm_mod_call(rows, h, norm_mix[l], mods, 0, 1, BF16)
        p_qk = _matmul(u, w_in_t, l, 0, QK_COLS, F32, w_rows=True)
        v = _matmul(u, w_in_t, l, c_v, V_A + V_B, BF16, w_rows=True)
        ro = _matmul(u, w_in_t, l, c_ro, V_A + V_B, F32, w_rows=True)
        small = _matmul(u, w_sm, 0, 0, LANES, F32, w_rows=True)
        if l % 2 == 0 and l + 1 < depth:
            cast = _SideCast(moe_w2_rows, (l // 2) * n_exp * d_ff, n_exp * d_ff, gate_steps)
            sg, w2_bf = _matmul(u, w_g, 0, 0, 2 * d, BF16, act="sigmoid", side=cast, w_rows=True)
        elif l % 2 == 1:
            cast = _SideCast(moe_w3_rows, (l // 2) * n_exp * d, n_exp * d, gate_steps)
            sg, w3_bf = _matmul(u, w_g, 0, 0, 2 * d, BF16, act="sigmoid", side=cast, w_rows=True)
        else:
            sg = _matmul(u, w_g, 0, 0, 2 * d, BF16, act="sigmoid", w_rows=True)
        qk = _conv_silu(rows, p_qk, conv_w[l], qk_scale)
        gates_t = small[:, 2 * GLA_RANK:n_small].T
        oaf, oab = _gla_scan(rows, qk, v, small, gla_gate_w[l], gla_gate_b[l])
        obf, obb = _mlstm_scan(rows, qk, v, small, gates_t, mlstm_gate_b[l])
        ha, hb = _mix_prep(oaf, oab, obf, obb, ro, gla_out_norm[l], mlstm_out_norm[l])
        merged = _up_merge(ha, hb, w_up_a, w_up_b, l, sg)
        h = _matmul(merged, w_o, l, 0, d, F32, res=h, rows=rows, mods=mods, which=2)

        if l % 2 == 0:
            e = l // 2
            vv = _norm_mod_call(rows, h, norm_ffn[l], mods, 3, 4, BF16)
            if l + 1 < depth:
                cast = _SideCast(moe_w1_rows, e * n_exp * d, n_exp * d, up_steps)
                hid, w1_bf = _swiglu_up(vv, ffn_w1, ffn_w3, e, side=cast)
            else:
                hid = _swiglu_up(vv, ffn_w1, ffn_w3, e)
            h = _matmul(hid, ffn_w2, e, 0, d, F32, res=h, rows=rows, mods=mods, which=5)
        else:
            e = l // 2
            n_tok = rows.n_lat if l == depth - 1 else rows.m
            h = _moe_ffn(rows, h, norm_ffn[l], mods, router_w[e], w1_bf.reshape(1, n_exp, d, d_ff),
                         w3_bf.reshape(1, n_exp, d, d_ff), w2_bf.reshape(1, n_exp, d_ff, d), 0, n_tok,
                         final_gn=norm_final if l == depth - 1 else None)

    out = h if depth % 2 == 0 else _final_norm(rows, h, norm_final)
    return out.reshape(batch, seq, d)
```

```python
import functools
import math

import jax
import jax.numpy as jnp
from jax import lax
from jax.experimental import pallas as pl
from jax.experimental.pallas import tpu as pltpu

F32 = jnp.float32
BF16 = jnp.bfloat16
I32 = jnp.int32

GRID_W = 64
GLA_HEADS = 4
GLA_DK = 128
GLA_DV = 256
GLA_RANK = 16
GLA_GATE_NORMALIZER = 16.0
MLSTM_HEADS = 4
MLSTM_DQK = 128
MLSTM_DV = 256
GATE_SOFT_CAP = 15.0
N_GATE_B = 4 * MLSTM_HEADS
TOP_K = 2
EPS = 1e-6
QA = GLA_HEADS * GLA_DK
QB = MLSTM_HEADS * MLSTM_DQK
QK_COLS = 2 * QA + 2 * QB
V_A = GLA_HEADS * GLA_DV
V_B = MLSTM_HEADS * MLSTM_DV

LANES = 128
SUBLANES = 8
BF16_ROWS = 16
COPY_GROUP = 8
SCAN_CHUNK = 256
VMEM_LIMIT = 56 * 1024 * 1024
NEG_BIG = -1e30
LOG2_E = 1.4426950408889634
HIGHEST = lax.Precision.HIGHEST


def _cparams(*sem):
    return pltpu.CompilerParams(dimension_semantics=sem, vmem_limit_bytes=VMEM_LIMIT)


def _pick(dim, prefs):
    for p in prefs:
        if dim % p == 0:
            return p
    return dim


def _dot(a, b, precision=None):
    return jnp.dot(a, b, preferred_element_type=F32, precision=precision)


def _dot_nt(a, b, precision=None):
    return lax.dot_general(a, b, (((1,), (1,)), ((), ())), preferred_element_type=F32, precision=precision)


def _dot_tn(a, b):
    return lax.dot_general(a, b, (((0,), (0,)), ((), ())), preferred_element_type=F32)


def _silu(x):
    return x * jax.nn.sigmoid(x)


def _split3(x):
    hi = x.astype(BF16)
    rest = x - hi.astype(F32)
    mid = rest.astype(BF16)
    lo = (rest - mid.astype(F32)).astype(BF16)
    return hi, mid, lo


def _ada_kernel(c_ref, w_ref, b_ref, o_ref):
    o_ref[...] = _dot(_silu(c_ref[...]), w_ref[...], HIGHEST) + b_ref[...]


def _ada_mods(cvec, ada_w, ada_b):
    depth, d, n = ada_w.shape
    tn = _pick(n, (1024, 512, 256, 128))
    return pl.pallas_call(
        _ada_kernel,
        grid=(depth, n // tn),
        in_specs=[pl.BlockSpec(cvec.shape, lambda l, j: (0, 0)),
                  pl.BlockSpec((None, d, tn), lambda l, j: (l, 0, j)),
                  pl.BlockSpec((None, 1, tn), lambda l, j: (l, 0, j))],
        out_specs=pl.BlockSpec((None, cvec.shape[0], tn), lambda l, j: (l, 0, j)),
        out_shape=jax.ShapeDtypeStruct((depth, cvec.shape[0], n), F32),
        compiler_params=_cparams("parallel", "parallel"),
        name="ada_mods",
    )(cvec, ada_w, ada_b.reshape(depth, 1, n))


class _Rows:
    def __init__(self, batch, seq, ctx_len):
        self.batch, self.seq, self.ctx_len = batch, seq, ctx_len
        self.n_lat = batch * seq
        self.n_ctx = batch * ctx_len
        self.m = self.n_lat + self.n_ctx

    def row_tile(self, prefs):
        return _pick(math.gcd(self.seq, self.n_ctx), prefs)

    def gid(self, i, tm):
        r0 = i * tm
        return jnp.where(r0 < self.n_lat, r0 // self.seq, self.batch)


def _mod_spec(rows, which, tm, d):
    return pl.BlockSpec((None, None, 1, d), lambda i: (which, rows.gid(i, tm), 0, 0))


def _norm_mod(x, gn, sh, sc):
    y = x * lax.rsqrt(jnp.mean(x * x, axis=-1, keepdims=True) + EPS) * gn
    return y * (1.0 + sc) + sh


def _norm_mod_kernel(x_ref, gn_ref, sh_ref, sc_ref, o_ref):
    o_ref[...] = _norm_mod(x_ref[...], gn_ref[...], sh_ref[...], sc_ref[...]).astype(o_ref.dtype)


def _norm_mod_router_kernel(x_ref, gn_ref, sh_ref, sc_ref, rw_ref, o_ref, ei_ref, ew_ref, *, n_experts):
    u = _norm_mod(x_ref[...], gn_ref[...], sh_ref[...], sc_ref[...])
    o_ref[...] = u
    uh, um, _ = _split3(u)
    rh, rm, _ = _split3(rw_ref[...])
    logits = _dot(jnp.concatenate([uh, uh, um], axis=1), jnp.concatenate([rh, rm, rh], axis=0))
    lane_i = lax.broadcasted_iota(I32, logits.shape, 1)
    lane = lane_i.astype(F32)
    lg = jnp.where(lane_i < n_experts, logits, -jnp.inf)
    m1 = jnp.max(lg, axis=-1, keepdims=True)
    i1 = jnp.min(jnp.where(lg == m1, lane, float(LANES)), axis=-1, keepdims=True)
    lg2 = jnp.where(lane == i1, -jnp.inf, lg)
    m2 = jnp.max(lg2, axis=-1, keepdims=True)
    i2 = jnp.min(jnp.where(lg2 == m2, lane, float(LANES)), axis=-1, keepdims=True)
    e = jnp.exp(m2 - m1)
    w1 = 1.0 / (1.0 + e)
    w2 = e / (1.0 + e)
    ei_ref[...] = jnp.where(lane_i == 0, i1, jnp.where(lane_i == 1, i2, 0.0)).astype(I32)
    ew_ref[...] = jnp.where(lane_i == 0, w1, jnp.where(lane_i == 1, w2, 0.0))


def _norm_mod_call(rows, x, gn, mods, which_sh, which_sc, out_dtype, router_w=None):
    assert router_w is None or out_dtype == F32
    m, d = x.shape
    tm = rows.row_tile((256, 128, 64, 32, 16, 8))
    in_specs = [pl.BlockSpec((tm, d), lambda i: (i, 0)),
                pl.BlockSpec((1, d), lambda i: (0, 0)),
                _mod_spec(rows, which_sh, tm, d),
                _mod_spec(rows, which_sc, tm, d)]
    row_spec = pl.BlockSpec((tm, d), lambda i: (i, 0))
    if router_w is None:
        return pl.pallas_call(
            _norm_mod_kernel, grid=(m // tm,), in_specs=in_specs, out_specs=row_spec,
            out_shape=jax.ShapeDtypeStruct((m, d), out_dtype),
            compiler_params=_cparams("parallel"), name="norm_mod",
        )(x, gn.reshape(1, d), mods, mods)
    n_experts = router_w.shape[1]
    rw = jnp.zeros((d, LANES), F32).at[:, :n_experts].set(router_w)
    lane_spec = pl.BlockSpec((tm, LANES), lambda i: (i, 0))
    return pl.pallas_call(
        functools.partial(_norm_mod_router_kernel, n_experts=n_experts),
        grid=(m // tm,),
        in_specs=in_specs + [pl.BlockSpec((d, LANES), lambda i: (0, 0))],
        out_specs=[row_spec, lane_spec, lane_spec],
        out_shape=[jax.ShapeDtypeStruct((m, d), F32), jax.ShapeDtypeStruct((m, LANES), I32),
                   jax.ShapeDtypeStruct((m, LANES), F32)],
        compiler_params=_cparams("parallel"), name="norm_mod_router",
    )(x, gn.reshape(1, d), mods, mods, rw)


def _final_norm_kernel(x_ref, gn_ref, o_ref):
    x = x_ref[...]
    o_ref[...] = x * lax.rsqrt(jnp.mean(x * x, axis=-1, keepdims=True) + EPS) * gn_ref[...]


def _final_norm(rows, x, gn):
    d = x.shape[1]
    tm = rows.row_tile((256, 128, 64, 32, 16, 8))
    return pl.pallas_call(
        _final_norm_kernel, grid=(rows.n_lat // tm,),
        in_specs=[pl.BlockSpec((tm, d), lambda i: (i, 0)), pl.BlockSpec((1, d), lambda i: (0, 0))],
        out_specs=pl.BlockSpec((tm, d), lambda i: (i, 0)),
        out_shape=jax.ShapeDtypeStruct((rows.n_lat, d), F32),
        compiler_params=_cparams("parallel"), name="final_norm",
    )(x, gn.reshape(1, d))


def _stash_weights(pairs):
    @pl.when(pl.program_id(1) == 0)
    def _():
        for w_ref, wbuf in pairs:
            wbuf[...] = w_ref[...].astype(BF16)


def _w_spec(kdim, tn, layer, col_blk0, w_rows=False):
    if w_rows:
        return pl.BlockSpec((None, tn, kdim), lambda j, i: (layer, col_blk0 + j, 0))
    return pl.BlockSpec((None, kdim, tn), lambda j, i: (layer, 0, col_blk0 + j))


class _SideCast:
    def __init__(self, src, row0, rows, steps):
        self.src = src
        self.cols = src.shape[1]
        self.n_blocks = next(nb for nb in range(min(steps, rows // BF16_ROWS), 0, -1)
                             if rows % nb == 0 and (rows // nb) % BF16_ROWS == 0)
        self.block_rows = rows // self.n_blocks
        assert row0 % self.block_rows == 0
        self.blk0 = row0 // self.block_rows
        self.out_shape = jax.ShapeDtypeStruct((rows, self.cols), BF16)

    def specs(self, n_inner):
        step = lambda j, i: jnp.minimum(j * n_inner + i, self.n_blocks - 1)
        blk = (self.block_rows, self.cols)
        return (pl.BlockSpec(blk, lambda j, i: (self.blk0 + step(j, i), 0)),
                pl.BlockSpec(blk, lambda j, i: (step(j, i), 0)))


def _mm_kernel(*refs, act, residual, side, w_rows):
    a_ref, w_ref = refs[:2]
    x_ref, g_ref = refs[2:4] if residual else (None, None)
    pos = 4 if residual else 2
    if side:
        side_in, o_ref, side_out, wbuf = refs[pos:pos + 4]
        side_out[...] = side_in[...].astype(BF16)
    else:
        o_ref, wbuf = refs[pos:pos + 2]
    _stash_weights([(w_ref, wbuf)])
    r = _dot_nt(a_ref[...], wbuf[...]) if w_rows else _dot(a_ref[...], wbuf[...])
    if act == "sigmoid":
        r = jax.nn.sigmoid(r)
    if residual:
        r = x_ref[...] + g_ref[...] * r
    o_ref[...] = r.astype(o_ref.dtype)


def _mm_tiles(m, kdim, n, rows):
    tn = _pick(n, (1024, 512, 256, 128) if kdim <= 2048 else (512, 256, 128))
    row_prefs = (1024, 512, 256, 128, 64, 32, 16, 8) if kdim <= 2048 else (512, 256, 128, 64, 32, 16, 8)
    tm = _pick(m, row_prefs) if rows is None else rows.row_tile(row_prefs)
    return tm, tn


def _matmul(a, w, layer, col0, n, out_dtype, *, act=None, res=None, rows=None, mods=None, which=None,
            side=None, w_rows=False):
    m, kdim = a.shape
    tm, tn = _mm_tiles(m, kdim, n, rows)
    assert col0 % tn == 0
    residual = res is not None
    in_specs = [pl.BlockSpec((tm, kdim), lambda j, i: (i, 0)), _w_spec(kdim, tn, layer, col0 // tn, w_rows)]
    args = [a, w]
    if residual:
        in_specs += [pl.BlockSpec((tm, tn), lambda j, i: (i, j)),
                     pl.BlockSpec((None, None, 1, tn), lambda j, i: (which, rows.gid(i, tm), 0, j))]
        args += [res, mods]
    out_specs = [pl.BlockSpec((tm, tn), lambda j, i: (i, j))]
    out_shape = [jax.ShapeDtypeStruct((m, n), out_dtype)]
    if side is not None:
        side_in, side_out = side.specs(m // tm)
        in_specs.append(side_in)
        args.append(side.src)
        out_specs.append(side_out)
        out_shape.append(side.out_shape)
    outs = pl.pallas_call(
        functools.partial(_mm_kernel, act=act, residual=residual, side=side is not None, w_rows=w_rows),
        grid=(n // tn, m // tm),
        in_specs=in_specs,
        out_specs=out_specs,
        out_shape=out_shape,
        scratch_shapes=[pltpu.VMEM((tn, kdim) if w_rows else (kdim, tn), BF16)],
        compiler_params=_cparams("parallel", "arbitrary"),
        name="matmul",
    )(*args)
    return outs[0] if side is None else outs


def _swiglu_kernel(*refs, side):
    if side:
        a_ref, w1_ref, w3_ref, side_in, o_ref, side_out, w1buf, w3buf = refs
        side_out[...] = side_in[...].astype(BF16)
    else:
        a_ref, w1_ref, w3_ref, o_ref, w1buf, w3buf = refs
    _stash_weights([(w1_ref, w1buf), (w3_ref, w3buf)])
    a = a_ref[...]
    o_ref[...] = (_silu(_dot(a, w1buf[...])) * _dot(a, w3buf[...])).astype(o_ref.dtype)


def _swiglu_tiles(m, f):
    return _pick(m, (1024, 512, 256, 128, 64, 32, 16, 8)), _pick(f, (512, 256, 128))


def _swiglu_up(a, w1, w3, layer, side=None):
    m, d = a.shape
    f = w1.shape[2]
    tm, tn = _swiglu_tiles(m, f)
    in_specs = [pl.BlockSpec((tm, d), lambda j, i: (i, 0)), _w_spec(d, tn, layer, 0), _w_spec(d, tn, layer, 0)]
    args = [a, w1, w3]
    out_specs = [pl.BlockSpec((tm, tn), lambda j, i: (i, j))]
    out_shape = [jax.ShapeDtypeStruct((m, f), BF16)]
    if side is not None:
        side_in, side_out = side.specs(m // tm)
        in_specs.append(side_in)
        args.append(side.src)
        out_specs.append(side_out)
        out_shape.append(side.out_shape)
    outs = pl.pallas_call(
        functools.partial(_swiglu_kernel, side=side is not None), grid=(f // tn, m // tm),
        in_specs=in_specs, out_specs=out_specs, out_shape=out_shape,
        scratch_shapes=[pltpu.VMEM((d, tn), BF16), pltpu.VMEM((d, tn), BF16)],
        compiler_params=_cparams("parallel", "arbitrary"), name="swiglu_up",
    )(*args)
    return outs[0] if side is None else outs


def _up_merge_kernel(ha_ref, hb_ref, wa_ref, wb_ref, sga_ref, sgb_ref, o_ref, wabuf, wbbuf):
    _stash_weights([(wa_ref, wabuf), (wb_ref, wbbuf)])
    up_a = _dot(ha_ref[...], wabuf[...])
    up_b = _dot(hb_ref[...], wbbuf[...])
    o_ref[...] = (sga_ref[...] * up_a + sgb_ref[...] * up_b).astype(o_ref.dtype)


def _up_merge(ha, hb, wa, wb, layer, sg):
    m, va = ha.shape
    vb = hb.shape[1]
    d = wa.shape[2]
    tm = _pick(m, (512, 256, 128, 64, 32, 16, 8))
    tn = _pick(d, (1024, 512, 256, 128))
    nj = d // tn
    return pl.pallas_call(
        _up_merge_kernel, grid=(nj, m // tm),
        in_specs=[pl.BlockSpec((tm, va), lambda j, i: (i, 0)),
                  pl.BlockSpec((tm, vb), lambda j, i: (i, 0)),
                  _w_spec(va, tn, layer, 0), _w_spec(vb, tn, layer, 0),
                  pl.BlockSpec((tm, tn), lambda j, i: (i, j)),
                  pl.BlockSpec((tm, tn), lambda j, i: (i, j + nj))],
        out_specs=pl.BlockSpec((tm, tn), lambda j, i: (i, j)),
        out_shape=jax.ShapeDtypeStruct((m, d), BF16),
        scratch_shapes=[pltpu.VMEM((va, tn), BF16), pltpu.VMEM((vb, tn), BF16)],
        compiler_params=_cparams("parallel", "arbitrary"), name="up_merge",
    )(ha, hb, wa, wb, sg, sg)


def _conv_taps(xs, w, dy, not_first, not_last):
    n = xs.shape[0]
    left = jnp.where(not_first, pltpu.roll(xs, 1, 0), 0.0)
    right = jnp.where(not_last, pltpu.roll(xs, n - 1, 0), 0.0)
    return w[3 * dy:3 * dy + 1] * left + w[3 * dy + 1:3 * dy + 2] * xs + w[3 * dy + 2:3 * dy + 3] * right


def _conv_kernel(up_ref, x_ref, dn_ref, w_ref, s_ref, o_ref, pad_ref, *, rows, rb, strip):
    r0 = pl.program_id(0) * rb
    w = w_ref[...]

    @pl.when(r0 < rows.n_lat)
    def _():
        tc = x_ref.shape[-1]
        at_start = lax.rem(r0, rows.seq) == 0
        at_end = lax.rem(r0 + rb, rows.seq) == 0
        pad_ref[pl.ds(0, GRID_W), :] = jnp.where(at_start, 0.0, up_ref[...])
        pad_ref[pl.ds(GRID_W + rb, GRID_W), :] = jnp.where(at_end, 0.0, dn_ref[...])
        pad_ref[pl.ds(GRID_W, rb), :] = x_ref[...]
        col = lax.broadcasted_iota(I32, (strip, tc), 0) % GRID_W
        not_first = col != 0
        not_last = col != GRID_W - 1
        for s in range(rb // strip):
            acc = jnp.zeros((strip, tc), F32)
            for dy in range(3):
                xs = pad_ref[pl.ds(s * strip + dy * GRID_W, strip), :]
                acc = acc + _conv_taps(xs, w, dy, not_first, not_last)
            o_ref[pl.ds(s * strip, strip), :] = _silu(acc) * s_ref[...]

    @pl.when(r0 >= rows.n_lat)
    def _():
        x = x_ref[...]
        pos = lax.broadcasted_iota(I32, x.shape, 0) % rows.ctx_len
        acc = _conv_taps(x, w, 1, pos != 0, pos != rows.ctx_len - 1)
        o_ref[...] = _silu(acc) * s_ref[...]


def _conv_silu(rows, p_qk, conv_w, scale):
    m, c = p_qk.shape
    tc = _pick(c, (512, 256, 128))
    rb = rows.n_ctx
    strip = _pick(rb, (512, 256, 128, 64))
    per = rb // GRID_W
    last = m // GRID_W - 1
    return pl.pallas_call(
        functools.partial(_conv_kernel, rows=rows, rb=rb, strip=strip),
        grid=(m // rb, c // tc),
        in_specs=[pl.BlockSpec((GRID_W, tc), lambda i, j: (jnp.maximum(i * per - 1, 0), j)),
                  pl.BlockSpec((rb, tc), lambda i, j: (i, j)),
                  pl.BlockSpec((GRID_W, tc), lambda i, j: (jnp.minimum((i + 1) * per, last), j)),
                  pl.BlockSpec((9, tc), lambda i, j: (0, j)),
                  pl.BlockSpec((1, tc), lambda i, j: (0, j))],
        out_specs=pl.BlockSpec((rb, tc), lambda i, j: (i, j)),
        out_shape=jax.ShapeDtypeStruct((m, c), F32),
        scratch_shapes=[pltpu.VMEM((rb + 2 * GRID_W, tc), F32)],
        compiler_params=_cparams("parallel", "parallel"), name="conv_silu",
    )(p_qk, p_qk, p_qk, conv_w.reshape(9, c), scale)


def _causal_masks(n):
    t = lax.broadcasted_iota(I32, (n, n), 0)
    s = lax.broadcasted_iota(I32, (n, n), 1)
    return t, s, (s <= t, s >= t)


def _block_ref_rows(x, blk, row):
    n, c = x.shape
    if blk >= SUBLANES:
        x3 = x.reshape(n // blk, blk, c)
        return jnp.broadcast_to(x3[:, row:row + 1, :], x3.shape).reshape(n, c)
    x3 = x.reshape(n // SUBLANES, SUBLANES, c)
    sub = lax.broadcasted_iota(I32, x3.shape, 1)
    out = jnp.zeros_like(x3)
    for g in range(SUBLANES // blk):
        r = g * blk + row
        out = jnp.where(sub // blk == g, jnp.broadcast_to(x3[:, r:r + 1, :], x3.shape), out)
    return out.reshape(n, c)


def _gla_direction(d, q, k, v, lr, wg, bg, s_ref, causal, lev_blk):
    n = q.shape[0]
    hn = n // 2
    lh, lm, _ = _split3(lr)
    wh, wm, _ = _split3(wg)
    z = _dot(jnp.concatenate([lh, lh, lm], axis=1), jnp.concatenate([wh, wm, wh], axis=0)) + bg
    la = jax.nn.log_sigmoid(z) * (LOG2_E / GLA_GATE_NORMALIZER)
    tri = causal.astype(BF16)
    c3 = _dot(tri, jnp.concatenate(_split3(la), axis=1))
    nk = la.shape[1]
    cum = c3[:, :nk] + c3[:, nk:2 * nk] + c3[:, 2 * nk:]
    total = cum[n - 1:n] if d == 0 else cum[0:1]
    state = s_ref[...]
    out = _dot((q * jnp.exp2(cum)).astype(BF16), state.astype(BF16))
    qb = q.astype(BF16)
    kb = k.astype(BF16)
    lo, hi = slice(0, hn), slice(hn, n)
    diag = [jnp.where(lev_blk == -1, _dot_nt(qb[h], kb[h]).astype(BF16), jnp.zeros((), BF16)) for h in (lo, hi)]
    top = n.bit_length() - 2
    for l in range(top + 1):
        half = 1 << l
        ref = _block_ref_rows(cum, 2 * half, half - 1 if d == 0 else half)
        e = jnp.exp2(-jnp.abs(cum - ref)).astype(BF16)
        qe = qb * e
        ke = kb * e
        if l < top:
            diag = [jnp.where(lev_blk == l, _dot_nt(qe[h], ke[h]).astype(BF16), sc)
                    for h, sc in zip((lo, hi), diag)]
        else:
            cross = (_dot_nt(qe[hi], ke[lo]) if d == 0 else _dot_nt(qe[lo], ke[hi])).astype(BF16)
    zero = jnp.zeros((hn, hn), BF16)
    upper, lower = ([diag[0], zero], [cross, diag[1]]) if d == 0 else ([diag[0], cross], [zero, diag[1]])
    scores = jnp.concatenate([jnp.concatenate(upper, axis=1), jnp.concatenate(lower, axis=1)], axis=0)
    out = out + _dot(scores, v)
    k_out = (k * jnp.exp2(total - cum)).astype(BF16)
    dk = state.shape[0]
    et = jnp.broadcast_to(jnp.exp2(total), (dk, dk)).T
    scale = jnp.concatenate([et] * (state.shape[1] // dk), axis=1)
    s_ref[...] = scale * state + _dot_tn(k_out, v)
    return out


def _gla_kernel(qf, kf, vf, smf, qb, kb, vb, smb, wg_ref, bg_ref, of_ref, ob_ref, sf_ref, sb_ref):
    @pl.when(pl.program_id(1) == 0)
    def _():
        sf_ref[...] = jnp.zeros_like(sf_ref)
        sb_ref[...] = jnp.zeros_like(sb_ref)

    n = qf.shape[0]
    _, _, causal = _causal_masks(n)
    t_idx, s_idx, causal_blk = _causal_masks(n // 2)
    lev = 31 - lax.clz(t_idx ^ s_idx)
    dirs = ((qf, kf, vf, smf, of_ref, sf_ref), (qb, kb, vb, smb, ob_ref, sb_ref))
    for d, (q, k, v, sm, o_ref, s_ref) in enumerate(dirs):
        lr = sm[:, d * GLA_RANK:(d + 1) * GLA_RANK]
        lev_blk = jnp.where(causal_blk[d], lev, -2).astype(BF16)
        for hd in range(GLA_HEADS):
            ks = slice(hd * GLA_DK, (hd + 1) * GLA_DK)
            vs = slice(hd * GLA_DV, (hd + 1) * GLA_DV)
            o_ref[:, vs] = _gla_direction(d, q[:, ks], k[:, ks], v[:, vs], lr, wg_ref[d, :, ks], bg_ref[d, :, ks],
                                          s_ref.at[hd], causal[d], lev_blk)


def _soft_cap(z):
    return GATE_SOFT_CAP * jnp.tanh(z * (1.0 / GATE_SOFT_CAP))


def _mlstm_direction(d, head, q, k, v, g_col, g_row, c_ref, m_ref, causal):
    n = q.shape[0]
    tri = causal.astype(BF16)
    i_idx = 2 * MLSTM_HEADS * d + head
    f_idx = i_idx + MLSTM_HEADS
    sel_rows = lax.broadcasted_iota(I32, (3 * N_GATE_B, LANES), 0) % N_GATE_B
    rep = lambda a, idx: _dot(jnp.concatenate(_split3(a), axis=1), (sel_rows == idx).astype(BF16))
    i_rep = rep(g_col, i_idx)
    f_rep = rep(jax.nn.log_sigmoid(g_col), f_idx)
    c3 = _dot(tri, jnp.concatenate(_split3(f_rep), axis=1))
    cum = c3[:, :LANES] + c3[:, LANES:2 * LANES] + c3[:, 2 * LANES:]
    r3 = _dot_nt(jnp.concatenate(_split3(jax.nn.log_sigmoid(g_row)), axis=0), tri)
    cum_r_all = r3[:N_GATE_B] + r3[N_GATE_B:2 * N_GATE_B] + r3[2 * N_GATE_B:]
    sub = lax.broadcasted_iota(I32, (N_GATE_B, 1), 0)
    pick_r = lambda a, idx: jnp.sum(jnp.where(sub == idx, a, 0.0), axis=0, keepdims=True)
    i_r, cum_r = pick_r(g_row, i_idx), pick_r(cum_r_all, f_idx)
    wide = lambda a, reps: jnp.concatenate([a] * reps, axis=1)
    nt = n // LANES
    total = cum[n - 1:n] if d == 0 else cum[0:1]
    m_prev = m_ref[...]
    dlog = jnp.where(causal, wide(cum, nt) - cum_r + i_r, NEG_BIG)
    inter_log = cum + m_prev
    row_max = dlog[:, :LANES]
    for t in range(1, nt):
        row_max = jnp.maximum(row_max, dlog[:, t * LANES:(t + 1) * LANES])
    m_t = jnp.maximum(inter_log, jnp.max(row_max, axis=1, keepdims=True))
    w_inter = jnp.exp(inter_log - m_t)
    qb16 = q.astype(BF16)
    s = _dot_nt(qb16, k.astype(BF16)) * jnp.exp(dlog - wide(m_t, nt))
    dv = v.shape[1]
    v_ext = jnp.concatenate([v, jnp.ones((n, LANES), BF16)], axis=1)
    state = c_ref[...]
    acc = wide(w_inter, dv // LANES + 1) * _dot(qb16, state.astype(BF16)) + _dot(s.astype(BF16), v_ext)
    bound = jnp.maximum(jnp.abs(acc[:, dv:]), jnp.exp(-m_t))
    out = acc[:, :dv] / wide(bound, dv // LANES)
    g = total - cum + i_rep
    m_new = jnp.maximum(total + m_prev, jnp.max(g, axis=0, keepdims=True))
    w_c = jnp.exp(total + m_prev - m_new)
    w_k = jnp.exp(g - m_new)
    c_ref[...] = wide(w_c, dv // LANES + 1) * state + _dot_tn((k * w_k).astype(BF16), v_ext)
    m_ref[...] = m_new
    return out


def _mlstm_kernel(qf, kf, vf, gcf, grf, qb, kb, vb, gcb, grb, brow_ref, bcol_ref,
                  of_ref, ob_ref, cf_ref, cb_ref, mf_ref, mb_ref):
    @pl.when(pl.program_id(1) == 0)
    def _():
        cf_ref[...] = jnp.zeros_like(cf_ref)
        cb_ref[...] = jnp.zeros_like(cb_ref)
        mf_ref[...] = jnp.zeros_like(mf_ref)
        mb_ref[...] = jnp.zeros_like(mb_ref)

    n = qf.shape[0]
    _, _, causal = _causal_masks(n)
    g0 = 2 * GLA_RANK
    dirs = ((qf, kf, vf, gcf, grf, of_ref, cf_ref, mf_ref), (qb, kb, vb, gcb, grb, ob_ref, cb_ref, mb_ref))
    for d, (q, k, v, gc, gr, o_ref, c_ref, m_ref) in enumerate(dirs):
        g_col = _soft_cap(gc[:, g0:g0 + N_GATE_B] + brow_ref[...])
        g_row = _soft_cap(gr[...] + bcol_ref[...])
        for hd in range(MLSTM_HEADS):
            qs = slice(hd * MLSTM_DQK, (hd + 1) * MLSTM_DQK)
            vs = slice(hd * MLSTM_DV, (hd + 1) * MLSTM_DV)
            o_ref[:, vs] = _mlstm_direction(d, hd, q[:, qs], k[:, qs], v[:, vs], g_col, g_row,
                                            c_ref.at[hd], m_ref.at[hd], causal[d])


def _scan_row_maps(rows, chunk):
    lat_chunks = rows.seq // chunk
    ctx_chunks = rows.ctx_len // chunk
    ctx0 = rows.n_lat // chunk

    def fwd(b, s):
        return jnp.where(s < ctx_chunks, ctx0 + b * ctx_chunks + s, b * lat_chunks + (s - ctx_chunks))

    def bwd(b, s):
        return jnp.where(s < ctx_chunks, ctx0 + b * ctx_chunks + (ctx_chunks - 1 - s),
                         b * lat_chunks + (lat_chunks - 1 - (s - ctx_chunks)))

    return fwd, bwd, ctx_chunks + lat_chunks


def _gla_scan(rows, qk, v, small, gate_w, gate_b):
    m = qk.shape[0]
    n = SCAN_CHUNK
    fwd, bwd, steps = _scan_row_maps(rows, n)

    def specs(rmap):
        return [pl.BlockSpec((n, QA), lambda b, s: (rmap(b, s), 0)),
                pl.BlockSpec((n, QA), lambda b, s: (rmap(b, s), 1)),
                pl.BlockSpec((n, V_A), lambda b, s: (rmap(b, s), 0)),
                pl.BlockSpec((n, LANES), lambda b, s: (rmap(b, s), 0))]

    out_spec = lambda rmap: pl.BlockSpec((n, V_A), lambda b, s: (rmap(b, s), 0))
    out_sds = jax.ShapeDtypeStruct((m, V_A), F32)
    state = pltpu.VMEM((GLA_HEADS, GLA_DK, GLA_DV), F32)
    return pl.pallas_call(
        _gla_kernel,
        grid=(rows.batch, steps),
        in_specs=specs(fwd) + specs(bwd) + [
            pl.BlockSpec((2, GLA_RANK, QA), lambda b, s: (0, 0, 0)),
            pl.BlockSpec((2, 1, QA), lambda b, s: (0, 0, 0))],
        out_specs=[out_spec(fwd), out_spec(bwd)],
        out_shape=[out_sds, out_sds],
        scratch_shapes=[state, state],
        compiler_params=_cparams("parallel", "arbitrary"), name="gla_scan",
    )(qk, qk, v, small, qk, qk, v, small, gate_w, gate_b.reshape(2, 1, QA))


def _mlstm_scan(rows, qk, v, small, gates_t, gate_b):
    m = qk.shape[0]
    n = SCAN_CHUNK
    fwd, bwd, steps = _scan_row_maps(rows, n)
    q0 = 2 * QA // QB
    v0 = V_A // V_B

    def specs(rmap):
        return [pl.BlockSpec((n, QB), lambda b, s: (rmap(b, s), q0)),
                pl.BlockSpec((n, QB), lambda b, s: (rmap(b, s), q0 + 1)),
                pl.BlockSpec((n, V_B), lambda b, s: (rmap(b, s), v0)),
                pl.BlockSpec((n, LANES), lambda b, s: (rmap(b, s), 0)),
                pl.BlockSpec((N_GATE_B, n), lambda b, s: (0, rmap(b, s)))]

    out_spec = lambda rmap: pl.BlockSpec((n, V_B), lambda b, s: (rmap(b, s), 0))
    out_sds = jax.ShapeDtypeStruct((m, V_B), F32)
    state = pltpu.VMEM((MLSTM_HEADS, MLSTM_DQK, MLSTM_DV + LANES), F32)
    stab = pltpu.VMEM((MLSTM_HEADS, 1, LANES), F32)
    return pl.pallas_call(
        _mlstm_kernel,
        grid=(rows.batch, steps),
        in_specs=specs(fwd) + specs(bwd) + [
            pl.BlockSpec((1, N_GATE_B), lambda b, s: (0, 0)),
            pl.BlockSpec((N_GATE_B, 1), lambda b, s: (0, 0))],
        out_specs=[out_spec(fwd), out_spec(bwd)],
        out_shape=[out_sds, out_sds],
        scratch_shapes=[state, state, stab, stab],
        compiler_params=_cparams("parallel", "arbitrary"), name="mlstm_scan",
    )(qk, qk, v, small, gates_t, qk, qk, v, small, gates_t,
      gate_b.reshape(1, N_GATE_B), gate_b.reshape(N_GATE_B, 1))


def _mix_prep_kernel(oaf, oab, obf, obb, ro_ref, gna_ref, gnb_ref, ha_ref, hb_ref):
    branches = ((oaf, oab, gna_ref, ha_ref, 0, GLA_HEADS, GLA_DV, _silu),
                (obf, obb, gnb_ref, hb_ref, V_A, MLSTM_HEADS, MLSTM_DV, jax.nn.sigmoid))
    for of, ob, gn_ref, h_ref, off, heads, dv, gate_fn in branches:
        for h in range(heads):
            sl = slice(h * dv, (h + 1) * dv)
            o = of[:, sl] + ob[:, sl]
            y = o * lax.rsqrt(jnp.mean(o * o, axis=-1, keepdims=True) + EPS) * gn_ref[:, sl]
            gate = gate_fn(ro_ref[:, off + h * dv:off + (h + 1) * dv])
            h_ref[:, sl] = (y * gate).astype(h_ref.dtype)


def _mix_prep(oaf, oab, obf, obb, ro, gna, gnb):
    m = ro.shape[0]
    tm = _pick(m, (256, 128, 64, 32, 16, 8))
    row = lambda c: pl.BlockSpec((tm, c), lambda i: (i, 0))
    vec = lambda c: pl.BlockSpec((1, c), lambda i: (0, 0))
    return pl.pallas_call(
        _mix_prep_kernel, grid=(m // tm,),
        in_specs=[row(V_A), row(V_A), row(V_B), row(V_B), row(V_A + V_B), vec(V_A), vec(V_B)],
        out_specs=[row(V_A), row(V_B)],
        out_shape=[jax.ShapeDtypeStruct((m, V_A), BF16), jax.ShapeDtypeStruct((m, V_B), BF16)],
        compiler_params=_cparams("parallel"), name="mix_prep",
    )(oaf, oab, obf, obb, ro, gna.reshape(1, V_A), gnb.reshape(1, V_B))


def _expert_up_kernel(te_ref, na_ref, src0_ref, srcn_ref, v_hbm, w1_ref, w3_ref, o_ref, xbuf, abuf, sem,
                      *, tm, issue_steps):
    del te_ref
    i = pl.program_id(0)
    j = pl.program_id(1)
    slot = lax.rem(i, 2)
    per = tm // issue_steps

    def row_copy(src_ref, grp, t, s):
        return pltpu.make_async_copy(v_hbm.at[pl.ds(src_ref[0, grp * per + t], 1)],
                                     xbuf.at[s, grp, pl.ds(t, 1)], sem.at[s])

    @pl.when(jnp.logical_and(i == 0, j == 0))
    def _():
        def start(grp, carry):
            for t in range(per):
                row_copy(src0_ref, grp, t, 0).start()
            return carry

        lax.fori_loop(0, issue_steps, start, 0)

    @pl.when(j == 0)
    def _():
        def wait(r, carry):
            row_copy(src0_ref, 0, 0, slot).wait()
            return carry

        lax.fori_loop(0, tm, wait, 0, unroll=32)
        abuf[...] = xbuf[slot].reshape(tm, abuf.shape[1]).astype(BF16)

    @pl.when(jnp.logical_and(i + 1 < pl.num_programs(0), j < issue_steps))
    def _():
        for t in range(per):
            row_copy(srcn_ref, j, t, 1 - slot).start()

    @pl.when(i < na_ref[0])
    def _():
        a = abuf[...]
        o_ref[...] = (_silu(_dot(a, w1_ref[...])) * _dot(a, w3_ref[...])).astype(o_ref.dtype)

    @pl.when(i >= na_ref[0])
    def _():
        o_ref[...] = jnp.zeros_like(o_ref)


def _expert_up(v, src, w1, w3, layer, tile_expert, n_active, tm):
    n_tiles = src.shape[0]
    d = v.shape[1]
    f = w1.shape[3]
    tn = _pick(f, (1408, 512, 256, 128))
    nj = f // tn
    issue_steps = max(s for s in (1, 2, 4, 8) if s <= nj)
    wmap = lambda i, j, te, na: (layer, te[i], 0, jnp.where(i < na[0], j, 0))
    return pl.pallas_call(
        functools.partial(_expert_up_kernel, tm=tm, issue_steps=issue_steps),
        grid_spec=pltpu.PrefetchScalarGridSpec(
            num_scalar_prefetch=2, grid=(n_tiles, nj),
            in_specs=[pl.BlockSpec((None, 1, tm), lambda i, j, te, na: (0, 0, 0), memory_space=pltpu.SMEM),
                      pl.BlockSpec((None, 1, tm), lambda i, j, te, na: (jnp.minimum(i + 1, n_tiles - 1), 0, 0),
                                   memory_space=pltpu.SMEM),
                      pl.BlockSpec(memory_space=pl.ANY),
                      pl.BlockSpec((None, None, d, tn), wmap),
                      pl.BlockSpec((None, None, d, tn), wmap)],
            out_specs=pl.BlockSpec((tm, tn), lambda i, j, te, na: (i, j)),
            scratch_shapes=[pltpu.VMEM((2, issue_steps, tm // issue_steps, d), F32), pltpu.VMEM((tm, d), BF16),
                            pltpu.SemaphoreType.DMA((2,))]),
        out_shape=jax.ShapeDtypeStruct((n_tiles * tm, f), BF16),
        compiler_params=_cparams("arbitrary", "arbitrary"), name="expert_up",
    )(tile_expert, n_active, src, src, v, w1, w3)


def _expert_down_kernel(te_ref, na_ref, a_ref, w_ref, o_ref):
    del te_ref
    i = pl.program_id(0)

    @pl.when(i < na_ref[0])
    def _():
        o_ref[...] = _dot(a_ref[...], w_ref[...])

    @pl.when(i >= na_ref[0])
    def _():
        o_ref[...] = jnp.zeros_like(o_ref)


def _expert_down(hs, w2, layer, tile_expert, n_active, tm):
    p, f = hs.shape
    d = w2.shape[3]
    tn = _pick(d, (1024, 512, 256, 128))
    return pl.pallas_call(
        _expert_down_kernel,
        grid_spec=pltpu.PrefetchScalarGridSpec(
            num_scalar_prefetch=2, grid=(p // tm, d // tn),
            in_specs=[pl.BlockSpec((tm, f), lambda i, j, te, na: (jnp.where(i < na[0], i, 0), 0)),
                      pl.BlockSpec((None, None, f, tn),
                                   lambda i, j, te, na: (layer, te[i], 0, jnp.where(i < na[0], j, 0)))],
            out_specs=pl.BlockSpec((tm, tn), lambda i, j, te, na: (i, j))),
        out_shape=jax.ShapeDtypeStruct((p, d), F32),
        compiler_params=_cparams("parallel", "parallel"), name="expert_down",
    )(tile_expert, n_active, hs, w2)


def _moe_combine_kernel(pos_ref, posn_ref, ys_hbm, x_ref, g_ref, w_ref, fn_ref, o_ref, buf, sem,
                        *, tile, final_norm):
    i = pl.program_id(0)
    slot = lax.rem(i, 2)

    def row_copy(p_ref, k, grp, t, s):
        return pltpu.make_async_copy(ys_hbm.at[pl.ds(p_ref[0, k * tile + grp * COPY_GROUP + t], 1)],
                                     buf.at[s, k, grp, pl.ds(t, 1)], sem.at[s])

    def start_tile(p_ref, s):
        for k in range(TOP_K):
            def start(grp, carry):
                for t in range(COPY_GROUP):
                    row_copy(p_ref, k, grp, t, s).start()
                return carry

            lax.fori_loop(0, tile // COPY_GROUP, start, 0)

    @pl.when(i == 0)
    def _():
        start_tile(pos_ref, 0)

    @pl.when(i + 1 < pl.num_programs(0))
    def _():
        start_tile(posn_ref, 1 - slot)

    def wait(r, carry):
        row_copy(pos_ref, 0, 0, 0, slot).wait()
        return carry

    lax.fori_loop(0, TOP_K * tile, wait, 0, unroll=32)
    w = w_ref[...]
    d = x_ref.shape[1]
    y = w[:, 0:1] * buf[slot, 0].reshape(tile, d) + w[:, 1:2] * buf[slot, 1].reshape(tile, d)
    r = x_ref[...] + g_ref[...] * y
    if final_norm:
        r = r * lax.rsqrt(jnp.mean(r * r, axis=-1, keepdims=True) + EPS) * fn_ref[...]
    o_ref[...] = r


def _moe_combine(rows, ys, pos, x, mods, which, ew, m, final_gn):
    d = x.shape[1]
    tile = rows.row_tile((256, 128, 64, 32, 16, 8))
    n = m // tile
    pos_spec = lambda imap: pl.BlockSpec((None, 1, TOP_K * tile), imap, memory_space=pltpu.SMEM)
    final_norm = final_gn is not None
    fn = final_gn.reshape(1, d) if final_norm else jnp.ones((1, d), F32)
    return pl.pallas_call(
        functools.partial(_moe_combine_kernel, tile=tile, final_norm=final_norm),
        grid=(n,),
        in_specs=[pos_spec(lambda i: (i, 0, 0)),
                  pos_spec(lambda i: (jnp.minimum(i + 1, n - 1), 0, 0)),
                  pl.BlockSpec(memory_space=pl.ANY),
                  pl.BlockSpec((tile, d), lambda i: (i, 0)),
                  _mod_spec(rows, which, tile, d),
                  pl.BlockSpec((tile, LANES), lambda i: (i, 0)),
                  pl.BlockSpec((1, d), lambda i: (0, 0))],
        out_specs=pl.BlockSpec((tile, d), lambda i: (i, 0)),
        out_shape=jax.ShapeDtypeStruct((m, d), F32),
        scratch_shapes=[pltpu.VMEM((2, TOP_K, tile // COPY_GROUP, COPY_GROUP, d), F32),
                        pltpu.SemaphoreType.DMA((2,))],
        compiler_params=_cparams("arbitrary"), name="moe_combine",
    )(pos, pos, ys, x, mods, ew, fn)


def _moe_dispatch(e_idx, n_experts, tm, combine_tile):
    m = e_idx.shape[0]
    ex = jnp.concatenate([e_idx[:, 0], e_idx[:, 1]])
    onehot = (ex[:, None] == jnp.arange(n_experts, dtype=I32)[None, :]).astype(I32)
    rank = jnp.sum((jnp.cumsum(onehot, axis=0) - onehot) * onehot, axis=1)
    counts = jnp.sum(onehot, axis=0)
    padded = (counts + tm - 1) // tm * tm
    ends = jnp.cumsum(padded)
    dest = (ends - padded)[ex] + rank
    n_tiles = (TOP_K * m + n_experts * (tm - 1)) // tm
    tok = jnp.concatenate([jnp.arange(m, dtype=I32)] * TOP_K)
    src = jnp.zeros((n_tiles * tm,), I32).at[dest].set(tok).reshape(n_tiles, 1, tm)
    tile_start = jnp.arange(n_tiles, dtype=I32) * tm
    tile_expert = jnp.minimum(jnp.sum((ends[None, :] <= tile_start[:, None]).astype(I32), axis=1), n_experts - 1)
    n_active = (ends[-1] // tm).astype(I32).reshape(1)
    pos = dest.reshape(TOP_K, m // combine_tile, combine_tile).transpose(1, 0, 2).reshape(
        m // combine_tile, 1, TOP_K * combine_tile).astype(I32)
    return src, tile_expert, n_active, pos


def _moe_ffn(rows, x, gn, mods, router_w, w1, w3, w2, layer, n_tok, final_gn):
    n_experts = w1.shape[1]
    v, e_idx, e_w = _norm_mod_call(rows, x, gn, mods, 3, 4, F32, router_w=router_w)
    tm = 512 if TOP_K * n_tok >= 8192 else 64
    combine_tile = rows.row_tile((256, 128, 64, 32, 16, 8))
    src, tile_expert, n_active, pos = _moe_dispatch(e_idx[:n_tok], n_experts, tm, combine_tile)
    hs = _expert_up(v, src, w1, w3, layer, tile_expert, n_active, tm)
    ys = _expert_down(hs, w2, layer, tile_expert, n_active, tm)
    return _moe_combine(rows, ys, pos, x, mods, 5, e_w, n_tok, final_gn)


def kernel(x, c, ctx, c_ctx, ada_w, ada_b, norm_mix, norm_ffn, w_in, conv_w, gla_gate_w, gla_gate_b,
           mlstm_gate_b, gla_out_norm, mlstm_out_norm, w_up_a, w_up_b, w_o, ffn_w1, ffn_w3, ffn_w2,
           router_w, moe_w1, moe_w3, moe_w2, norm_final):
    batch, seq, d = x.shape
    ctx_len = ctx.shape[1]
    depth = ada_w.shape[0]
    assert seq % GRID_W == 0 and seq % SCAN_CHUNK == 0 and ctx_len % SCAN_CHUNK == 0
    assert seq % (batch * ctx_len) == 0 and (batch * ctx_len) % GRID_W == 0
    rows = _Rows(batch, seq, ctx_len)

    h = jnp.concatenate([x.reshape(batch * seq, d), ctx.reshape(batch * ctx_len, d)], axis=0)

    cvec = jnp.zeros((SUBLANES * ((batch + 1 + SUBLANES - 1) // SUBLANES), d), F32)
    cvec = cvec.at[:batch].set(c).at[batch].set(c_ctx)
    mods_all = _ada_mods(cvec, ada_w, ada_b)
    mods_all = mods_all[:, :batch + 1].reshape(depth, batch + 1, 6, 1, d).transpose(0, 2, 1, 3, 4)

    qk_scale = jnp.ones((QK_COLS,), F32)
    qk_scale = qk_scale.at[:QA].set(GLA_DK ** -0.5).at[2 * QA + QB:].set(MLSTM_DQK ** -0.5).reshape(1, QK_COLS)
    c_v, c_ro = QK_COLS, QK_COLS + V_A + V_B
    c_sm = c_ro + V_A + V_B
    c_g = c_sm + 2 * GLA_RANK + N_GATE_B
    n_small = c_g - c_sm
    n_exp, d_ff = moe_w1.shape[1], moe_w1.shape[3]
    moe_w1_rows = moe_w1.reshape(-1, d_ff)
    moe_w3_rows = moe_w3.reshape(-1, d_ff)
    moe_w2_rows = moe_w2.reshape(-1, d)
    gate_tm, gate_tn = _mm_tiles(rows.m, d, 2 * d, None)
    gate_steps = (2 * d // gate_tn) * (rows.m // gate_tm)
    up_tm, up_tn = _swiglu_tiles(rows.m, ffn_w1.shape[2])
    up_steps = (ffn_w1.shape[2] // up_tn) * (rows.m // up_tm)
    w_in_t = jnp.swapaxes(w_in, 1, 2)

    for l in range(depth):
        mods = mods_all[l]
        w_sm = jnp.zeros((1, LANES, d), F32).at[0, :n_small].set(w_in_t[l, c_sm:c_g])
        w_g = w_in_t[l, c_g:][None]

        u = _norm_mod_call(rows, h, norm_mix[l], mods, 0, 1, BF16)
        p_qk = _matmul(u, w_in_t, l, 0, QK_COLS, F32, w_rows=True)
        v = _matmul(u, w_in_t, l, c_v, V_A + V_B, BF16, w_rows=True)
        ro = _matmul(u, w_in_t, l, c_ro, V_A + V_B, F32, w_rows=True)
        small = _matmul(u, w_sm, 0, 0, LANES, F32, w_rows=True)
        if l % 2 == 0 and l + 1 < depth:
            cast = _SideCast(moe_w2_rows, (l // 2) * n_exp * d_ff, n_exp * d_ff, gate_steps)
            sg, w2_bf = _matmul(u, w_g, 0, 0, 2 * d, BF16, act="sigmoid", side=cast, w_rows=True)
        elif l % 2 == 1:
            cast = _SideCast(moe_w3_rows, (l // 2) * n_exp * d, n_exp * d, gate_steps)
            sg, w3_bf = _matmul(u, w_g, 0, 0, 2 * d, BF16, act="sigmoid", side=cast, w_rows=True)
        else:
            sg = _matmul(u, w_g, 0, 0, 2 * d, BF16, act="sigmoid", w_rows=True)
        qk = _conv_silu(rows, p_qk, conv_w[l], qk_scale)
        gates_t = small[:, 2 * GLA_RANK:n_small].T
        oaf, oab = _gla_scan(rows, qk, v, small, gla_gate_w[l], gla_gate_b[l])
        obf, obb = _mlstm_scan(rows, qk, v, small, gates_t, mlstm_gate_b[l])
        ha, hb = _mix_prep(oaf, oab, obf, obb, ro, gla_out_norm[l], mlstm_out_norm[l])
        merged = _up_merge(ha, hb, w_up_a, w_up_b, l, sg)
        h = _matmul(merged, w_o, l, 0, d, F32, res=h, rows=rows, mods=mods, which=2)

        if l % 2 == 0:
            e = l // 2
            vv = _norm_mod_call(rows, h, norm_ffn[l], mods, 3, 4, BF16)
            if l + 1 < depth:
                cast = _SideCast(moe_w1_rows, e * n_exp * d, n_exp * d, up_steps)
                hid, w1_bf = _swiglu_up(vv, ffn_w1, ffn_w3, e, side=cast)
            else:
                hid = _swiglu_up(vv, ffn_w1, ffn_w3, e)
            h = _matmul(hid, ffn_w2, e, 0, d, F32, res=h, rows=rows, mods=mods, which=5)
        else:
            e = l // 2
            n_tok = rows.n_lat if l == depth - 1 else rows.m
            h = _moe_ffn(rows, h, norm_ffn[l], mods, router_w[e], w1_bf.reshape(1, n_exp, d, d_ff),
                         w3_bf.reshape(1, n_exp, d, d_ff), w2_bf.reshape(1, n_exp, d_ff, d), 0, n_tok,
                         final_gn=norm_final if l == depth - 1 else None)

    out = h if depth % 2 == 0 else _final_norm(rows, h, norm_final)
    return out.reshape(batch, seq, d)
```
